```python
import math
import jax, jax.numpy as jnp
from jax import lax
import numpy as np

D_MODEL = 1024
BATCH = 16
SEQ = 4096
DEPTH = 1

CHUNK = 64
Q_BLOCK = 128
MAX_STREAM_OFFSET = 65536

LRU_WIDTH = D_MODEL
LRU_BLOCKS = 8
CONV_WIDTH = 4
LRU_C = 8.0

ATTN_HEADS = 8
ATTN_HEAD_DIM = D_MODEL // (2 * ATTN_HEADS)
ATTN_V_DIM = 2 * ATTN_HEAD_DIM
ATTN_QK_WIDTH = ATTN_HEADS * 2 * ATTN_HEAD_DIM
ATTN_WIDTH = ATTN_HEADS * ATTN_V_DIM
ROPE_THETA = 500000.0
ROT_DIMS = ATTN_HEAD_DIM // 4

MEM_TOKENS = 256
MEM_HEADS = 4
MEM_HEAD_DIM = 128
MEM_WIDTH = MEM_HEADS * MEM_HEAD_DIM

N_GROUPS = 4
EXPERTS_PER_GROUP = 8
N_EXPERTS = N_GROUPS * EXPERTS_PER_GROUP
TOP_K = 2
EXPERT_FF = 512
EXPERT_BLOCK = 128

IN_WIDTHS = (LRU_WIDTH, LRU_WIDTH, ATTN_QK_WIDTH, ATTN_QK_WIDTH, ATTN_WIDTH, D_MODEL, D_MODEL)
EPS = 1e-6
NEG_INF = -1e30

kernel_name = "hybrid_rglru_diffattn_hiermoe_block"


def rms_norm(x, g):
    xf = x.astype(jnp.float32)
    y = xf * lax.rsqrt(jnp.mean(xf * xf, axis=-1, keepdims=True) + EPS)
    return (y * g.astype(jnp.float32)).astype(x.dtype)


def rotary_tables(positions):
    half = ROT_DIMS // 2
    inv_freq = jnp.exp(-math.log(ROPE_THETA) * jnp.arange(half, dtype=jnp.float32) / half)
    ang = positions.astype(jnp.float32)[..., None] * inv_freq
    return jnp.cos(ang)[:, :, None, None, :], jnp.sin(ang)[:, :, None, None, :]


def rope_partial(x, cos, sin):
    half = ROT_DIMS // 2
    xr = x[..., :ROT_DIMS].astype(jnp.float32)
    x1, x2 = xr[..., :half], xr[..., half:]
    rot = jnp.concatenate([x1 * cos - x2 * sin, x2 * cos + x1 * sin], axis=-1)
    return jnp.concatenate([rot.astype(x.dtype), x[..., ROT_DIMS:]], axis=-1)


def causal_depthwise_conv(x, w, b):
    k = w.shape[0]
    y = lax.conv_general_dilated(x, w[:, None, :].astype(x.dtype), window_strides=(1,),
                                 padding=((k - 1, 0),), dimension_numbers=('NWC', 'WIO', 'NWC'),
                                 feature_group_count=x.shape[-1])
    return y + b


def block_diag_linear(x, w, b):
    bsz, s, c = x.shape
    nb, bw, _ = w.shape
    y = jnp.einsum('bsnc,ncd->bsnd', x.reshape(bsz, s, nb, bw), w).reshape(bsz, s, c)
    return y + b


def rg_lru(x, wa, ba, wx, bx, lam):
    r = jax.nn.sigmoid(block_diag_linear(x, wa, ba).astype(jnp.float32))
    i = jax.nn.sigmoid(block_diag_linear(x, wx, bx).astype(jnp.float32))
    log_a = -LRU_C * r * jax.nn.softplus(-lam.astype(jnp.float32))
    a = jnp.exp(log_a)
    b = jnp.sqrt(-jnp.expm1(2.0 * log_a)) * i * x.astype(jnp.float32)

    def combine(left, right):
        a_l, b_l = left
        a_r, b_r = right
        return a_l * a_r, a_r * b_l + b_r

    _, h = lax.associative_scan(combine, (a, b), axis=1)
    return h.astype(x.dtype)


def diff_attention(q, k, v, cos, sin, q_norm, k_norm, lam, lambda_init, subln):
    bsz, s, _ = q.shape
    q = q.reshape(bsz, s, ATTN_HEADS, 2, ATTN_HEAD_DIM)
    k = k.reshape(bsz, s, ATTN_HEADS, 2, ATTN_HEAD_DIM)
    v = v.reshape(bsz, s, ATTN_HEADS, ATTN_V_DIM)
    q = rope_partial(rms_norm(q, q_norm), cos, sin) * (ATTN_HEAD_DIM ** -0.5)
    k = rope_partial(rms_norm(k, k_norm), cos, sin)
    outs = []
    for blk in range(s // Q_BLOCK):
        q0, q1 = blk * Q_BLOCK, (blk + 1) * Q_BLOCK
        sc = jnp.einsum('bqhmd,bkhmd->bhmqk', q[:, q0:q1], k[:, :q1]).astype(jnp.float32)
        mask = (jnp.arange(q1) // CHUNK)[None, :] <= (jnp.arange(q0, q1) // CHUNK)[:, None]
        p = jax.nn.softmax(jnp.where(mask, sc, NEG_INF), axis=-1)
        w = p[:, :, 0] - lam * p[:, :, 1]
        outs.append(jnp.einsum('bhqk,bkhd->bqhd', w.astype(v.dtype), v[:, :q1]))
    o = jnp.concatenate(outs, axis=1)
    o = rms_norm(o, subln) * (1.0 - lambda_init)
    return o.reshape(bsz, s, ATTN_WIDTH)


def memory_cross_attention(x, mem, norm_cx, norm_mem, w_cq, w_ckv, cq_norm, ck_norm, w_co):
    bsz, s, _ = x.shape
    m = mem.shape[1]
    q = (rms_norm(x, norm_cx) @ w_cq).reshape(bsz, s, MEM_HEADS, MEM_HEAD_DIM)
    kv = rms_norm(mem, norm_mem) @ w_ckv
    k = kv[..., :MEM_WIDTH].reshape(bsz, m, MEM_HEADS, MEM_HEAD_DIM)
    v = kv[..., MEM_WIDTH:].reshape(bsz, m, MEM_HEADS, MEM_HEAD_DIM)
    q = rms_norm(q, cq_norm) * (MEM_HEAD_DIM ** -0.5)
    k = rms_norm(k, ck_norm)
    p = jax.nn.softmax(jnp.einsum('bqhd,bkhd->bhqk', q, k).astype(jnp.float32), axis=-1)
    o = jnp.einsum('bhqk,bkhd->bqhd', p.astype(v.dtype), v).reshape(bsz, s, MEM_WIDTH)
    return o @ w_co


def hierarchical_moe(h, w_group, b_group, w_router, b_router, w_gate_up, w_down):
    bsz, s, d = h.shape
    n = bsz * s
    a_tot = n * TOP_K
    hf = h.reshape(n, d)
    gp = jax.nn.softmax((hf @ w_group).astype(jnp.float32) + b_group.astype(jnp.float32), axis=-1)
    gval, gidx = lax.top_k(gp, 1)
    el = ((hf @ w_router).astype(jnp.float32) + b_router.astype(jnp.float32))
    el = el.reshape(n, N_GROUPS, EXPERTS_PER_GROUP)
    el = jnp.take_along_axis(el, gidx[:, :, None], axis=1)[:, 0]
    ev, eidx = lax.top_k(el, TOP_K)
    ew = jax.nn.softmax(ev, axis=-1) * gval
    expert_id = (gidx * EXPERTS_PER_GROUP + eidx).reshape(-1)
    token_id = jnp.repeat(jnp.arange(n, dtype=jnp.int32), TOP_K)
    weight = ew.reshape(-1)
    order = jnp.argsort(expert_id)
    sorted_e = expert_id[order]
    counts = jnp.bincount(expert_id, length=N_EXPERTS)
    padded = (counts + EXPERT_BLOCK - 1) // EXPERT_BLOCK * EXPERT_BLOCK
    start = jnp.cumsum(counts) - counts
    pend = jnp.cumsum(padded)
    pstart = pend - padded
    dest = pstart[sorted_e] + jnp.arange(a_tot) - start[sorted_e]
    p_rows = a_tot + N_EXPERTS * EXPERT_BLOCK
    n_blk = p_rows // EXPERT_BLOCK
    buf_tok = jnp.full((p_rows,), n, jnp.int32).at[dest].set(token_id[order])
    buf_w = jnp.zeros((p_rows,), jnp.float32).at[dest].set(weight[order])
    blk_e = jnp.minimum(jnp.searchsorted(pend, jnp.arange(n_blk) * EXPERT_BLOCK, side='right'),
                        N_EXPERTS - 1)
    h_pad = jnp.concatenate([hf, jnp.zeros((1, d), hf.dtype)], axis=0)
    xs = h_pad[buf_tok].reshape(n_blk, EXPERT_BLOCK, d)

    def expert_block(args):
        xb, e = args
        gu = xb @ w_gate_up[e]
        return (jax.nn.silu(gu[:, :EXPERT_FF]) * gu[:, EXPERT_FF:]) @ w_down[e]

    ys = lax.map(expert_block, (xs, blk_e)).reshape(p_rows, d)
    out = jnp.zeros((n + 1, d), ys.dtype).at[buf_tok].add(ys * buf_w[:, None].astype(ys.dtype))
    return out[:n].reshape(bsz, s, d)


def setup_inputs(seed: int = 0) -> dict:
    key = jax.random.key(seed)
    keys = iter(jax.random.split(key, 48))

    def normal(shape, scale):
        return jax.random.normal(next(keys), shape, jnp.float32) * scale

    def gain(n):
        return 1.0 + normal((DEPTH, n), 0.02)

    bw = LRU_WIDTH // LRU_BLOCKS
    u = jax.random.uniform(next(keys), (DEPTH, LRU_WIDTH), jnp.float32, 0.9, 0.999)
    p = u ** (1.0 / LRU_C)
    lru_lambda = jnp.log(p) - jnp.log1p(-p)
    offset = jax.random.randint(next(keys), (BATCH, 1), 0, MAX_STREAM_OFFSET, jnp.int32)
    positions = offset + jnp.arange(SEQ, dtype=jnp.int32)[None, :]
    return {
        "x": normal((BATCH, SEQ, D_MODEL), 1.0),
        "mem": normal((BATCH, MEM_TOKENS, D_MODEL), 1.0),
        "positions": positions,
        "norm_mix": gain(D_MODEL),
        "w_in": normal((DEPTH, D_MODEL, sum(IN_WIDTHS)), D_MODEL ** -0.5),
        "conv_w": normal((DEPTH, CONV_WIDTH, LRU_WIDTH), CONV_WIDTH ** -0.5),
        "conv_b": normal((DEPTH, LRU_WIDTH), 0.01),
        "lru_wa": normal((DEPTH, LRU_BLOCKS, bw, bw), bw ** -0.5),
        "lru_ba": normal((DEPTH, LRU_WIDTH), 0.01),
        "lru_wx": normal((DEPTH, LRU_BLOCKS, bw, bw), bw ** -0.5),
        "lru_bx": normal((DEPTH, LRU_WIDTH), 0.01),
        "lru_lambda": lru_lambda,
        "w_lru_o": normal((DEPTH, LRU_WIDTH, D_MODEL), LRU_WIDTH ** -0.5),
        "q_norm": gain(ATTN_HEAD_DIM),
        "k_norm": gain(ATTN_HEAD_DIM),
        "lambda_q1": normal((DEPTH, ATTN_HEAD_DIM), 0.1),
        "lambda_k1": normal((DEPTH, ATTN_HEAD_DIM), 0.1),
        "lambda_q2": normal((DEPTH, ATTN_HEAD_DIM), 0.1),
        "lambda_k2": normal((DEPTH, ATTN_HEAD_DIM), 0.1),
        "subln": gain(ATTN_V_DIM),
        "w_attn_o": normal((DEPTH, ATTN_WIDTH, D_MODEL), ATTN_WIDTH ** -0.5),
        "w_out": normal((DEPTH, D_MODEL, D_MODEL), D_MODEL ** -0.5),
        "norm_cx": gain(D_MODEL),
        "norm_mem": gain(D_MODEL),
        "w_cq": normal((DEPTH, D_MODEL, MEM_WIDTH), D_MODEL ** -0.5),
        "w_ckv": normal((DEPTH, D_MODEL, 2 * MEM_WIDTH), D_MODEL ** -0.5),
        "cq_norm": gain(MEM_HEAD_DIM),
        "ck_norm": gain(MEM_HEAD_DIM),
        "w_co": normal((DEPTH, MEM_WIDTH, D_MODEL), MEM_WIDTH ** -0.5),
        "norm_ffn": gain(D_MODEL),
        "w_group": normal((DEPTH, D_MODEL, N_GROUPS), D_MODEL ** -0.5),
        "b_group": normal((DEPTH, N_GROUPS), 0.01),
        "w_router": normal((DEPTH, D_MODEL, N_EXPERTS), D_MODEL ** -0.5),
        "b_router": normal((DEPTH, N_EXPERTS), 0.01),
        "w_gate_up": normal((DEPTH, N_EXPERTS, D_MODEL, 2 * EXPERT_FF), D_MODEL ** -0.5),
        "w_down": normal((DEPTH, N_EXPERTS, EXPERT_FF, D_MODEL), EXPERT_FF ** -0.5),
    }


def reference(x, mem, positions, norm_mix, w_in, conv_w, conv_b, lru_wa, lru_ba, lru_wx, lru_bx,
              lru_lambda, w_lru_o, q_norm, k_norm, lambda_q1, lambda_k1, lambda_q2, lambda_k2,
              subln, w_attn_o, w_out, norm_cx, norm_mem, w_cq, w_ckv, cq_norm, ck_norm, w_co,
              norm_ffn, w_group, b_group, w_router, b_router, w_gate_up, w_down):
    cos, sin = rotary_tables(positions)
    splits = np.cumsum(IN_WIDTHS)[:-1].tolist()
    for layer in range(DEPTH):
        lambda_init = 0.8 - 0.6 * math.exp(-0.3 * layer)
        h = rms_norm(x, norm_mix[layer])
        proj = h @ w_in[layer]
        lru_in, lru_gate, q, k, v, g_lru, g_attn = jnp.split(proj, splits, axis=-1)
        xc = causal_depthwise_conv(lru_in, conv_w[layer], conv_b[layer])
        hr = rg_lru(xc, lru_wa[layer], lru_ba[layer], lru_wx[layer], lru_bx[layer], lru_lambda[layer])
        y_lru = (jax.nn.gelu(lru_gate) * hr) @ w_lru_o[layer]
        lam = (jnp.exp(jnp.sum(lambda_q1[layer].astype(jnp.float32) * lambda_k1[layer].astype(jnp.float32)))
               - jnp.exp(jnp.sum(lambda_q2[layer].astype(jnp.float32) * lambda_k2[layer].astype(jnp.float32)))
               + lambda_init)
        o_attn = diff_attention(q, k, v, cos, sin, q_norm[layer], k_norm[layer], lam, lambda_init,
                                subln[layer])
        y_attn = o_attn @ w_attn_o[layer]
        mixed = jax.nn.sigmoid(g_lru) * y_lru + jax.nn.sigmoid(g_attn) * y_attn
        x = x + mixed @ w_out[layer]
        x = x + memory_cross_attention(x, mem, norm_cx[layer], norm_mem[layer], w_cq[layer],
                                       w_ckv[layer], cq_norm[layer], ck_norm[layer], w_co[layer])
        x = x + hierarchical_moe(rms_norm(x, norm_ffn[layer]), w_group[layer], b_group[layer],
                                 w_router[layer], b_router[layer], w_gate_up[layer], w_down[layer])
    return x
```

```python
import functools
import math

import jax
import jax.numpy as jnp
from jax import lax
from jax.experimental import pallas as pl
from jax.experimental.pallas import tpu as pltpu

F32 = jnp.float32
BF16 = jnp.bfloat16
I32 = jnp.int32

D_MODEL = 1024
CHUNK = 64
LRU_BLOCKS = 8
LRU_BLOCK_WIDTH = D_MODEL // LRU_BLOCKS
CONV_WIDTH = 4
LRU_C = 8.0
ATTN_HEADS = 8
HEAD_DIM = 64
V_DIM = 2 * HEAD_DIM
ROPE_THETA = 500000.0
ROT_DIMS = HEAD_DIM // 4
ROT_HALF = ROT_DIMS // 2
MEM_HEADS = 4
MEM_HEAD_DIM = 128
MEM_WIDTH = MEM_HEADS * MEM_HEAD_DIM
N_GROUPS = 4
EXPERTS_PER_GROUP = 8
N_EXPERTS = N_GROUPS * EXPERTS_PER_GROUP
TOP_K = 2
EXPERT_FF = 512
N_PROJ = 7
EPS = 1e-6
NEG_INF = -1e30

LANES = 128
SUBLANES = 8
V7X_VMEM_BYTES = 64 * 1024 * 1024
MIB = 1024 * 1024


def _vmem_limit(estimate_bytes):
    return int(min(max(estimate_bytes * 3 // 2, 16 * MIB), V7X_VMEM_BYTES - 8 * MIB))


def _params(semantics, vmem_estimate):
    return pltpu.CompilerParams(dimension_semantics=semantics, vmem_limit_bytes=_vmem_limit(vmem_estimate))


def _resident(shape, index_map):
    return pl.BlockSpec(shape, index_map, pipeline_mode=pl.Buffered(1))


def _rms(x, g):
    return x * lax.rsqrt(jnp.mean(x * x, axis=-1, keepdims=True) + EPS) * g


def _lane_tile(x, n):
    return x if n == 1 else jnp.concatenate([x] * n, axis=1)


def _segment_ones():
    r = lax.broadcasted_iota(I32, (LANES, LANES), 0) // HEAD_DIM
    c = lax.broadcasted_iota(I32, (LANES, LANES), 1) // HEAD_DIM
    return (r == c).astype(BF16)


def _qk_post(p, gain, cos_t, sin_lo, sin_hi, seg, scale):
    cols = []
    for c in range(D_MODEL // LANES):
        pc = p[:, c * LANES:(c + 1) * LANES]
        ss = jnp.dot((pc * pc).astype(BF16), seg, preferred_element_type=F32)
        y = pc * lax.rsqrt(ss * (1.0 / HEAD_DIM) + EPS) * gain
        y = y * cos_t + pltpu.roll(y, LANES - ROT_HALF, 1) * sin_lo + pltpu.roll(y, ROT_HALF, 1) * sin_hi
        cols.append((y * scale).astype(BF16))
    return jnp.concatenate(cols, axis=1)


def _in_proj_kernel(x_ref, g_ref, w_ref, qg_ref, kg_ref, cos_ref, slo_ref, shi_ref,
                    lin_ref, lgate_ref, q_ref, k_ref, v_ref, gl_ref, ga_ref):
    h = _rms(x_ref[...], g_ref[...]).astype(BF16)

    def proj(j):
        return jnp.dot(h, w_ref[:, j * D_MODEL:(j + 1) * D_MODEL], preferred_element_type=F32)

    lin_ref[...] = proj(0).astype(BF16)
    lgate_ref[...] = proj(1).astype(BF16)
    seg = _segment_ones()
    cos_t, sin_lo, sin_hi = cos_ref[...], slo_ref[...], shi_ref[...]
    q_ref[...] = _qk_post(proj(2), qg_ref[...], cos_t, sin_lo, sin_hi, seg, HEAD_DIM ** -0.5)
    k_ref[...] = _qk_post(proj(3), kg_ref[...], cos_t, sin_lo, sin_hi, seg, 1.0)
    v_ref[...] = proj(4).astype(BF16)
    gl_ref[...] = proj(5).astype(BF16)
    ga_ref[...] = proj(6).astype(BF16)


def _in_proj(x2d, g, w_in, qg, kg, cos_t, sin_lo, sin_hi, tm):
    n = x2d.shape[0]
    row = lambda i: (i, 0)
    fixed = lambda i: (0, 0)
    tok = pl.BlockSpec((tm, D_MODEL), row)
    tab = pl.BlockSpec((tm, LANES), row)
    est = (w_in.size * 2 + 2 * tm * D_MODEL * 4 + 6 * tm * LANES * 4 + N_PROJ * 2 * tm * D_MODEL * 2
           + 4 * tm * D_MODEL * 4)
    return pl.pallas_call(
        _in_proj_kernel,
        out_shape=[jax.ShapeDtypeStruct((n, D_MODEL), BF16)] * N_PROJ,
        grid=(n // tm,),
        in_specs=[tok, _resident((1, D_MODEL), fixed), _resident(w_in.shape, fixed),
                  _resident((1, LANES), fixed), _resident((1, LANES), fixed), tab, tab, tab],
        out_specs=[tok] * N_PROJ,
        compiler_params=_params(("parallel",), est),
        name="in_proj",
    )(x2d, g, w_in, qg, kg, cos_t, sin_lo, sin_hi)


def _lru_kernel(lin_ref, lgate_ref, gl_ref, cw_ref, cb_ref, wax_ref, bax_ref, lam_ref, wo_ref,
                out_ref, xprev_ref, hprev_ref, a_ref, b_ref, h_ref):
    t = pl.program_id(1)

    @pl.when(t == 0)
    def _():
        xprev_ref[...] = jnp.zeros_like(xprev_ref)
        hprev_ref[...] = jnp.zeros_like(hprev_ref)

    x = lin_ref[...].astype(F32)
    tt = x.shape[0]
    nblk = tt // SUBLANES
    xp = jnp.concatenate([xprev_ref[...], x], axis=0)
    cw = cw_ref[...]
    xc = cb_ref[...] + cw[3:4] * x
    for j in range(1, CONV_WIDTH):
        xc = xc + cw[CONV_WIDTH - 1 - j:CONV_WIDTH - j] * xp[SUBLANES - j:SUBLANES - j + tt]
    xprev_ref[...] = x[tt - SUBLANES:tt]

    xcb = xc.astype(BF16)
    ra, ri = [], []
    for n in range(LRU_BLOCKS):
        g = jnp.dot(xcb[:, n * LRU_BLOCK_WIDTH:(n + 1) * LRU_BLOCK_WIDTH], wax_ref[n], preferred_element_type=F32)
        ra.append(g[:, :LRU_BLOCK_WIDTH])
        ri.append(g[:, LRU_BLOCK_WIDTH:])
    bax = bax_ref[...]
    r = jax.nn.sigmoid(jnp.concatenate(ra, axis=1) + bax[0:1])
    i = jax.nn.sigmoid(jnp.concatenate(ri, axis=1) + bax[1:2])
    log_a = -LRU_C * r * jax.nn.softplus(-lam_ref[...])
    a = jnp.exp(log_a)
    b = jnp.sqrt(-jnp.tanh(log_a) * (a * a + 1.0)) * i * xc

    a3 = a.reshape(nblk, SUBLANES, D_MODEL)
    b3 = b.reshape(nblk, SUBLANES, D_MODEL)
    sub = lax.broadcasted_iota(I32, (nblk, SUBLANES, D_MODEL), 1)
    shift = 1
    while shift < SUBLANES:
        keep = sub >= shift
        a_sh = pltpu.roll(a3, shift, 1)
        b_sh = pltpu.roll(b3, shift, 1)
        b3 = jnp.where(keep, a3 * b_sh + b3, b3)
        a3 = jnp.where(keep, a3 * a_sh, a3)
        shift *= 2
    a_ref[...] = a3
    b_ref[...] = b3

    def carry_step(blk, h_last):
        h = a_ref[blk] * h_last + b_ref[blk]
        h_ref[blk] = h
        return jnp.broadcast_to(h[SUBLANES - 1:SUBLANES], (SUBLANES, D_MODEL))

    hprev_ref[...] = lax.fori_loop(0, nblk, carry_step, hprev_ref[...], unroll=4)
    hr = h_ref[...].reshape(tt, D_MODEL)

    y = (jax.nn.gelu(lgate_ref[...].astype(F32)) * hr).astype(BF16)
    yl = jnp.dot(y, wo_ref[...], preferred_element_type=F32)
    out_ref[...] = (jax.nn.sigmoid(gl_ref[...].astype(F32)) * yl).astype(BF16)


def _lru(lin, lgate, gl, cw, cb, wax, bax, lam, wo, tt):
    bsz, s, _ = lin.shape
    seq = pl.BlockSpec((None, tt, D_MODEL), lambda b, t: (b, t, 0))
    fix2 = lambda b, t: (0, 0)
    fix3 = lambda b, t: (0, 0, 0)
    nblk = tt // SUBLANES
    est = 4 * 2 * tt * D_MODEL * 2 + wo.size * 2 + wax.size * 2 + 3 * tt * D_MODEL * 4 + 10 * tt * D_MODEL * 4
    return pl.pallas_call(
        _lru_kernel,
        out_shape=jax.ShapeDtypeStruct((bsz, s, D_MODEL), BF16),
        grid=(bsz, s // tt),
        in_specs=[seq, seq, seq, _resident(cw.shape, fix2), _resident(cb.shape, fix2), _resident(wax.shape, fix3),
                  _resident(bax.shape, fix2), _resident(lam.shape, fix2), _resident(wo.shape, fix2)],
        out_specs=seq,
        scratch_shapes=[pltpu.VMEM((SUBLANES, D_MODEL), F32), pltpu.VMEM((SUBLANES, D_MODEL), F32),
                        pltpu.VMEM((nblk, SUBLANES, D_MODEL), F32), pltpu.VMEM((nblk, SUBLANES, D_MODEL), F32),
                        pltpu.VMEM((nblk, SUBLANES, D_MODEL), F32)],
        compiler_params=_params(("parallel", "arbitrary"), est),
        name="lru",
    )(lin, lgate, gl, cw, cb, wax, bax, lam, wo)


def _attn_kernel(lam_ref, q_ref, k_ref, v_ref, sub_ref, o_ref, m_ref, l_ref, acc_ref, *, tq, out_scale):
    i = pl.program_id(2)
    q = q_ref[...]
    lane = lax.broadcasted_iota(I32, (tq, LANES), 1)
    zero = jnp.zeros_like(q)
    qz = jnp.concatenate([jnp.where(lane < HEAD_DIM, q, zero), jnp.where(lane >= HEAD_DIM, q, zero)], axis=0)
    m_ref[...] = jnp.full_like(m_ref, NEG_INF)
    l_ref[...] = jnp.zeros_like(l_ref)
    acc_ref[...] = jnp.zeros_like(acc_ref)
    ntile = tq // LANES

    def step(off, mask):
        kb = k_ref[pl.ds(off, tq), :]
        vb = v_ref[pl.ds(off, tq), :]
        s = lax.dot_general(qz, kb, (((1,), (1,)), ((), ())), preferred_element_type=F32)
        if mask is not None:
            s = jnp.where(mask, s, NEG_INF)
        m_prev = m_ref[...]
        m_new = jnp.maximum(m_prev, jnp.max(s, axis=1, keepdims=True))
        alpha = jnp.exp(m_prev - m_new)
        p = jnp.exp(s - _lane_tile(m_new, ntile))
        l_ref[...] = alpha * l_ref[...] + jnp.sum(p, axis=1, keepdims=True)
        acc_ref[...] = alpha * acc_ref[...] + jnp.dot(p.astype(BF16), vb, preferred_element_type=F32)
        m_ref[...] = m_new

    def full_block(j, carry):
        step(pl.multiple_of(j * tq, tq), None)
        return carry

    lax.fori_loop(0, i, full_block, 0)
    r = lax.broadcasted_iota(I32, (2 * tq, tq), 0)
    c = lax.broadcasted_iota(I32, (2 * tq, tq), 1)
    r = jnp.where(r >= tq, r - tq, r)
    step(pl.multiple_of(i * tq, tq), (c // CHUNK) <= (r // CHUNK))

    o12 = acc_ref[...] / l_ref[...]
    o = o12[:tq] - lam_ref[0, 0] * o12[tq:]
    o = _rms(o, sub_ref[...]) * out_scale
    o_ref[...] = o.astype(BF16)


def _diff_attn(lam, q, k, v, sub, lambda_init, tq):
    bsz, s, _ = q.shape
    qspec = pl.BlockSpec((None, tq, V_DIM), lambda b, h, i: (b, i, h))
    kvspec = pl.BlockSpec((None, s, V_DIM), lambda b, h, i: (b, 0, h))
    est = 2 * 2 * s * V_DIM * 2 + 4 * tq * V_DIM * 2 + 3 * 2 * tq * LANES * 4 + 4 * 2 * tq * tq * 4
    return pl.pallas_call(
        functools.partial(_attn_kernel, tq=tq, out_scale=1.0 - lambda_init),
        out_shape=jax.ShapeDtypeStruct((bsz, s, ATTN_HEADS * V_DIM), BF16),
        grid=(bsz, ATTN_HEADS, s // tq),
        in_specs=[pl.BlockSpec(memory_space=pltpu.SMEM), qspec, kvspec, kvspec,
                  pl.BlockSpec((1, V_DIM), lambda b, h, i: (0, 0))],
        out_specs=qspec,
        scratch_shapes=[pltpu.VMEM((2 * tq, LANES), F32)] * 3,
        compiler_params=_params(("parallel", "parallel", "parallel"), est),
        name="diff_attn",
    )(lam, q, k, v, sub)


def _mix_out_kernel(x_ref, o_ref, ga_ref, ml_ref, wao_ref, wout_ref, x1_ref):
    ya = jnp.dot(o_ref[...], wao_ref[...], preferred_element_type=F32)
    mixed = ml_ref[...].astype(F32) + jax.nn.sigmoid(ga_ref[...].astype(F32)) * ya
    x1_ref[...] = x_ref[...] + jnp.dot(mixed.astype(BF16), wout_ref[...], preferred_element_type=F32)


def _mix_out(x2d, o, ga, ml, wao, wout, tm):
    n = x2d.shape[0]
    tok = pl.BlockSpec((tm, D_MODEL), lambda i: (i, 0))
    fixed = lambda i: (0, 0)
    est = 2 * 2 * tm * D_MODEL * 4 + 3 * 2 * tm * D_MODEL * 2 + 2 * D_MODEL * D_MODEL * 2 + 3 * tm * D_MODEL * 4
    return pl.pallas_call(
        _mix_out_kernel,
        out_shape=jax.ShapeDtypeStruct((n, D_MODEL), F32),
        grid=(n // tm,),
        in_specs=[tok, tok, tok, tok, _resident(wao.shape, fixed), _resident(wout.shape, fixed)],
        out_specs=tok,
        compiler_params=_params(("parallel",), est),
        name="mix_out",
    )(x2d, o, ga, ml, wao, wout)


def _mem_kv_kernel(mem_ref, g_ref, w_ref, ckg_ref, k_ref, v_ref):
    h = _rms(mem_ref[...], g_ref[...]).astype(BF16)
    kv = jnp.dot(h, w_ref[...], preferred_element_type=F32)
    ks = [_rms(kv[:, hd * MEM_HEAD_DIM:(hd + 1) * MEM_HEAD_DIM], ckg_ref[...]) for hd in range(MEM_HEADS)]
    k_ref[...] = jnp.concatenate(ks, axis=1).astype(BF16)
    v_ref[...] = kv[:, MEM_WIDTH:].astype(BF16)


def _mem_kv(mem, g, w, ckg):
    bsz, m, _ = mem.shape
    fixed = lambda b: (0, 0)
    out = pl.BlockSpec((None, m, MEM_WIDTH), lambda b: (b, 0, 0))
    est = 2 * m * D_MODEL * 4 + w.size * 2 + 4 * m * MEM_WIDTH * 2 + 4 * m * D_MODEL * 4
    return pl.pallas_call(
        _mem_kv_kernel,
        out_shape=[jax.ShapeDtypeStruct((bsz, m, MEM_WIDTH), BF16)] * 2,
        grid=(bsz,),
        in_specs=[pl.BlockSpec((None, m, D_MODEL), lambda b: (b, 0, 0)), _resident((1, D_MODEL), fixed),
                  _resident(w.shape, fixed), _resident((1, MEM_HEAD_DIM), fixed)],
        out_specs=[out, out],
        compiler_params=_params(("parallel",), est),
        name="mem_kv",
    )(mem, g, w, ckg)


def _split_bf16(x):
    hi = x.astype(BF16)
    return hi, (x - hi.astype(F32)).astype(BF16)


def _cross_router_kernel(x1_ref, gcx_ref, wcq_ref, cqg_ref, kc_ref, vc_ref, wco_ref, gffn_ref, wrh_ref, wrl_ref,
                         br_ref, x2_ref, hn_ref, eid_ref, ew_ref):
    x1 = x1_ref[...]
    q = jnp.dot(_rms(x1, gcx_ref[...]).astype(BF16), wcq_ref[...], preferred_element_type=F32)
    outs = []
    for hd in range(MEM_HEADS):
        sl = slice(hd * MEM_HEAD_DIM, (hd + 1) * MEM_HEAD_DIM)
        qh = _rms(q[:, sl], cqg_ref[...]) * MEM_HEAD_DIM ** -0.5
        s = lax.dot_general(qh.astype(BF16), kc_ref[:, sl], (((1,), (1,)), ((), ())), preferred_element_type=F32)
        p = jnp.exp(s - jnp.max(s, axis=1, keepdims=True))
        o = jnp.dot(p.astype(BF16), vc_ref[:, sl], preferred_element_type=F32)
        outs.append(o / jnp.sum(p, axis=1, keepdims=True))
    x2 = x1 + jnp.dot(jnp.concatenate(outs, axis=1).astype(BF16), wco_ref[...], preferred_element_type=F32)
    x2_ref[...] = x2

    hn = _rms(x2, gffn_ref[...])
    hn_ref[...] = hn
    h_hi, h_lo = _split_bf16(hn)
    logits = (jnp.dot(h_hi, wrh_ref[...], preferred_element_type=F32)
              + jnp.dot(h_lo, wrh_ref[...], preferred_element_type=F32)
              + jnp.dot(h_hi, wrl_ref[...], preferred_element_type=F32)) + br_ref[...]
    lane = lax.broadcasted_iota(I32, logits.shape, 1)
    is_group = lane < N_GROUPS
    gl = jnp.where(is_group, logits, NEG_INF)
    gmax = jnp.max(gl, axis=1, keepdims=True)
    gval = 1.0 / jnp.sum(jnp.where(is_group, jnp.exp(gl - gmax), 0.0), axis=1, keepdims=True)
    gidx = jnp.min(jnp.where(gl == gmax, lane, LANES), axis=1, keepdims=True)
    lane_group = lax.shift_right_logical(lane + (EXPERTS_PER_GROUP - N_GROUPS), 3) - 1
    chosen = lane_group == gidx
    el = jnp.where(chosen, logits, NEG_INF)
    v1 = jnp.max(el, axis=1, keepdims=True)
    i1 = jnp.min(jnp.where(chosen & (el == v1), lane, LANES), axis=1, keepdims=True)
    rest = chosen & (lane != i1)
    el2 = jnp.where(rest, logits, NEG_INF)
    v2 = jnp.max(el2, axis=1, keepdims=True)
    i2 = jnp.min(jnp.where(rest & (el2 == v2), lane, LANES), axis=1, keepdims=True)
    t = jnp.exp(v2 - v1)
    w1 = gval / (1.0 + t)
    w2 = gval * t / (1.0 + t)
    eid_ref[...] = jnp.where(lane == 0, i1 - N_GROUPS, jnp.where(lane == 1, i2 - N_GROUPS, 0))
    ew_ref[...] = jnp.where(lane == 0, w1, jnp.where(lane == 1, w2, 0.0))


def _cross_router(x1, gcx, wcq, cqg, kc, vc, wco, gffn, wrh, wrl, br, tm):
    bsz, s, _ = x1.shape
    m = kc.shape[1]
    tok = pl.BlockSpec((None, tm, D_MODEL), lambda b, i: (b, i, 0))
    small = pl.BlockSpec((None, tm, LANES), lambda b, i: (b, i, 0))
    memb = pl.BlockSpec((None, m, MEM_WIDTH), lambda b, i: (b, 0, 0))
    fixed = lambda b, i: (0, 0)
    est = (3 * 2 * tm * D_MODEL * 4 + 2 * 2 * tm * LANES * 4 + 2 * 2 * m * MEM_WIDTH * 2 + 2 * D_MODEL * MEM_WIDTH * 2
           + 2 * D_MODEL * LANES * 2 + 6 * tm * D_MODEL * 4)
    return pl.pallas_call(
        _cross_router_kernel,
        out_shape=[jax.ShapeDtypeStruct((bsz, s, D_MODEL), F32), jax.ShapeDtypeStruct((bsz, s, D_MODEL), F32),
                   jax.ShapeDtypeStruct((bsz, s, LANES), I32), jax.ShapeDtypeStruct((bsz, s, LANES), F32)],
        grid=(bsz, s // tm),
        in_specs=[tok, _resident((1, D_MODEL), fixed), _resident(wcq.shape, fixed), _resident((1, MEM_HEAD_DIM), fixed),
                  memb, memb, _resident(wco.shape, fixed), _resident((1, D_MODEL), fixed),
                  _resident(wrh.shape, fixed), _resident(wrl.shape, fixed), _resident((1, LANES), fixed)],
        out_specs=[tok, tok, small, small],
        compiler_params=_params(("parallel", "parallel"), est),
        name="cross_router",
    )(x1, gcx, wcq, cqg, kc, vc, wco, gffn, wrh, wrl, br)


def _expert_kernel(blk_e_ref, blk_n_ref, idx_ref, idx_next_ref, hn_hbm, wgu_ref, wd_ref, ys_hbm,
                   xbuf, ybuf, gsem, ssem, *, eb, n_assign, plane):
    i = pl.program_id(0)
    last = pl.num_programs(0) - 1
    slot = i % 2

    def start_gather(idx, s):
        def issue(r, carry):
            flat = idx[0, 0, r]
            tok = jnp.where(flat < n_assign, lax.shift_right_logical(flat, 1), 0)
            pltpu.make_async_copy(hn_hbm.at[pl.ds(tok, 1), :], xbuf.at[s, pl.ds(r, 1), :], gsem.at[s]).start()
            return carry
        lax.fori_loop(0, eb, issue, 0, unroll=8)

    def wait_gather(s):
        pltpu.make_async_copy(hn_hbm.at[pl.ds(0, eb), :], xbuf.at[s], gsem.at[s]).wait()

    def wait_scatter():
        pltpu.make_async_copy(ybuf, ys_hbm.at[pl.ds(0, eb), :], ssem).wait()

    @pl.when((i == 0) & (blk_n_ref[0] > 0))
    def _():
        start_gather(idx_ref, 0)

    @pl.when((i < last) & (blk_n_ref[jnp.minimum(i + 1, last)] > 0))
    def _():
        start_gather(idx_next_ref, 1 - slot)

    @pl.when(blk_n_ref[i] > 0)
    def _():
        wait_gather(slot)
        x = xbuf[slot].astype(BF16)
        gu = jnp.dot(x, wgu_ref[0], preferred_element_type=F32)
        act = (jax.nn.silu(gu[:, :EXPERT_FF]) * gu[:, EXPERT_FF:]).astype(BF16)
        y = jnp.dot(act, wd_ref[0], preferred_element_type=F32)

        @pl.when((i > 0) & (blk_n_ref[jnp.maximum(i - 1, 0)] > 0))
        def _():
            wait_scatter()

        ybuf[...] = y

        def issue(r, carry):
            flat = idx_ref[0, 0, r]
            dst = jnp.where(flat < n_assign, (flat & 1) * plane + lax.shift_right_logical(flat, 1), plane - eb + r)
            pltpu.make_async_copy(ybuf.at[pl.ds(r, 1), :], ys_hbm.at[pl.ds(dst, 1), :], ssem).start()
            return carry
        lax.fori_loop(0, eb, issue, 0, unroll=8)

        @pl.when(i == last)
        def _():
            wait_scatter()

    @pl.when((blk_n_ref[i] == 0) & (i > 0) & (blk_n_ref[jnp.maximum(i - 1, 0)] > 0))
    def _():
        wait_scatter()


def _experts(blk_e, blk_n, idx3, hn2d, wgu, wd, eb, n_assign, plane):
    n_blk = blk_e.shape[0]
    idx_spec = pl.BlockSpec((1, 1, eb), lambda i, be, bn: (i, 0, 0), memory_space=pltpu.SMEM)
    idx_next_spec = pl.BlockSpec((1, 1, eb), lambda i, be, bn: (jnp.minimum(i + 1, n_blk - 1), 0, 0),
                                 memory_space=pltpu.SMEM)
    est = 2 * eb * D_MODEL * 4 + eb * D_MODEL * 4 + 2 * (wgu.shape[1] * wgu.shape[2] + wd.shape[1] * wd.shape[2]) * 2 \
        + 4 * eb * D_MODEL * 4
    grid_spec = pltpu.PrefetchScalarGridSpec(
        num_scalar_prefetch=2,
        grid=(n_blk,),
        in_specs=[idx_spec, idx_next_spec, pl.BlockSpec(memory_space=pl.ANY),
                  pl.BlockSpec((1,) + wgu.shape[1:], lambda i, be, bn: (be[i], 0, 0)),
                  pl.BlockSpec((1,) + wd.shape[1:], lambda i, be, bn: (be[i], 0, 0))],
        out_specs=pl.BlockSpec(memory_space=pl.ANY),
        scratch_shapes=[pltpu.VMEM((2, eb, D_MODEL), F32), pltpu.VMEM((eb, D_MODEL), F32),
                        pltpu.SemaphoreType.DMA((2,)), pltpu.SemaphoreType.DMA],
    )
    return pl.pallas_call(
        functools.partial(_expert_kernel, eb=eb, n_assign=n_assign, plane=plane),
        out_shape=jax.ShapeDtypeStruct((TOP_K * plane - eb, D_MODEL), F32),
        grid_spec=grid_spec,
        compiler_params=_params(("arbitrary",), est),
        name="experts",
    )(blk_e, blk_n, idx3, idx3, hn2d, wgu, wd)


def _combine_kernel(x2_ref, ew_ref, y0_ref, y1_ref, o_ref):
    ew = ew_ref[...]
    o_ref[...] = x2_ref[...] + (ew[:, 0:1] * y0_ref[...] + ew[:, 1:2] * y1_ref[...])


def _combine(x2, ew, ys, plane, tm):
    n = x2.shape[0]
    tok = pl.BlockSpec((tm, D_MODEL), lambda i: (i, 0))
    shift = plane // tm
    est = 4 * 2 * tm * D_MODEL * 4 + 2 * tm * LANES * 4 + 2 * tm * D_MODEL * 4
    return pl.pallas_call(
        _combine_kernel,
        out_shape=jax.ShapeDtypeStruct((n, D_MODEL), F32),
        grid=(n // tm,),
        in_specs=[tok, pl.BlockSpec((tm, LANES), lambda i: (i, 0)), tok,
                  pl.BlockSpec((tm, D_MODEL), lambda i: (i + shift, 0))],
        out_specs=tok,
        compiler_params=_params(("parallel",), est),
        name="combine",
    )(x2, ew, ys, ys)


def _rope_tables(positions):
    inv_freq = jnp.exp(-math.log(ROPE_THETA) * jnp.arange(ROT_HALF, dtype=F32) / ROT_HALF)
    ang = positions.astype(F32).reshape(-1, 1) * inv_freq
    cos, sin = jnp.cos(ang), jnp.sin(ang)
    n = ang.shape[0]
    rest = HEAD_DIM - ROT_DIMS
    cos_seg = jnp.concatenate([cos, cos, jnp.ones((n, rest), F32)], axis=1)
    lo_seg = jnp.concatenate([-sin, jnp.zeros((n, HEAD_DIM - ROT_HALF), F32)], axis=1)
    hi_seg = jnp.concatenate([jnp.zeros((n, ROT_HALF), F32), sin, jnp.zeros((n, rest), F32)], axis=1)
    rep = LANES // HEAD_DIM
    return jnp.tile(cos_seg, (1, rep)), jnp.tile(lo_seg, (1, rep)), jnp.tile(hi_seg, (1, rep))


def _route(eid, n_tok, eb):
    n_assign = eid.shape[0]
    order = jnp.argsort(eid).astype(I32)
    sorted_e = eid[order]
    counts = jnp.bincount(eid, length=N_EXPERTS).astype(I32)
    padded = (counts + eb - 1) // eb * eb
    start = jnp.cumsum(counts) - counts
    pend = jnp.cumsum(padded)
    pstart = pend - padded
    dest = pstart[sorted_e] + jnp.arange(n_assign, dtype=I32) - start[sorted_e]
    p_rows = n_assign + N_EXPERTS * eb
    n_blk = p_rows // eb
    buf_flat = jnp.full((p_rows,), n_assign, I32).at[dest].set(order)
    blk_first = jnp.arange(n_blk, dtype=I32) * eb
    blk_e = jnp.minimum(jnp.searchsorted(pend, blk_first, side='right'), N_EXPERTS - 1).astype(I32)
    blk_n = jnp.clip(pstart[blk_e] + counts[blk_e] - blk_first, 0, eb).astype(I32)
    return blk_e, blk_n, buf_flat.reshape(n_blk, 1, eb)


def _tile(n, pref):
    t = min(n, pref)
    assert n % t == 0, (n, pref)
    return t


def kernel(x, mem, positions, norm_mix, w_in, conv_w, conv_b, lru_wa, lru_ba, lru_wx, lru_bx, lru_lambda, w_lru_o, q_norm, k_norm, lambda_q1, lambda_k1, lambda_q2, lambda_k2, subln, w_attn_o, w_out, norm_cx, norm_mem, w_cq, w_ckv, cq_norm, ck_norm, w_co, norm_ffn, w_group, b_group, w_router, b_router, w_gate_up, w_down):
    bsz, s, d = x.shape
    assert d == D_MODEL and w_in.shape[-1] == N_PROJ * D_MODEL
    n = bsz * s
    depth = w_in.shape[0]
    tm = _tile(n, 512)
    ts = _tile(s, 512)
    tq = _tile(s, 512)
    assert tq % CHUNK == 0 and ts % SUBLANES == 0
    eb = _tile(n, 512)
    plane = n + eb
    cos_t, sin_lo, sin_hi = _rope_tables(positions)
    row = lambda v: v.reshape(1, -1).astype(F32)
    rep = LANES // HEAD_DIM

    for layer in range(depth):
        lambda_init = 0.8 - 0.6 * math.exp(-0.3 * layer)
        lin, lgate, q, k, v, gl, ga = _in_proj(
            x.reshape(n, d), row(norm_mix[layer]), w_in[layer].astype(BF16),
            jnp.tile(row(q_norm[layer]), (1, rep)), jnp.tile(row(k_norm[layer]), (1, rep)), cos_t, sin_lo, sin_hi, tm)
        seq = lambda a: a.reshape(bsz, s, d)

        wax = jnp.concatenate([lru_wa[layer], lru_wx[layer]], axis=-1).astype(BF16)
        bax = jnp.stack([lru_ba[layer], lru_bx[layer]]).astype(F32)
        ml = _lru(seq(lin), seq(lgate), seq(gl), conv_w[layer].astype(F32), row(conv_b[layer]), wax, bax,
                  row(lru_lambda[layer]), w_lru_o[layer].astype(BF16), ts)

        lam = (jnp.exp(jnp.sum(lambda_q1[layer].astype(F32) * lambda_k1[layer].astype(F32)))
               - jnp.exp(jnp.sum(lambda_q2[layer].astype(F32) * lambda_k2[layer].astype(F32))) + lambda_init)
        o = _diff_attn(lam.reshape(1, 1), seq(q), seq(k), seq(v), row(subln[layer]), lambda_init, tq)

        x1 = _mix_out(x.reshape(n, d), o.reshape(n, d), ga, ml.reshape(n, d), w_attn_o[layer].astype(BF16),
                      w_out[layer].astype(BF16), tm)

        kc, vc = _mem_kv(mem, row(norm_mem[layer]), w_ckv[layer].astype(BF16), row(ck_norm[layer]))
        w_r = jnp.concatenate([w_group[layer], w_router[layer],
                               jnp.zeros((d, LANES - N_GROUPS - N_EXPERTS), F32)], axis=1).astype(F32)
        b_r = jnp.concatenate([b_group[layer], b_router[layer],
                               jnp.zeros((LANES - N_GROUPS - N_EXPERTS,), F32)]).reshape(1, LANES).astype(F32)
        wrh, wrl = _split_bf16(w_r)
        x2, hn, eid, ew = _cross_router(seq(x1), row(norm_cx[layer]), w_cq[layer].astype(BF16), row(cq_norm[layer]),
                                        kc, vc, w_co[layer].astype(BF16), row(norm_ffn[layer]), wrh, wrl, b_r, ts)

        blk_e, blk_n, idx3 = _route(eid.reshape(n, LANES)[:, :TOP_K].reshape(-1), n, eb)
        ys = _experts(blk_e, blk_n, idx3, hn.reshape(n, d), w_gate_up[layer].astype(BF16),
                      w_down[layer].astype(BF16), eb, n * TOP_K, plane)
        x = _combine(x2.reshape(n, d), ew.reshape(n, LANES), ys, plane, tm).reshape(bsz, s, d)
    return x
```

```python
import functools
import math

import jax
import jax.numpy as jnp
from jax import lax
from jax.experimental import pallas as pl
from jax.experimental.pallas import tpu as pltpu

F32 = jnp.float32
BF16 = jnp.bfloat16
I32 = jnp.int32

D_MODEL = 1024
CHUNK = 64
LRU_BLOCKS = 8
LRU_BLOCK_WIDTH = D_MODEL // LRU_BLOCKS
CONV_WIDTH = 4
LRU_C = 8.0
ATTN_HEADS = 8
HEAD_DIM = 64
V_DIM = 2 * HEAD_DIM
ROPE_THETA = 500000.0
ROT_DIMS = HEAD_DIM // 4
ROT_HALF = ROT_DIMS // 2
MEM_HEADS = 4
MEM_HEAD_DIM = 128
MEM_WIDTH = MEM_HEADS * MEM_HEAD_DIM
N_GROUPS = 4
EXPERTS_PER_GROUP = 8
N_EXPERTS = N_GROUPS * EXPERTS_PER_GROUP
TOP_K = 2
EXPERT_FF = 512
N_PROJ = 7
EPS = 1e-6
NEG_INF = -1e30
LOG2_E = math.log2(math.e)
ATTN_COL_GROUP = 512

LANES = 128
SUBLANES = 8
V7X_VMEM_BYTES = 64 * 1024 * 1024
MIB = 1024 * 1024


def _vmem_limit(estimate_bytes):
    return int(min(max(estimate_bytes * 3 // 2, 16 * MIB), V7X_VMEM_BYTES - 8 * MIB))


def _params(semantics, vmem_estimate):
    return pltpu.CompilerParams(dimension_semantics=semantics, vmem_limit_bytes=_vmem_limit(vmem_estimate))


def _resident(shape, index_map):
    return pl.BlockSpec(shape, index_map, pipeline_mode=pl.Buffered(1))


def _rms(x, g):
    return x * lax.rsqrt(jnp.mean(x * x, axis=-1, keepdims=True) + EPS) * g


def _lane_tile(x, n):
    return x if n == 1 else jnp.concatenate([x] * n, axis=1)


def _segment_ones():
    r = lax.broadcasted_iota(I32, (LANES, LANES), 0) // HEAD_DIM
    c = lax.broadcasted_iota(I32, (LANES, LANES), 1) // HEAD_DIM
    return (r == c).astype(BF16)


def _qk_post(p, gain, cos_t, sin_lo, sin_hi, seg, scale):
    cols = []
    for c in range(D_MODEL // LANES):
        pc = p[:, c * LANES:(c + 1) * LANES]
        ss = jnp.dot((pc * pc).astype(BF16), seg, preferred_element_type=F32)
        y = pc * lax.rsqrt(ss * (1.0 / HEAD_DIM) + EPS) * gain
        y = y * cos_t + pltpu.roll(y, LANES - ROT_HALF, 1) * sin_lo + pltpu.roll(y, ROT_HALF, 1) * sin_hi
        cols.append((y * scale).astype(BF16))
    return jnp.concatenate(cols, axis=1)


def _in_proj_kernel(x_ref, g_ref, w_ref, qg_ref, kg_ref, cos_ref, slo_ref, shi_ref,
                    lin_ref, lgate_ref, q_ref, k_ref, vt_ref, gl_ref, ga_ref):
    h = _rms(x_ref[...], g_ref[...]).astype(BF16)

    def proj(j):
        return jnp.dot(h, w_ref[:, j * D_MODEL:(j + 1) * D_MODEL], preferred_element_type=F32)

    lin_ref[...] = proj(0).astype(BF16)
    lgate_ref[...] = proj(1).astype(BF16)
    seg = _segment_ones()
    cos_t, sin_lo, sin_hi = cos_ref[...], slo_ref[...], shi_ref[...]
    q_ref[...] = _qk_post(proj(2), qg_ref[...], cos_t, sin_lo, sin_hi, seg, HEAD_DIM ** -0.5 * LOG2_E)
    k_ref[...] = _qk_post(proj(3), kg_ref[...], cos_t, sin_lo, sin_hi, seg, 1.0)
    v = proj(4)
    for hd in range(ATTN_HEADS):
        vt_ref[hd * V_DIM:(hd + 1) * V_DIM, :] = v[:, hd * V_DIM:(hd + 1) * V_DIM].T.astype(BF16)
    gl_ref[...] = proj(5).astype(BF16)
    ga_ref[...] = proj(6).astype(BF16)


def _in_proj(x2d, g, w_in, qg, kg, cos_t, sin_lo, sin_hi, tm, seq_len):
    n = x2d.shape[0]
    per_seq = seq_len // tm
    row = lambda i: (i, 0)
    fixed = lambda i: (0, 0)
    tok = pl.BlockSpec((tm, D_MODEL), row)
    tab = pl.BlockSpec((tm, LANES), row)
    tok_out = jax.ShapeDtypeStruct((n, D_MODEL), BF16)
    vt_out = jax.ShapeDtypeStruct((n // seq_len, D_MODEL, seq_len), BF16)
    vt_spec = pl.BlockSpec((None, D_MODEL, tm), lambda i: (i // per_seq, 0, i % per_seq))
    est = (w_in.size * 2 + 2 * tm * D_MODEL * 4 + 6 * tm * LANES * 4 + N_PROJ * 2 * tm * D_MODEL * 2
           + 4 * tm * D_MODEL * 4)
    return pl.pallas_call(
        _in_proj_kernel,
        out_shape=[tok_out] * 4 + [vt_out] + [tok_out] * 2,
        grid=(n // tm,),
        in_specs=[tok, _resident((1, D_MODEL), fixed), _resident(w_in.shape, fixed),
                  _resident((1, LANES), fixed), _resident((1, LANES), fixed), tab, tab, tab],
        out_specs=[tok] * 4 + [vt_spec] + [tok] * 2,
        compiler_params=_params(("parallel",), est),
        name="in_proj",
    )(x2d, g, w_in, qg, kg, cos_t, sin_lo, sin_hi)


def _lru_kernel(lin_ref, lgate_ref, gl_ref, cw_ref, cb_ref, wax_ref, bax_ref, lam_ref, wo_ref,
                out_ref, xprev_ref, hprev_ref, a_ref, b_ref, h_ref):
    t = pl.program_id(1)

    @pl.when(t == 0)
    def _():
        xprev_ref[...] = jnp.zeros_like(xprev_ref)
        hprev_ref[...] = jnp.zeros_like(hprev_ref)

    x = lin_ref[...].astype(F32)
    tt = x.shape[0]
    nblk = tt // SUBLANES
    xp = jnp.concatenate([xprev_ref[...], x], axis=0)
    cw = cw_ref[...]
    xc = cb_ref[...] + cw[3:4] * x
    for j in range(1, CONV_WIDTH):
        xc = xc + cw[CONV_WIDTH - 1 - j:CONV_WIDTH - j] * xp[SUBLANES - j:SUBLANES - j + tt]
    xprev_ref[...] = x[tt - SUBLANES:tt]

    xcb = xc.astype(BF16)
    ra, ri = [], []
    for n in range(LRU_BLOCKS):
        g = jnp.dot(xcb[:, n * LRU_BLOCK_WIDTH:(n + 1) * LRU_BLOCK_WIDTH], wax_ref[n], preferred_element_type=F32)
        ra.append(g[:, :LRU_BLOCK_WIDTH])
        ri.append(g[:, LRU_BLOCK_WIDTH:])
    bax = bax_ref[...]
    r = jax.nn.sigmoid(jnp.concatenate(ra, axis=1) + bax[0:1])
    i = jax.nn.sigmoid(jnp.concatenate(ri, axis=1) + bax[1:2])
    log_a = -LRU_C * r * jax.nn.softplus(-lam_ref[...])
    a = jnp.exp(log_a)
    b = jnp.sqrt(-jnp.tanh(log_a) * (a * a + 1.0)) * i * xc

    a3 = a.reshape(nblk, SUBLANES, D_MODEL)
    b3 = b.reshape(nblk, SUBLANES, D_MODEL)
    sub = lax.broadcasted_iota(I32, (nblk, SUBLANES, D_MODEL), 1)
    shift = 1
    while shift < SUBLANES:
        keep = sub >= shift
        a_sh = pltpu.roll(a3, shift, 1)
        b_sh = pltpu.roll(b3, shift, 1)
        b3 = jnp.where(keep, a3 * b_sh + b3, b3)
        a3 = jnp.where(keep, a3 * a_sh, a3)
        shift *= 2
    a_ref[...] = a3
    b_ref[...] = b3

    def carry_step(blk, h_last):
        h = a_ref[blk] * h_last + b_ref[blk]
        h_ref[blk] = h
        return jnp.broadcast_to(h[SUBLANES - 1:SUBLANES], (SUBLANES, D_MODEL))

    hprev_ref[...] = lax.fori_loop(0, nblk, carry_step, hprev_ref[...], unroll=4)
    hr = h_ref[...].reshape(tt, D_MODEL)

    y = (jax.nn.gelu(lgate_ref[...].astype(F32)) * hr).astype(BF16)
    yl = jnp.dot(y, wo_ref[...], preferred_element_type=F32)
    out_ref[...] = (jax.nn.sigmoid(gl_ref[...].astype(F32)) * yl).astype(BF16)


def _lru(lin, lgate, gl, cw, cb, wax, bax, lam, wo, tt):
    bsz, s, _ = lin.shape
    seq = pl.BlockSpec((None, tt, D_MODEL), lambda b, t: (b, t, 0))
    fix2 = lambda b, t: (0, 0)
    fix3 = lambda b, t: (0, 0, 0)
    nblk = tt // SUBLANES
    est = 4 * 2 * tt * D_MODEL * 2 + wo.size * 2 + wax.size * 2 + 3 * tt * D_MODEL * 4 + 10 * tt * D_MODEL * 4
    return pl.pallas_call(
        _lru_kernel,
        out_shape=jax.ShapeDtypeStruct((bsz, s, D_MODEL), BF16),
        grid=(bsz, s // tt),
        in_specs=[seq, seq, seq, _resident(cw.shape, fix2), _resident(cb.shape, fix2), _resident(wax.shape, fix3),
                  _resident(bax.shape, fix2), _resident(lam.shape, fix2), _resident(wo.shape, fix2)],
        out_specs=seq,
        scratch_shapes=[pltpu.VMEM((SUBLANES, D_MODEL), F32), pltpu.VMEM((SUBLANES, D_MODEL), F32),
                        pltpu.VMEM((nblk, SUBLANES, D_MODEL), F32), pltpu.VMEM((nblk, SUBLANES, D_MODEL), F32),
                        pltpu.VMEM((nblk, SUBLANES, D_MODEL), F32)],
        compiler_params=_params(("parallel", "arbitrary"), est),
        name="lru",
    )(lin, lgate, gl, cw, cb, wax, bax, lam, wo)


def _attn_kernel(lam_ref, q_ref, k_ref, vt_ref, sub_ref, o_ref, m_ref, l_ref, acc_ref, qz_ref, sa_ref, sb_ref, *,
                 tq, out_scale):
    i = pl.program_id(2)
    qt = q_ref[...].astype(F32).T
    row = lax.broadcasted_iota(I32, (V_DIM, tq), 0)
    zero = jnp.zeros_like(qt)
    qz = jnp.concatenate([jnp.where(row < HEAD_DIM, qt, zero), jnp.where(row >= HEAD_DIM, qt, zero)],
                         axis=1).astype(BF16)
    m_ref[...] = jnp.full_like(m_ref, NEG_INF)
    l_ref[...] = jnp.zeros_like(l_ref)
    acc_ref[...] = jnp.zeros_like(acc_ref)

    qz_ref[...] = qz
    cw = ATTN_COL_GROUP

    def block_off(j):
        return pl.multiple_of(j * tq, tq)

    def scores(j, s_ref):
        s_ref[...] = jnp.dot(k_ref[pl.ds(block_off(j), tq), :], qz_ref[...], preferred_element_type=F32)

    def softmax_pv(j, s_ref, diagonal):
        vtb = vt_ref[:, pl.ds(block_off(j), tq)]
        for g in range(2 * tq // cw):
            cols = pl.ds(g * cw, cw)
            s = s_ref[:, cols]
            if diagonal:
                key = lax.broadcasted_iota(I32, (tq, cw), 0)
                qry = (lax.broadcasted_iota(I32, (tq, cw), 1) + g * cw) % tq
                s = jnp.where((key // CHUNK) <= (qry // CHUNK), s, NEG_INF)
            m_prev = m_ref[:, cols]
            m_new = jnp.maximum(m_prev, jnp.max(s, axis=0, keepdims=True))
            alpha = jnp.exp2(m_prev - m_new)
            p = jnp.exp2(s - m_new)
            l_ref[:, cols] = alpha * l_ref[:, cols] + jnp.sum(p, axis=0, keepdims=True)
            acc_ref[:, cols] = alpha * acc_ref[:, cols] + jnp.dot(vtb, p.astype(BF16), preferred_element_type=F32)
            m_ref[:, cols] = m_new

    scores(0, sa_ref)

    def pair(p, carry):
        j = 2 * p
        scores(j + 1, sb_ref)
        softmax_pv(j, sa_ref, False)
        scores(j + 2, sa_ref)
        softmax_pv(j + 1, sb_ref, False)
        return carry

    lax.fori_loop(0, i // 2, pair, 0)

    @pl.when(i % 2 == 1)
    def _():
        scores(i, sb_ref)
        softmax_pv(i - 1, sa_ref, False)
        softmax_pv(i, sb_ref, True)

    @pl.when(i % 2 == 0)
    def _():
        softmax_pv(i, sa_ref, True)

    o12 = acc_ref[...] * (1.0 / l_ref[...])
    ot = o12[:, :tq] - lam_ref[0, 0] * o12[:, tq:]
    ot = ot * lax.rsqrt(jnp.mean(ot * ot, axis=0, keepdims=True) + EPS) * sub_ref[...] * out_scale
    o_ref[...] = ot.T.astype(BF16)


def _diff_attn(lam, q, k, vt, sub, lambda_init, tq):
    bsz, s, _ = q.shape
    qspec = pl.BlockSpec((None, tq, V_DIM), lambda b, h, i: (b, i, h))
    kspec = pl.BlockSpec((None, s, V_DIM), lambda b, h, i: (b, 0, h))
    vtspec = pl.BlockSpec((None, V_DIM, s), lambda b, h, i: (b, h, 0))
    est = 2 * 2 * s * V_DIM * 2 + 4 * tq * V_DIM * 2 + V_DIM * 2 * tq * 4 + 5 * 2 * tq * tq * 4
    return pl.pallas_call(
        functools.partial(_attn_kernel, tq=tq, out_scale=1.0 - lambda_init),
        out_shape=jax.ShapeDtypeStruct((bsz, s, ATTN_HEADS * V_DIM), BF16),
        grid=(bsz, ATTN_HEADS, s // tq),
        in_specs=[pl.BlockSpec(memory_space=pltpu.SMEM), qspec, kspec, vtspec,
                  pl.BlockSpec((V_DIM, 1), lambda b, h, i: (0, 0))],
        out_specs=qspec,
        scratch_shapes=[pltpu.VMEM((1, 2 * tq), F32), pltpu.VMEM((1, 2 * tq), F32),
                        pltpu.VMEM((V_DIM, 2 * tq), F32), pltpu.VMEM((V_DIM, 2 * tq), BF16),
                        pltpu.VMEM((tq, 2 * tq), F32), pltpu.VMEM((tq, 2 * tq), F32)],
        compiler_params=_params(("parallel", "parallel", "parallel"), est),
        name="diff_attn",
    )(lam, q, k, vt, sub)


def _mix_out_kernel(x_ref, o_ref, ga_ref, ml_ref, wao_ref, wout_ref, x1_ref):
    ya = jnp.dot(o_ref[...], wao_ref[...], preferred_element_type=F32)
    mixed = ml_ref[...].astype(F32) + jax.nn.sigmoid(ga_ref[...].astype(F32)) * ya
    x1_ref[...] = x_ref[...] + jnp.dot(mixed.astype(BF16), wout_ref[...], preferred_element_type=F32)


def _mix_out(x2d, o, ga, ml, wao, wout, tm):
    n = x2d.shape[0]
    tok = pl.BlockSpec((tm, D_MODEL), lambda i: (i, 0))
    fixed = lambda i: (0, 0)
    est = 2 * 2 * tm * D_MODEL * 4 + 3 * 2 * tm * D_MODEL * 2 + 2 * D_MODEL * D_MODEL * 2 + 3 * tm * D_MODEL * 4
    return pl.pallas_call(
        _mix_out_kernel,
        out_shape=jax.ShapeDtypeStruct((n, D_MODEL), F32),
        grid=(n // tm,),
        in_specs=[tok, tok, tok, tok, _resident(wao.shape, fixed), _resident(wout.shape, fixed)],
        out_specs=tok,
        compiler_params=_params(("parallel",), est),
        name="mix_out",
    )(x2d, o, ga, ml, wao, wout)


def _mem_kv_kernel(mem_ref, g_ref, w_ref, ckg_ref, k_ref, v_ref):
    h = _rms(mem_ref[...], g_ref[...]).astype(BF16)
    kv = jnp.dot(h, w_ref[...], preferred_element_type=F32)
    ks = [_rms(kv[:, hd * MEM_HEAD_DIM:(hd + 1) * MEM_HEAD_DIM], ckg_ref[...]) for hd in range(MEM_HEADS)]
    k_ref[...] = jnp.concatenate(ks, axis=1).astype(BF16)
    v_ref[...] = kv[:, MEM_WIDTH:].astype(BF16)


def _mem_kv(mem, g, w, ckg):
    bsz, m, _ = mem.shape
    fixed = lambda b: (0, 0)
    out = pl.BlockSpec((None, m, MEM_WIDTH), lambda b: (b, 0, 0))
    est = 2 * m * D_MODEL * 4 + w.size * 2 + 4 * m * MEM_WIDTH * 2 + 4 * m * D_MODEL * 4
    return pl.pallas_call(
        _mem_kv_kernel,
        out_shape=[jax.ShapeDtypeStruct((bsz, m, MEM_WIDTH), BF16)] * 2,
        grid=(bsz,),
        in_specs=[pl.BlockSpec((None, m, D_MODEL), lambda b: (b, 0, 0)), _resident((1, D_MODEL), fixed),
                  _resident(w.shape, fixed), _resident((1, MEM_HEAD_DIM), fixed)],
        out_specs=[out, out],
        compiler_params=_params(("parallel",), est),
        name="mem_kv",
    )(mem, g, w, ckg)


def _split_bf16(x):
    hi = x.astype(BF16)
    return hi, (x - hi.astype(F32)).astype(BF16)


def _cross_router_kernel(x1_ref, gcx_ref, wcq_ref, cqg_ref, kc_ref, vc_ref, wco_ref, gffn_ref, wrh_ref, wrl_ref,
                         br_ref, x2_ref, hn_ref, eid_ref, ew_ref):
    x1 = x1_ref[...]
    q = jnp.dot(_rms(x1, gcx_ref[...]).astype(BF16), wcq_ref[...], preferred_element_type=F32)
    outs = []
    for hd in range(MEM_HEADS):
        sl = slice(hd * MEM_HEAD_DIM, (hd + 1) * MEM_HEAD_DIM)
        qh = _rms(q[:, sl], cqg_ref[...]) * MEM_HEAD_DIM ** -0.5
        s = lax.dot_general(qh.astype(BF16), kc_ref[:, sl], (((1,), (1,)), ((), ())), preferred_element_type=F32)
        p = jnp.exp(s - jnp.max(s, axis=1, keepdims=True))
        o = jnp.dot(p.astype(BF16), vc_ref[:, sl], preferred_element_type=F32)
        outs.append(o / jnp.sum(p, axis=1, keepdims=True))
    x2 = x1 + jnp.dot(jnp.concatenate(outs, axis=1).astype(BF16), wco_ref[...], preferred_element_type=F32)
    x2_ref[...] = x2

    hn = _rms(x2, gffn_ref[...])
    hn_ref[...] = hn
    h_hi, h_lo = _split_bf16(hn)
    logits = (jnp.dot(h_hi, wrh_ref[...], preferred_element_type=F32)
              + jnp.dot(h_lo, wrh_ref[...], preferred_element_type=F32)
              + jnp.dot(h_hi, wrl_ref[...], preferred_element_type=F32)) + br_ref[...]
    lane = lax.broadcasted_iota(I32, logits.shape, 1)
    is_group = lane < N_GROUPS
    gl = jnp.where(is_group, logits, NEG_INF)
    gmax = jnp.max(gl, axis=1, keepdims=True)
    gval = 1.0 / jnp.sum(jnp.where(is_group, jnp.exp(gl - gmax), 0.0), axis=1, keepdims=True)
    gidx = jnp.min(jnp.where(gl == gmax, lane, LANES), axis=1, keepdims=True)
    lane_group = lax.shift_right_logical(lane + (EXPERTS_PER_GROUP - N_GROUPS), 3) - 1
    chosen = lane_group == gidx
    el = jnp.where(chosen, logits, NEG_INF)
    v1 = jnp.max(el, axis=1, keepdims=True)
    i1 = jnp.min(jnp.where(chosen & (el == v1), lane, LANES), axis=1, keepdims=True)
    rest = chosen & (lane != i1)
    el2 = jnp.where(rest, logits, NEG_INF)
    v2 = jnp.max(el2, axis=1, keepdims=True)
    i2 = jnp.min(jnp.where(rest & (el2 == v2), lane, LANES), axis=1, keepdims=True)
    t = jnp.exp(v2 - v1)
    w1 = gval / (1.0 + t)
    w2 = gval * t / (1.0 + t)
    eid_ref[...] = jnp.where(lane == 0, i1 - N_GROUPS, jnp.where(lane == 1, i2 - N_GROUPS, 0))
    ew_ref[...] = jnp.where(lane == 0, w1, jnp.where(lane == 1, w2, 0.0))


def _cross_router(x1, gcx, wcq, cqg, kc, vc, wco, gffn, wrh, wrl, br, tm):
    bsz, s, _ = x1.shape
    m = kc.shape[1]
    tok = pl.BlockSpec((None, tm, D_MODEL), lambda b, i: (b, i, 0))
    small = pl.BlockSpec((None, tm, LANES), lambda b, i: (b, i, 0))
    memb = pl.BlockSpec((None, m, MEM_WIDTH), lambda b, i: (b, 0, 0))
    fixed = lambda b, i: (0, 0)
    est = (3 * 2 * tm * D_MODEL * 4 + 2 * 2 * tm * LANES * 4 + 2 * 2 * m * MEM_WIDTH * 2 + 2 * D_MODEL * MEM_WIDTH * 2
           + 2 * D_MODEL * LANES * 2 + 6 * tm * D_MODEL * 4)
    return pl.pallas_call(
        _cross_router_kernel,
        out_shape=[jax.ShapeDtypeStruct((bsz, s, D_MODEL), F32), jax.ShapeDtypeStruct((bsz, s, D_MODEL), F32),
                   jax.ShapeDtypeStruct((bsz, s, LANES), I32), jax.ShapeDtypeStruct((bsz, s, LANES), F32)],
        grid=(bsz, s // tm),
        in_specs=[tok, _resident((1, D_MODEL), fixed), _resident(wcq.shape, fixed), _resident((1, MEM_HEAD_DIM), fixed),
                  memb, memb, _resident(wco.shape, fixed), _resident((1, D_MODEL), fixed),
                  _resident(wrh.shape, fixed), _resident(wrl.shape, fixed), _resident((1, LANES), fixed)],
        out_specs=[tok, tok, small, small],
        compiler_params=_params(("parallel", "parallel"), est),
        name="cross_router",
    )(x1, gcx, wcq, cqg, kc, vc, wco, gffn, wrh, wrl, br)


def _expert_kernel(blk_e_ref, blk_n_ref, idx_ref, idx_next_ref, hn_hbm, wgu_ref, wd_ref, ys_hbm,
                   xbuf, ybuf, gsem, ssem, *, eb, n_assign, plane):
    i = pl.program_id(0)
    last = pl.num_programs(0) - 1
    slot = i % 2

    def start_gather(idx, s):
        def issue(r, carry):
            flat = idx[0, 0, r]
            tok = jnp.where(flat < n_assign, lax.shift_right_logical(flat, 1), 0)
            pltpu.make_async_copy(hn_hbm.at[pl.ds(tok, 1), :], xbuf.at[s, pl.ds(r, 1), :], gsem.at[s]).start()
            return carry
        lax.fori_loop(0, eb, issue, 0, unroll=8)

    def wait_gather(s):
        pltpu.make_async_copy(hn_hbm.at[pl.ds(0, eb), :], xbuf.at[s], gsem.at[s]).wait()

    def wait_scatter():
        pltpu.make_async_copy(ybuf, ys_hbm.at[pl.ds(0, eb), :], ssem).wait()

    @pl.when((i == 0) & (blk_n_ref[0] > 0))
    def _():
        start_gather(idx_ref, 0)

    @pl.when((i < last) & (blk_n_ref[jnp.minimum(i + 1, last)] > 0))
    def _():
        start_gather(idx_next_ref, 1 - slot)

    @pl.when(blk_n_ref[i] > 0)
    def _():
        wait_gather(slot)
        x = xbuf[slot].astype(BF16)
        gu = jnp.dot(x, wgu_ref[0], preferred_element_type=F32)
        act = (jax.nn.silu(gu[:, :EXPERT_FF]) * gu[:, EXPERT_FF:]).astype(BF16)
        y = jnp.dot(act, wd_ref[0], preferred_element_type=F32)

        @pl.when((i > 0) & (blk_n_ref[jnp.maximum(i - 1, 0)] > 0))
        def _():
            wait_scatter()

        ybuf[...] = y

        def issue(r, carry):
            flat = idx_ref[0, 0, r]
            dst = jnp.where(flat < n_assign, (flat & 1) * plane + lax.shift_right_logical(flat, 1), plane - eb + r)
            pltpu.make_async_copy(ybuf.at[pl.ds(r, 1), :], ys_hbm.at[pl.ds(dst, 1), :], ssem).start()
            return carry
        lax.fori_loop(0, eb, issue, 0, unroll=8)

        @pl.when(i == last)
        def _():
            wait_scatter()

    @pl.when((blk_n_ref[i] == 0) & (i > 0) & (blk_n_ref[jnp.maximum(i - 1, 0)] > 0))
    def _():
        wait_scatter()


def _experts(blk_e, blk_n, idx3, hn2d, wgu, wd, eb, n_assign, plane):
    n_blk = blk_e.shape[0]
    idx_spec = pl.BlockSpec((1, 1, eb), lambda i, be, bn: (i, 0, 0), memory_space=pltpu.SMEM)
    idx_next_spec = pl.BlockSpec((1, 1, eb), lambda i, be, bn: (jnp.minimum(i + 1, n_blk - 1), 0, 0),
                                 memory_space=pltpu.SMEM)
    est = 2 * eb * D_MODEL * 4 + eb * D_MODEL * 4 + 2 * (wgu.shape[1] * wgu.shape[2] + wd.shape[1] * wd.shape[2]) * 2 \
        + 4 * eb * D_MODEL * 4
    grid_spec = pltpu.PrefetchScalarGridSpec(
        num_scalar_prefetch=2,
        grid=(n_blk,),
        in_specs=[idx_spec, idx_next_spec, pl.BlockSpec(memory_space=pl.ANY),
                  pl.BlockSpec((1,) + wgu.shape[1:], lambda i, be, bn: (be[i], 0, 0)),
                  pl.BlockSpec((1,) + wd.shape[1:], lambda i, be, bn: (be[i], 0, 0))],
        out_specs=pl.BlockSpec(memory_space=pl.ANY),
        scratch_shapes=[pltpu.VMEM((2, eb, D_MODEL), F32), pltpu.VMEM((eb, D_MODEL), F32),
                        pltpu.SemaphoreType.DMA((2,)), pltpu.SemaphoreType.DMA],
    )
    return pl.pallas_call(
        functools.partial(_expert_kernel, eb=eb, n_assign=n_assign, plane=plane),
        out_shape=jax.ShapeDtypeStruct((TOP_K * plane - eb, D_MODEL), F32),
        grid_spec=grid_spec,
        compiler_params=_params(("arbitrary",), est),
        name="experts",
    )(blk_e, blk_n, idx3, idx3, hn2d, wgu, wd)


def _combine_kernel(x2_ref, ew_ref, y0_ref, y1_ref, o_ref):
    ew = ew_ref[...]
    o_ref[...] = x2_ref[...] + (ew[:, 0:1] * y0_ref[...] + ew[:, 1:2] * y1_ref[...])


def _combine(x2, ew, ys, plane, tm):
    n = x2.shape[0]
    tok = pl.BlockSpec((tm, D_MODEL), lambda i: (i, 0))
    shift = plane // tm
    est = 4 * 2 * tm * D_MODEL * 4 + 2 * tm * LANES * 4 + 2 * tm * D_MODEL * 4
    return pl.pallas_call(
        _combine_kernel,
        out_shape=jax.ShapeDtypeStruct((n, D_MODEL), F32),
        grid=(n // tm,),
        in_specs=[tok, pl.BlockSpec((tm, LANES), lambda i: (i, 0)), tok,
                  pl.BlockSpec((tm, D_MODEL), lambda i: (i + shift, 0))],
        out_specs=tok,
        compiler_params=_params(("parallel",), est),
        name="combine",
    )(x2, ew, ys, ys)


def _rope_tables(positions):
    inv_freq = jnp.exp(-math.log(ROPE_THETA) * jnp.arange(ROT_HALF, dtype=F32) / ROT_HALF)
    ang = positions.astype(F32).reshape(-1, 1) * inv_freq
    cos, sin = jnp.cos(ang), jnp.sin(ang)
    n = ang.shape[0]
    rest = HEAD_DIM - ROT_DIMS
    cos_seg = jnp.concatenate([cos, cos, jnp.ones((n, rest), F32)], axis=1)
    lo_seg = jnp.concatenate([-sin, jnp.zeros((n, HEAD_DIM - ROT_HALF), F32)], axis=1)
    hi_seg = jnp.concatenate([jnp.zeros((n, ROT_HALF), F32), sin, jnp.zeros((n, rest), F32)], axis=1)
    rep = LANES // HEAD_DIM
    return jnp.tile(cos_seg, (1, rep)), jnp.tile(lo_seg, (1, rep)), jnp.tile(hi_seg, (1, rep))


def _route(eid, n_tok, eb):
    n_assign = eid.shape[0]
    order = jnp.argsort(eid).astype(I32)
    sorted_e = eid[order]
    counts = jnp.bincount(eid, length=N_EXPERTS).astype(I32)
    padded = (counts + eb - 1) // eb * eb
    start = jnp.cumsum(counts) - counts
    pend = jnp.cumsum(padded)
    pstart = pend - padded
    dest = pstart[sorted_e] + jnp.arange(n_assign, dtype=I32) - start[sorted_e]
    p_rows = n_assign + N_EXPERTS * eb
    n_blk = p_rows // eb
    buf_flat = jnp.full((p_rows,), n_assign, I32).at[dest].set(order)
    blk_first = jnp.arange(n_blk, dtype=I32) * eb
    blk_e = jnp.minimum(jnp.searchsorted(pend, blk_first, side='right'), N_EXPERTS - 1).astype(I32)
    blk_n = jnp.clip(pstart[blk_e] + counts[blk_e] - blk_first, 0, eb).astype(I32)
    return blk_e, blk_n, buf_flat.reshape(n_blk, 1, eb)


def _tile(n, pref):
    t = min(n, pref)
    assert n % t == 0, (n, pref)
    return t


def kernel(x, mem, positions, norm_mix, w_in, conv_w, conv_b, lru_wa, lru_ba, lru_wx, lru_bx, lru_lambda, w_lru_o, q_norm, k_norm, lambda_q1, lambda_k1, lambda_q2, lambda_k2, subln, w_attn_o, w_out, norm_cx, norm_mem, w_cq, w_ckv, cq_norm, ck_norm, w_co, norm_ffn, w_group, b_group, w_router, b_router, w_gate_up, w_down):
    bsz, s, d = x.shape
    assert d == D_MODEL and w_in.shape[-1] == N_PROJ * D_MODEL
    n = bsz * s
    depth = w_in.shape[0]
    tm = _tile(n, 512)
    ts = _tile(s, 512)
    tq = _tile(s, 512)
    assert tq % CHUNK == 0 and ts % SUBLANES == 0
    eb = _tile(n, 512)
    plane = n + eb
    cos_t, sin_lo, sin_hi = _rope_tables(positions)
    row = lambda v: v.reshape(1, -1).astype(F32)
    rep = LANES // HEAD_DIM

    for layer in range(depth):
        lambda_init = 0.8 - 0.6 * math.exp(-0.3 * layer)
        lin, lgate, q, k, vt, gl, ga = _in_proj(
            x.reshape(n, d), row(norm_mix[layer]), w_in[layer].astype(BF16),
            jnp.tile(row(q_norm[layer]), (1, rep)), jnp.tile(row(k_norm[layer]), (1, rep)), cos_t, sin_lo, sin_hi,
            ts, s)
        seq = lambda a: a.reshape(bsz, s, d)

        wax = jnp.concatenate([lru_wa[layer], lru_wx[layer]], axis=-1).astype(BF16)
        bax = jnp.stack([lru_ba[layer], lru_bx[layer]]).astype(F32)
        ml = _lru(seq(lin), seq(lgate), seq(gl), conv_w[layer].astype(F32), row(conv_b[layer]), wax, bax,
                  row(lru_lambda[layer]), w_lru_o[layer].astype(BF16), ts)

        lam = (jnp.exp(jnp.sum(lambda_q1[layer].astype(F32) * lambda_k1[layer].astype(F32)))
               - jnp.exp(jnp.sum(lambda_q2[layer].astype(F32) * lambda_k2[layer].astype(F32))) + lambda_init)
        o = _diff_attn(lam.reshape(1, 1), seq(q), seq(k), vt, subln[layer].astype(F32).reshape(V_DIM, 1),
                       lambda_init, tq)

        x1 = _mix_out(x.reshape(n, d), o.reshape(n, d), ga, ml.reshape(n, d), w_attn_o[layer].astype(BF16),
                      w_out[layer].astype(BF16), tm)

        kc, vc = _mem_kv(mem, row(norm_mem[layer]), w_ckv[layer].astype(BF16), row(ck_norm[layer]))
        w_r = jnp.concatenate([w_group[layer], w_router[layer],
                               jnp.zeros((d, LANES - N_GROUPS - N_EXPERTS), F32)], axis=1).astype(F32)
        b_r = jnp.concatenate([b_group[layer], b_router[layer],
                               jnp.zeros((LANES - N_GROUPS - N_EXPERTS,), F32)]).reshape(1, LANES).astype(F32)
        wrh, wrl = _split_bf16(w_r)
        x2, hn, eid, ew = _cross_router(seq(x1), row(norm_cx[layer]), w_cq[layer].astype(BF16), row(cq_norm[layer]),
                                        kc, vc, w_co[layer].astype(BF16), row(norm_ffn[layer]), wrh, wrl, b_r, ts)

        blk_e, blk_n, idx3 = _route(eid.reshape(n, LANES)[:, :TOP_K].reshape(-1), n, eb)
        ys = _experts(blk_e, blk_n, idx3, hn.reshape(n, d), w_gate_up[layer].astype(BF16),
                      w_down[layer].astype(BF16), eb, n * TOP_K, plane)
        x = _combine(x2.reshape(n, d), ew.reshape(n, LANES), ys, plane, tm).reshape(bsz, s, d)
    return x
```

```python
import functools
import math

import jax
import jax.numpy as jnp
from jax import lax
from jax.experimental import pallas as pl
from jax.experimental.pallas import tpu as pltpu

F32 = jnp.float32
BF16 = jnp.bfloat16
I32 = jnp.int32

D_MODEL = 1024
CHUNK = 64
LRU_BLOCKS = 8
LRU_BLOCK_WIDTH = D_MODEL // LRU_BLOCKS
CONV_WIDTH = 4
LRU_C = 8.0
ATTN_HEADS = 8
HEAD_DIM = 64
V_DIM = 2 * HEAD_DIM
ROPE_THETA = 500000.0
ROT_DIMS = HEAD_DIM // 4
ROT_HALF = ROT_DIMS // 2
MEM_HEADS = 4
MEM_HEAD_DIM = 128
MEM_WIDTH = MEM_HEADS * MEM_HEAD_DIM
N_GROUPS = 4
EXPERTS_PER_GROUP = 8
N_EXPERTS = N_GROUPS * EXPERTS_PER_GROUP
TOP_K = 2
EXPERT_FF = 512
N_PROJ = 7
EPS = 1e-6
NEG_INF = -1e30
LOG2_E = math.log2(math.e)
ATTN_COL_GROUP = 512

LANES = 128
SUBLANES = 8
V7X_VMEM_BYTES = 64 * 1024 * 1024
MIB = 1024 * 1024


def _vmem_limit(estimate_bytes):
    return int(min(max(estimate_bytes * 3 // 2, 16 * MIB), V7X_VMEM_BYTES - 8 * MIB))


def _params(semantics, vmem_estimate):
    return pltpu.CompilerParams(dimension_semantics=semantics, vmem_limit_bytes=_vmem_limit(vmem_estimate))


def _resident(shape, index_map):
    return pl.BlockSpec(shape, index_map, pipeline_mode=pl.Buffered(1))


def _rms(x, g):
    return x * lax.rsqrt(jnp.mean(x * x, axis=-1, keepdims=True) + EPS) * g


def _lane_tile(x, n):
    return x if n == 1 else jnp.concatenate([x] * n, axis=1)


def _segment_ones():
    r = lax.broadcasted_iota(I32, (LANES, LANES), 0) // HEAD_DIM
    c = lax.broadcasted_iota(I32, (LANES, LANES), 1) // HEAD_DIM
    return (r == c).astype(BF16)


def _qk_post(p, gain, cos_t, sin_lo, sin_hi, seg, scale):
    cols = []
    for c in range(D_MODEL // LANES):
        pc = p[:, c * LANES:(c + 1) * LANES]
        ss = jnp.dot((pc * pc).astype(BF16), seg, preferred_element_type=F32)
        y = pc * lax.rsqrt(ss * (1.0 / HEAD_DIM) + EPS) * gain
        y = y * cos_t + pltpu.roll(y, LANES - ROT_HALF, 1) * sin_lo + pltpu.roll(y, ROT_HALF, 1) * sin_hi
        cols.append((y * scale).astype(BF16))
    return jnp.concatenate(cols, axis=1)


def _in_proj_kernel(x_ref, g_ref, w_ref, qg_ref, kg_ref, cos_ref, slo_ref, shi_ref,
                    lin_ref, lgate_ref, q_ref, k_ref, vt_ref, gl_ref, ga_ref):
    h = _rms(x_ref[...], g_ref[...]).astype(BF16)

    def proj(j):
        return jnp.dot(h, w_ref[:, j * D_MODEL:(j + 1) * D_MODEL], preferred_element_type=F32)

    lin_ref[...] = proj(0).astype(BF16)
    lgate_ref[...] = proj(1).astype(BF16)
    seg = _segment_ones()
    cos_t, sin_lo, sin_hi = cos_ref[...], slo_ref[...], shi_ref[...]
    q_ref[...] = _qk_post(proj(2), qg_ref[...], cos_t, sin_lo, sin_hi, seg, HEAD_DIM ** -0.5 * LOG2_E)
    k_ref[...] = _qk_post(proj(3), kg_ref[...], cos_t, sin_lo, sin_hi, seg, 1.0)
    v = proj(4)
    for hd in range(ATTN_HEADS):
        vt_ref[hd * V_DIM:(hd + 1) * V_DIM, :] = v[:, hd * V_DIM:(hd + 1) * V_DIM].T.astype(BF16)
    gl_ref[...] = proj(5).astype(BF16)
    ga_ref[...] = proj(6).astype(BF16)


def _in_proj(x2d, g, w_in, qg, kg, cos_t, sin_lo, sin_hi, tm, seq_len):
    n = x2d.shape[0]
    per_seq = seq_len // tm
    row = lambda i: (i, 0)
    fixed = lambda i: (0, 0)
    tok = pl.BlockSpec((tm, D_MODEL), row)
    tab = pl.BlockSpec((tm, LANES), row)
    tok_out = jax.ShapeDtypeStruct((n, D_MODEL), BF16)
    vt_out = jax.ShapeDtypeStruct((n // seq_len, D_MODEL, seq_len), BF16)
    vt_spec = pl.BlockSpec((None, D_MODEL, tm), lambda i: (i // per_seq, 0, i % per_seq))
    est = (w_in.size * 2 + 2 * tm * D_MODEL * 4 + 6 * tm * LANES * 4 + N_PROJ * 2 * tm * D_MODEL * 2
           + 4 * tm * D_MODEL * 4)
    return pl.pallas_call(
        _in_proj_kernel,
        out_shape=[tok_out] * 4 + [vt_out] + [tok_out] * 2,
        grid=(n // tm,),
        in_specs=[tok, _resident((1, D_MODEL), fixed), _resident(w_in.shape, fixed),
                  _resident((1, LANES), fixed), _resident((1, LANES), fixed), tab, tab, tab],
        out_specs=[tok] * 4 + [vt_spec] + [tok] * 2,
        compiler_params=_params(("parallel",), est),
        name="in_proj",
    )(x2d, g, w_in, qg, kg, cos_t, sin_lo, sin_hi)


def _lru_kernel(lin_ref, lgate_ref, gl_ref, cw_ref, cb_ref, wax_ref, bax_ref, lam_ref, wo_ref,
                out_ref, xprev_ref, hprev_ref, a_ref, b_ref, h_ref):
    t = pl.program_id(1)

    @pl.when(t == 0)
    def _():
        xprev_ref[...] = jnp.zeros_like(xprev_ref)
        hprev_ref[...] = jnp.zeros_like(hprev_ref)

    x = lin_ref[...].astype(F32)
    tt = x.shape[0]
    nblk = tt // SUBLANES
    xp = jnp.concatenate([xprev_ref[...], x], axis=0)
    cw = cw_ref[...]
    xc = cb_ref[...] + cw[3:4] * x
    for j in range(1, CONV_WIDTH):
        xc = xc + cw[CONV_WIDTH - 1 - j:CONV_WIDTH - j] * xp[SUBLANES - j:SUBLANES - j + tt]
    xprev_ref[...] = x[tt - SUBLANES:tt]

    xcb = xc.astype(BF16)
    ra, ri = [], []
    for n in range(LRU_BLOCKS):
        g = jnp.dot(xcb[:, n * LRU_BLOCK_WIDTH:(n + 1) * LRU_BLOCK_WIDTH], wax_ref[n], preferred_element_type=F32)
        ra.append(g[:, :LRU_BLOCK_WIDTH])
        ri.append(g[:, LRU_BLOCK_WIDTH:])
    bax = bax_ref[...]
    r = jax.nn.sigmoid(jnp.concatenate(ra, axis=1) + bax[0:1])
    i = jax.nn.sigmoid(jnp.concatenate(ri, axis=1) + bax[1:2])
    log_a = -LRU_C * r * jax.nn.softplus(-lam_ref[...])
    a = jnp.exp(log_a)
    b = jnp.sqrt(-jnp.tanh(log_a) * (a * a + 1.0)) * i * xc

    a3 = a.reshape(nblk, SUBLANES, D_MODEL)
    b3 = b.reshape(nblk, SUBLANES, D_MODEL)
    sub = lax.broadcasted_iota(I32, (nblk, SUBLANES, D_MODEL), 1)
    shift = 1
    while shift < SUBLANES:
        keep = sub >= shift
        a_sh = pltpu.roll(a3, shift, 1)
        b_sh = pltpu.roll(b3, shift, 1)
        b3 = jnp.where(keep, a3 * b_sh + b3, b3)
        a3 = jnp.where(keep, a3 * a_sh, a3)
        shift *= 2
    a_ref[...] = a3
    b_ref[...] = b3

    def carry_step(blk, h_last):
        h = a_ref[blk] * h_last + b_ref[blk]
        h_ref[blk] = h
        return jnp.broadcast_to(h[SUBLANES - 1:SUBLANES], (SUBLANES, D_MODEL))

    hprev_ref[...] = lax.fori_loop(0, nblk, carry_step, hprev_ref[...], unroll=4)
    hr = h_ref[...].reshape(tt, D_MODEL)

    y = (jax.nn.gelu(lgate_ref[...].astype(F32)) * hr).astype(BF16)
    yl = jnp.dot(y, wo_ref[...], preferred_element_type=F32)
    out_ref[...] = (jax.nn.sigmoid(gl_ref[...].astype(F32)) * yl).astype(BF16)


def _lru(lin, lgate, gl, cw, cb, wax, bax, lam, wo, tt):
    bsz, s, _ = lin.shape
    seq = pl.BlockSpec((None, tt, D_MODEL), lambda b, t: (b, t, 0))
    fix2 = lambda b, t: (0, 0)
    fix3 = lambda b, t: (0, 0, 0)
    nblk = tt // SUBLANES
    est = 4 * 2 * tt * D_MODEL * 2 + wo.size * 2 + wax.size * 2 + 3 * tt * D_MODEL * 4 + 10 * tt * D_MODEL * 4
    return pl.pallas_call(
        _lru_kernel,
        out_shape=jax.ShapeDtypeStruct((bsz, s, D_MODEL), BF16),
        grid=(bsz, s // tt),
        in_specs=[seq, seq, seq, _resident(cw.shape, fix2), _resident(cb.shape, fix2), _resident(wax.shape, fix3),
                  _resident(bax.shape, fix2), _resident(lam.shape, fix2), _resident(wo.shape, fix2)],
        out_specs=seq,
        scratch_shapes=[pltpu.VMEM((SUBLANES, D_MODEL), F32), pltpu.VMEM((SUBLANES, D_MODEL), F32),
                        pltpu.VMEM((nblk, SUBLANES, D_MODEL), F32), pltpu.VMEM((nblk, SUBLANES, D_MODEL), F32),
                        pltpu.VMEM((nblk, SUBLANES, D_MODEL), F32)],
        compiler_params=_params(("parallel", "arbitrary"), est),
        name="lru",
    )(lin, lgate, gl, cw, cb, wax, bax, lam, wo)


def _attn_kernel(lam_ref, q_ref, k_ref, vt_ref, sub_ref, o_ref, m_ref, l_ref, acc_ref, qz_ref, sa_ref, sb_ref, *,
                 tq, out_scale):
    i = pl.program_id(2)
    qt = q_ref[...].astype(F32).T
    row = lax.broadcasted_iota(I32, (V_DIM, tq), 0)
    zero = jnp.zeros_like(qt)
    qz = jnp.concatenate([jnp.where(row < HEAD_DIM, qt, zero), jnp.where(row >= HEAD_DIM, qt, zero)],
                         axis=1).astype(BF16)
    m_ref[...] = jnp.full_like(m_ref, NEG_INF)
    l_ref[...] = jnp.zeros_like(l_ref)
    acc_ref[...] = jnp.zeros_like(acc_ref)

    qz_ref[...] = qz
    cw = ATTN_COL_GROUP

    def block_off(j):
        return pl.multiple_of(j * tq, tq)

    def scores(j, s_ref):
        s_ref[...] = jnp.dot(k_ref[pl.ds(block_off(j), tq), :], qz_ref[...], preferred_element_type=F32)

    def softmax_pv(j, s_ref, diagonal):
        vtb = vt_ref[:, pl.ds(block_off(j), tq)]
        for g in range(2 * tq // cw):
            cols = pl.ds(g * cw, cw)
            s = s_ref[:, cols]
            if diagonal:
                key = lax.broadcasted_iota(I32, (tq, cw), 0)
                qry = (lax.broadcasted_iota(I32, (tq, cw), 1) + g * cw) % tq
                s = jnp.where((key // CHUNK) <= (qry // CHUNK), s, NEG_INF)
            m_prev = m_ref[:, cols]
            m_new = jnp.maximum(m_prev, jnp.max(s, axis=0, keepdims=True))
            alpha = jnp.exp2(m_prev - m_new)
            p = jnp.exp2(s - m_new)
            l_ref[:, cols] = alpha * l_ref[:, cols] + jnp.sum(p, axis=0, keepdims=True)
            acc_ref[:, cols] = alpha * acc_ref[:, cols] + jnp.dot(vtb, p.astype(BF16), preferred_element_type=F32)
            m_ref[:, cols] = m_new

    scores(0, sa_ref)

    def pair(p, carry):
        j = 2 * p
        scores(j + 1, sb_ref)
        softmax_pv(j, sa_ref, False)
        scores(j + 2, sa_ref)
        softmax_pv(j + 1, sb_ref, False)
        return carry

    lax.fori_loop(0, i // 2, pair, 0)

    @pl.when(i % 2 == 1)
    def _():
        scores(i, sb_ref)
        softmax_pv(i - 1, sa_ref, False)
        softmax_pv(i, sb_ref, True)

    @pl.when(i % 2 == 0)
    def _():
        softmax_pv(i, sa_ref, True)

    o12 = acc_ref[...] * (1.0 / l_ref[...])
    ot = o12[:, :tq] - lam_ref[0, 0] * o12[:, tq:]
    ot = ot * lax.rsqrt(jnp.mean(ot * ot, axis=0, keepdims=True) + EPS) * sub_ref[...] * out_scale
    o_ref[...] = ot.T.astype(BF16)


def _diff_attn(lam, q, k, vt, sub, lambda_init, tq):
    bsz, s, _ = q.shape
    qspec = pl.BlockSpec((None, tq, V_DIM), lambda b, h, i: (b, i, h))
    kspec = pl.BlockSpec((None, s, V_DIM), lambda b, h, i: (b, 0, h))
    vtspec = pl.BlockSpec((None, V_DIM, s), lambda b, h, i: (b, h, 0))
    est = 2 * 2 * s * V_DIM * 2 + 4 * tq * V_DIM * 2 + V_DIM * 2 * tq * 4 + 5 * 2 * tq * tq * 4
    return pl.pallas_call(
        functools.partial(_attn_kernel, tq=tq, out_scale=1.0 - lambda_init),
        out_shape=jax.ShapeDtypeStruct((bsz, s, ATTN_HEADS * V_DIM), BF16),
        grid=(bsz, ATTN_HEADS, s // tq),
        in_specs=[pl.BlockSpec(memory_space=pltpu.SMEM), qspec, kspec, vtspec,
                  pl.BlockSpec((V_DIM, 1), lambda b, h, i: (0, 0))],
        out_specs=qspec,
        scratch_shapes=[pltpu.VMEM((1, 2 * tq), F32), pltpu.VMEM((1, 2 * tq), F32),
                        pltpu.VMEM((V_DIM, 2 * tq), F32), pltpu.VMEM((V_DIM, 2 * tq), BF16),
                        pltpu.VMEM((tq, 2 * tq), F32), pltpu.VMEM((tq, 2 * tq), F32)],
        compiler_params=_params(("parallel", "parallel", "parallel"), est),
        name="diff_attn",
    )(lam, q, k, vt, sub)


def _mix_out_kernel(x_ref, o_ref, ga_ref, ml_ref, wao_ref, wout_ref, x1_ref):
    ya = jnp.dot(o_ref[...], wao_ref[...], preferred_element_type=F32)
    mixed = ml_ref[...].astype(F32) + jax.nn.sigmoid(ga_ref[...].astype(F32)) * ya
    x1_ref[...] = x_ref[...] + jnp.dot(mixed.astype(BF16), wout_ref[...], preferred_element_type=F32)


def _mix_out(x2d, o, ga, ml, wao, wout, tm):
    n = x2d.shape[0]
    tok = pl.BlockSpec((tm, D_MODEL), lambda i: (i, 0))
    fixed = lambda i: (0, 0)
    est = 2 * 2 * tm * D_MODEL * 4 + 3 * 2 * tm * D_MODEL * 2 + 2 * D_MODEL * D_MODEL * 2 + 3 * tm * D_MODEL * 4
    return pl.pallas_call(
        _mix_out_kernel,
        out_shape=jax.ShapeDtypeStruct((n, D_MODEL), F32),
        grid=(n // tm,),
        in_specs=[tok, tok, tok, tok, _resident(wao.shape, fixed), _resident(wout.shape, fixed)],
        out_specs=tok,
        compiler_params=_params(("parallel",), est),
        name="mix_out",
    )(x2d, o, ga, ml, wao, wout)


def _mem_kv_kernel(mem_ref, g_ref, w_ref, ckg_ref, k_ref, v_ref):
    h = _rms(mem_ref[...], g_ref[...]).astype(BF16)
    kv = jnp.dot(h, w_ref[...], preferred_element_type=F32)
    ks = [_rms(kv[:, hd * MEM_HEAD_DIM:(hd + 1) * MEM_HEAD_DIM], ckg_ref[...]) for hd in range(MEM_HEADS)]
    k_ref[...] = jnp.concatenate(ks, axis=1).astype(BF16)
    v_ref[...] = kv[:, MEM_WIDTH:].astype(BF16)


def _mem_kv(mem, g, w, ckg):
    bsz, m, _ = mem.shape
    fixed = lambda b: (0, 0)
    out = pl.BlockSpec((None, m, MEM_WIDTH), lambda b: (b, 0, 0))
    est = 2 * m * D_MODEL * 4 + w.size * 2 + 4 * m * MEM_WIDTH * 2 + 4 * m * D_MODEL * 4
    return pl.pallas_call(
        _mem_kv_kernel,
        out_shape=[jax.ShapeDtypeStruct((bsz, m, MEM_WIDTH), BF16)] * 2,
        grid=(bsz,),
        in_specs=[pl.BlockSpec((None, m, D_MODEL), lambda b: (b, 0, 0)), _resident((1, D_MODEL), fixed),
                  _resident(w.shape, fixed), _resident((1, MEM_HEAD_DIM), fixed)],
        out_specs=[out, out],
        compiler_params=_params(("parallel",), est),
        name="mem_kv",
    )(mem, g, w, ckg)


def _split_bf16(x):
    hi = x.astype(BF16)
    return hi, (x - hi.astype(F32)).astype(BF16)


def _cross_router_kernel(x1_ref, gcx_ref, wcq_ref, cqg_ref, kc_ref, vc_ref, wco_ref, gffn_ref, wrh_ref, wrl_ref,
                         br_ref, x2_ref, hn_ref, eid_ref, ew_ref, hist_ref):
    x1 = x1_ref[...]
    q = jnp.dot(_rms(x1, gcx_ref[...]).astype(BF16), wcq_ref[...], preferred_element_type=F32)
    outs = []
    for hd in range(MEM_HEADS):
        sl = slice(hd * MEM_HEAD_DIM, (hd + 1) * MEM_HEAD_DIM)
        qh = _rms(q[:, sl], cqg_ref[...]) * MEM_HEAD_DIM ** -0.5
        s = lax.dot_general(qh.astype(BF16), kc_ref[:, sl], (((1,), (1,)), ((), ())), preferred_element_type=F32)
        p = jnp.exp(s - jnp.max(s, axis=1, keepdims=True))
        o = jnp.dot(p.astype(BF16), vc_ref[:, sl], preferred_element_type=F32)
        outs.append(o / jnp.sum(p, axis=1, keepdims=True))
    x2 = x1 + jnp.dot(jnp.concatenate(outs, axis=1).astype(BF16), wco_ref[...], preferred_element_type=F32)
    x2_ref[...] = x2

    hn = _rms(x2, gffn_ref[...])
    hn_ref[...] = hn.astype(BF16)
    h_hi, h_lo = _split_bf16(hn)
    logits = (jnp.dot(h_hi, wrh_ref[...], preferred_element_type=F32)
              + jnp.dot(h_lo, wrh_ref[...], preferred_element_type=F32)
              + jnp.dot(h_hi, wrl_ref[...], preferred_element_type=F32)) + br_ref[...]
    lane = lax.broadcasted_iota(I32, logits.shape, 1)
    is_group = lane < N_GROUPS
    gl = jnp.where(is_group, logits, NEG_INF)
    gmax = jnp.max(gl, axis=1, keepdims=True)
    gval = 1.0 / jnp.sum(jnp.where(is_group, jnp.exp(gl - gmax), 0.0), axis=1, keepdims=True)
    gidx = jnp.min(jnp.where(gl == gmax, lane, LANES), axis=1, keepdims=True)
    lane_group = lax.shift_right_logical(lane + (EXPERTS_PER_GROUP - N_GROUPS), 3) - 1
    chosen = lane_group == gidx
    el = jnp.where(chosen, logits, NEG_INF)
    v1 = jnp.max(el, axis=1, keepdims=True)
    i1 = jnp.min(jnp.where(chosen & (el == v1), lane, LANES), axis=1, keepdims=True)
    rest = chosen & (lane != i1)
    el2 = jnp.where(rest, logits, NEG_INF)
    v2 = jnp.max(el2, axis=1, keepdims=True)
    i2 = jnp.min(jnp.where(rest & (el2 == v2), lane, LANES), axis=1, keepdims=True)
    t = jnp.exp(v2 - v1)
    w1 = gval / (1.0 + t)
    w2 = gval * t / (1.0 + t)
    e1, e2 = i1 - N_GROUPS, i2 - N_GROUPS
    eid_ref[...] = jnp.where(lane == 0, e1, jnp.where(lane == 1, e2, 0))
    ew_ref[...] = jnp.where(lane == 0, w1, jnp.where(lane == 1, w2, 0.0))
    count = jnp.sum((lane == e1).astype(I32) + (lane == e2).astype(I32), axis=0, keepdims=True)
    hist_ref[...] = jnp.broadcast_to(count, hist_ref.shape)


def _cross_router(x1, gcx, wcq, cqg, kc, vc, wco, gffn, wrh, wrl, br, tm):
    bsz, s, _ = x1.shape
    m = kc.shape[1]
    tok = pl.BlockSpec((None, tm, D_MODEL), lambda b, i: (b, i, 0))
    small = pl.BlockSpec((None, tm, LANES), lambda b, i: (b, i, 0))
    memb = pl.BlockSpec((None, m, MEM_WIDTH), lambda b, i: (b, 0, 0))
    fixed = lambda b, i: (0, 0)
    est = (3 * 2 * tm * D_MODEL * 4 + 2 * 2 * tm * LANES * 4 + 2 * 2 * m * MEM_WIDTH * 2 + 2 * D_MODEL * MEM_WIDTH * 2
           + 2 * D_MODEL * LANES * 2 + 6 * tm * D_MODEL * 4)
    return pl.pallas_call(
        _cross_router_kernel,
        out_shape=[jax.ShapeDtypeStruct((bsz, s, D_MODEL), F32), jax.ShapeDtypeStruct((bsz, s, D_MODEL), BF16),
                   jax.ShapeDtypeStruct((bsz, s, LANES), I32), jax.ShapeDtypeStruct((bsz, s, LANES), F32),
                   jax.ShapeDtypeStruct((bsz, s // tm, SUBLANES, LANES), I32)],
        grid=(bsz, s // tm),
        in_specs=[tok, _resident((1, D_MODEL), fixed), _resident(wcq.shape, fixed), _resident((1, MEM_HEAD_DIM), fixed),
                  memb, memb, _resident(wco.shape, fixed), _resident((1, D_MODEL), fixed),
                  _resident(wrh.shape, fixed), _resident(wrl.shape, fixed), _resident((1, LANES), fixed)],
        out_specs=[tok, tok, small, small,
                   pl.BlockSpec((None, None, SUBLANES, LANES), lambda b, i: (b, i, 0, 0))],
        compiler_params=_params(("parallel", "parallel"), est),
        name="cross_router",
    )(x1, gcx, wcq, cqg, kc, vc, wco, gffn, wrh, wrl, br)


MOE_CHUNK = SUBLANES


def _local_rows(tm):
    return TOP_K * tm + N_EXPERTS * MOE_CHUNK


def _segment_loop(t, lst_ref, gofs_ref, nch_ref, fn):
    def per_expert(e, carry):
        k = t * N_EXPERTS + e
        lst, gofs = lst_ref[k], gofs_ref[k]

        def per_chunk(c, cc):
            fn(pl.multiple_of(lst + c * MOE_CHUNK, MOE_CHUNK), pl.multiple_of(gofs + c * MOE_CHUNK, MOE_CHUNK))
            return cc
        lax.fori_loop(0, nch_ref[k], per_chunk, 0)
        return carry
    lax.fori_loop(0, N_EXPERTS, per_expert, 0)


def _repeat(count, fn):
    def body(c, carry):
        fn()
        return carry
    lax.fori_loop(0, count, body, 0)


def _dispatch_kernel(gofs_ref, lst_ref, nch_ref, ntot_ref, tail_ref, tailn_ref, tailtot_ref,
                     hn_ref, eid_ref, lrow_ref, xs_hbm, before_ref, xloc_ref, zero_ref, sem, *, tm):
    t = pl.program_id(0)
    last = pl.num_programs(0) - 1
    slot = t % 2
    na = TOP_K * tm
    loc = xloc_ref.shape[1]

    @pl.when(t == 0)
    def _():
        r = lax.broadcasted_iota(I32, (na, na), 0)
        c = lax.broadcasted_iota(I32, (na, na), 1)
        before_ref[...] = (r < c).astype(BF16)
        zero_ref[...] = jnp.zeros_like(zero_ref)

    et = eid_ref[...].astype(F32).T
    e_row = jnp.concatenate([et[0:1], et[1:2]], axis=1)
    hit = lax.broadcasted_iota(I32, (LANES, na), 0).astype(F32) == e_row
    hit_b = hit.astype(BF16)
    rank = jnp.dot(hit_b, before_ref[...], preferred_element_type=F32)
    start = jnp.dot(lrow_ref[...].astype(BF16), hit_b, preferred_element_type=F32)[0:1] * MOE_CHUNK
    pos = (start + jnp.sum(jnp.where(hit, rank, 0.0), axis=0, keepdims=True)).astype(I32)
    r = lax.broadcasted_iota(I32, (loc, tm), 0)
    sel = ((r == pos[:, :tm]) | (r == pos[:, tm:])).astype(BF16)
    xloc_ref[slot] = jnp.dot(sel, hn_ref[...], preferred_element_type=F32)

    def copy_out(local_row, global_row, s):
        return pltpu.make_async_copy(xloc_ref.at[s, pl.ds(local_row, MOE_CHUNK), :],
                                     xs_hbm.at[pl.ds(global_row, MOE_CHUNK), :], sem.at[s])

    _segment_loop(t, lst_ref, gofs_ref, nch_ref, lambda lr, gr: copy_out(lr, gr, slot).start())

    @pl.when(t > 0)
    def _():
        _repeat(ntot_ref[jnp.maximum(t - 1, 0)], lambda: copy_out(0, 0, 1 - slot).wait())

    @pl.when(t == last)
    def _():
        _repeat(ntot_ref[t], lambda: copy_out(0, 0, slot).wait())

        def zero_out(global_row):
            return pltpu.make_async_copy(zero_ref, xs_hbm.at[pl.ds(global_row, MOE_CHUNK), :], sem.at[0])

        def per_expert(e, carry):
            def per_chunk(c, cc):
                zero_out(pl.multiple_of(tail_ref[e] + c * MOE_CHUNK, MOE_CHUNK)).start()
                return cc
            lax.fori_loop(0, tailn_ref[e], per_chunk, 0)
            return carry
        lax.fori_loop(0, N_EXPERTS, per_expert, 0)
        _repeat(tailtot_ref[0], lambda: zero_out(0).wait())


def _dispatch(plan, hn2d, eid2d, lrow, p_rows, tm):
    n = hn2d.shape[0]
    na = TOP_K * tm
    loc = _local_rows(tm)
    pre = (plan["gofs"], plan["lst"], plan["nch"], plan["ntot"], plan["tail"], plan["tailn"], plan["tailtot"])
    est = 2 * tm * D_MODEL * 2 + na * na * 2 + 2 * loc * D_MODEL * 4 + loc * tm * 4 + 4 * LANES * na * 4
    grid_spec = pltpu.PrefetchScalarGridSpec(
        num_scalar_prefetch=len(pre),
        grid=(n // tm,),
        in_specs=[pl.BlockSpec((tm, D_MODEL), lambda t, *_: (t, 0)), pl.BlockSpec((tm, LANES), lambda t, *_: (t, 0)),
                  pl.BlockSpec((None, SUBLANES, LANES), lambda t, *_: (t, 0, 0))],
        out_specs=pl.BlockSpec(memory_space=pl.ANY),
        scratch_shapes=[pltpu.VMEM((na, na), BF16), pltpu.VMEM((2, loc, D_MODEL), F32),
                        pltpu.VMEM((MOE_CHUNK, D_MODEL), F32), pltpu.SemaphoreType.DMA((2,))],
    )
    return pl.pallas_call(
        functools.partial(_dispatch_kernel, tm=tm),
        out_shape=jax.ShapeDtypeStruct((p_rows, D_MODEL), F32),
        grid_spec=grid_spec,
        compiler_params=_params(("arbitrary",), est),
        name="moe_dispatch",
    )(*pre, hn2d, eid2d, lrow)


def _expert_kernel(blk_e_ref, blk_src_ref, blk_n_ref, xs_ref, wgu_ref, wd_ref, ys_ref):
    @pl.when(blk_n_ref[pl.program_id(0)] > 0)
    def _():
        gu = jnp.dot(xs_ref[...].astype(BF16), wgu_ref[0], preferred_element_type=F32)
        act = (jax.nn.silu(gu[:, :EXPERT_FF]) * gu[:, EXPERT_FF:]).astype(BF16)
        ys_ref[...] = jnp.dot(act, wd_ref[0], preferred_element_type=F32)


def _experts(plan, xs, wgu, wd, eb):
    n_blk = xs.shape[0] // eb
    rows = pl.BlockSpec((eb, D_MODEL), lambda i, be, bs, bn: (bs[i], 0))
    est = 2 * 2 * eb * D_MODEL * 4 + 2 * (wgu.shape[1] * wgu.shape[2] + wd.shape[1] * wd.shape[2]) * 2 \
        + 4 * eb * D_MODEL * 4
    grid_spec = pltpu.PrefetchScalarGridSpec(
        num_scalar_prefetch=3,
        grid=(n_blk,),
        in_specs=[rows, pl.BlockSpec((1,) + wgu.shape[1:], lambda i, be, bs, bn: (be[i], 0, 0)),
                  pl.BlockSpec((1,) + wd.shape[1:], lambda i, be, bs, bn: (be[i], 0, 0))],
        out_specs=rows,
    )
    return pl.pallas_call(
        _expert_kernel,
        out_shape=jax.ShapeDtypeStruct(xs.shape, F32),
        grid_spec=grid_spec,
        compiler_params=_params(("arbitrary",), est),
        name="experts",
    )(plan["blk_e"], plan["blk_src"], plan["blk_n"], xs, wgu, wd)


def _combine_kernel(gofs_ref, lst_ref, nch_ref, ntot_ref, x2_ref, ew_ref, eid_ref, lrow_ref, ys_hbm, o_ref,
                    before_ref, yloc_ref, sem, *, tm):
    t = pl.program_id(0)
    last = pl.num_programs(0) - 1
    slot = t % 2
    na = TOP_K * tm
    loc = yloc_ref.shape[1]

    def copy_in(local_row, global_row, s):
        return pltpu.make_async_copy(ys_hbm.at[pl.ds(global_row, MOE_CHUNK), :],
                                     yloc_ref.at[s, pl.ds(local_row, MOE_CHUNK), :], sem.at[s])

    def fetch(tile, s):
        _segment_loop(tile, lst_ref, gofs_ref, nch_ref, lambda lr, gr: copy_in(lr, gr, s).start())

    @pl.when(t == 0)
    def _():
        r = lax.broadcasted_iota(I32, (na, na), 0)
        c = lax.broadcasted_iota(I32, (na, na), 1)
        before_ref[...] = (c < r).astype(BF16)
        yloc_ref[...] = jnp.zeros_like(yloc_ref)
        fetch(0, 0)

    @pl.when(t < last)
    def _():
        fetch(t + 1, 1 - slot)

    _repeat(ntot_ref[t], lambda: copy_in(0, 0, slot).wait())

    eid = eid_ref[...]
    lane = lax.broadcasted_iota(I32, (tm, LANES), 1)
    hit = jnp.concatenate([lane == eid[:, 0:1], lane == eid[:, 1:2]], axis=0)
    rank = jnp.dot(before_ref[...], hit.astype(BF16), preferred_element_type=F32)
    start = lrow_ref[0:1, :] * MOE_CHUNK
    pos = jnp.sum(jnp.where(hit, rank + start, 0.0), axis=1, keepdims=True).astype(I32)
    col = lax.broadcasted_iota(I32, (tm, loc), 1)
    yb = yloc_ref[slot].astype(BF16)
    y0 = jnp.dot((col == pos[:tm]).astype(BF16), yb, preferred_element_type=F32)
    y1 = jnp.dot((col == pos[tm:]).astype(BF16), yb, preferred_element_type=F32)
    ew = ew_ref[...]
    o_ref[...] = x2_ref[...] + (ew[:, 0:1] * y0 + ew[:, 1:2] * y1)


def _combine(plan, x2, ew, eid2d, lrow, ys, tm):
    n = x2.shape[0]
    na = TOP_K * tm
    loc = _local_rows(tm)
    pre = (plan["gofs"], plan["lst"], plan["nch"], plan["ntot"])
    tok = pl.BlockSpec((tm, D_MODEL), lambda t, *_: (t, 0))
    small = pl.BlockSpec((tm, LANES), lambda t, *_: (t, 0))
    est = 2 * 2 * tm * D_MODEL * 4 + na * na * 2 + 2 * loc * D_MODEL * 4 + loc * D_MODEL * 2 + 2 * tm * loc * 4 \
        + 3 * tm * D_MODEL * 4
    grid_spec = pltpu.PrefetchScalarGridSpec(
        num_scalar_prefetch=len(pre),
        grid=(n // tm,),
        in_specs=[tok, small, small, pl.BlockSpec((None, SUBLANES, LANES), lambda t, *_: (t, 0, 0)),
                  pl.BlockSpec(memory_space=pl.ANY)],
        out_specs=tok,
        scratch_shapes=[pltpu.VMEM((na, na), BF16), pltpu.VMEM((2, loc, D_MODEL), F32),
                        pltpu.SemaphoreType.DMA((2,))],
    )
    return pl.pallas_call(
        functools.partial(_combine_kernel, tm=tm),
        out_shape=jax.ShapeDtypeStruct((n, D_MODEL), F32),
        grid_spec=grid_spec,
        compiler_params=_params(("arbitrary",), est),
        name="moe_combine",
    )(*pre, x2, ew, eid2d, lrow, ys)


def _rope_tables(positions):
    inv_freq = jnp.exp(-math.log(ROPE_THETA) * jnp.arange(ROT_HALF, dtype=F32) / ROT_HALF)
    ang = positions.astype(F32).reshape(-1, 1) * inv_freq
    cos, sin = jnp.cos(ang), jnp.sin(ang)
    n = ang.shape[0]
    rest = HEAD_DIM - ROT_DIMS
    cos_seg = jnp.concatenate([cos, cos, jnp.ones((n, rest), F32)], axis=1)
    lo_seg = jnp.concatenate([-sin, jnp.zeros((n, HEAD_DIM - ROT_HALF), F32)], axis=1)
    hi_seg = jnp.concatenate([jnp.zeros((n, ROT_HALF), F32), sin, jnp.zeros((n, rest), F32)], axis=1)
    rep = LANES // HEAD_DIM
    return jnp.tile(cos_seg, (1, rep)), jnp.tile(lo_seg, (1, rep)), jnp.tile(hi_seg, (1, rep))


def _plan_rows(n_tok, tm, eb):
    worst = n_tok * TOP_K + (n_tok // tm) * N_EXPERTS * (MOE_CHUNK - 1)
    return (worst + eb - 1) // eb * eb + N_EXPERTS * eb


def _moe_plan(hist, p_rows, eb):
    n_tiles = hist.shape[0]
    seg = (hist + MOE_CHUNK - 1) // MOE_CHUNK * MOE_CHUNK
    tot = jnp.sum(seg, axis=0)
    region = (tot + eb - 1) // eb * eb
    pend = jnp.cumsum(region)
    pstart = pend - region
    gofs = pstart[None, :] + jnp.cumsum(seg, axis=0) - seg
    lst = jnp.cumsum(seg, axis=1) - seg
    nch = seg // MOE_CHUNK
    n_blk = p_rows // eb
    blk_first = jnp.arange(n_blk, dtype=I32) * eb
    blk_e = jnp.minimum(jnp.sum(blk_first[:, None] >= pend[None, :], axis=1), N_EXPERTS - 1).astype(I32)
    blk_n = jnp.clip(pstart[blk_e] + tot[blk_e] - blk_first, 0, eb)
    blk_src = jnp.minimum(jnp.arange(n_blk, dtype=I32), jnp.maximum(pend[-1] // eb - 1, 0))
    tailn = (region - tot) // MOE_CHUNK
    lrow = jnp.pad((lst // MOE_CHUNK).astype(F32), ((0, 0), (0, LANES - N_EXPERTS)))
    flat = lambda a: a.reshape(-1).astype(I32)
    plan = dict(gofs=flat(gofs), lst=flat(lst), nch=flat(nch), ntot=flat(jnp.sum(nch, axis=1)),
                tail=flat(pstart + tot), tailn=flat(tailn), tailtot=flat(jnp.sum(tailn)),
                blk_e=blk_e, blk_src=flat(blk_src), blk_n=flat(blk_n))
    return plan, jnp.broadcast_to(lrow[:, None, :], (n_tiles, SUBLANES, LANES))


def _tile(n, pref):
    t = min(n, pref)
    assert n % t == 0, (n, pref)
    return t


def kernel(x, mem, positions, norm_mix, w_in, conv_w, conv_b, lru_wa, lru_ba, lru_wx, lru_bx, lru_lambda, w_lru_o, q_norm, k_norm, lambda_q1, lambda_k1, lambda_q2, lambda_k2, subln, w_attn_o, w_out, norm_cx, norm_mem, w_cq, w_ckv, cq_norm, ck_norm, w_co, norm_ffn, w_group, b_group, w_router, b_router, w_gate_up, w_down):
    bsz, s, d = x.shape
    assert d == D_MODEL and w_in.shape[-1] == N_PROJ * D_MODEL
    n = bsz * s
    depth = w_in.shape[0]
    tm = _tile(n, 512)
    ts = _tile(s, 512)
    tq = _tile(s, 512)
    assert tq % CHUNK == 0 and ts % SUBLANES == 0
    eb = _tile(n, 512)
    cos_t, sin_lo, sin_hi = _rope_tables(positions)
    row = lambda v: v.reshape(1, -1).astype(F32)
    rep = LANES // HEAD_DIM

    for layer in range(depth):
        lambda_init = 0.8 - 0.6 * math.exp(-0.3 * layer)
        lin, lgate, q, k, vt, gl, ga = _in_proj(
            x.reshape(n, d), row(norm_mix[layer]), w_in[layer].astype(BF16),
            jnp.tile(row(q_norm[layer]), (1, rep)), jnp.tile(row(k_norm[layer]), (1, rep)), cos_t, sin_lo, sin_hi,
            ts, s)
        seq = lambda a: a.reshape(bsz, s, d)

        wax = jnp.concatenate([lru_wa[layer], lru_wx[layer]], axis=-1).astype(BF16)
        bax = jnp.stack([lru_ba[layer], lru_bx[layer]]).astype(F32)
        ml = _lru(seq(lin), seq(lgate), seq(gl), conv_w[layer].astype(F32), row(conv_b[layer]), wax, bax,
                  row(lru_lambda[layer]), w_lru_o[layer].astype(BF16), ts)

        lam = (jnp.exp(jnp.sum(lambda_q1[layer].astype(F32) * lambda_k1[layer].astype(F32)))
               - jnp.exp(jnp.sum(lambda_q2[layer].astype(F32) * lambda_k2[layer].astype(F32))) + lambda_init)
        o = _diff_attn(lam.reshape(1, 1), seq(q), seq(k), vt, subln[layer].astype(F32).reshape(V_DIM, 1),
                       lambda_init, tq)

        x1 = _mix_out(x.reshape(n, d), o.reshape(n, d), ga, ml.reshape(n, d), w_attn_o[layer].astype(BF16),
                      w_out[layer].astype(BF16), tm)

        kc, vc = _mem_kv(mem, row(norm_mem[layer]), w_ckv[layer].astype(BF16), row(ck_norm[layer]))
        w_r = jnp.concatenate([w_group[layer], w_router[layer],
                               jnp.zeros((d, LANES - N_GROUPS - N_EXPERTS), F32)], axis=1).astype(F32)
        b_r = jnp.concatenate([b_group[layer], b_router[layer],
                               jnp.zeros((LANES - N_GROUPS - N_EXPERTS,), F32)]).reshape(1, LANES).astype(F32)
        wrh, wrl = _split_bf16(w_r)
        x2, hn, eid, ew, hist = _cross_router(
            seq(x1), row(norm_cx[layer]), w_cq[layer].astype(BF16), row(cq_norm[layer]), kc, vc,
            w_co[layer].astype(BF16), row(norm_ffn[layer]), wrh, wrl, b_r, ts)

        p_rows = _plan_rows(n, ts, eb)
        plan, lrow = _moe_plan(hist[:, :, 0, :N_EXPERTS].reshape(n // ts, N_EXPERTS), p_rows, eb)
        eid2d = eid.reshape(n, LANES)
        xs = _dispatch(plan, hn.reshape(n, d), eid2d, lrow, p_rows, ts)
        ys = _experts(plan, xs, w_gate_up[layer].astype(BF16), w_down[layer].astype(BF16), eb)
        x = _combine(plan, x2.reshape(n, d), ew.reshape(n, LANES), eid2d, lrow, ys, ts).reshape(bsz, s, d)
    return x
```

```python
import functools
import math

import jax
import jax.numpy as jnp
from jax import lax
from jax.experimental import pallas as pl
from jax.experimental.pallas import tpu as pltpu

F32 = jnp.float32
BF16 = jnp.bfloat16
I32 = jnp.int32

D_MODEL = 1024
CHUNK = 64
LRU_BLOCKS = 8
LRU_BLOCK_WIDTH = D_MODEL // LRU_BLOCKS
CONV_WIDTH = 4
LRU_C = 8.0
ATTN_HEADS = 8
HEAD_DIM = 64
V_DIM = 2 * HEAD_DIM
ROPE_THETA = 500000.0
ROT_DIMS = HEAD_DIM // 4
ROT_HALF = ROT_DIMS // 2
MEM_HEADS = 4
MEM_HEAD_DIM = 128
MEM_WIDTH = MEM_HEADS * MEM_HEAD_DIM
N_GROUPS = 4
EXPERTS_PER_GROUP = 8
N_EXPERTS = N_GROUPS * EXPERTS_PER_GROUP
TOP_K = 2
EXPERT_FF = 512
N_PROJ = 7
EPS = 1e-6
NEG_INF = -1e30
LOG2_E = math.log2(math.e)
ATTN_COL_GROUP = 512
ATTN_HEADS_PER_STEP = 4

LANES = 128
SUBLANES = 8
V7X_VMEM_BYTES = 64 * 1024 * 1024
MIB = 1024 * 1024


def _vmem_limit(estimate_bytes):
    return int(min(max(estimate_bytes * 3 // 2, 16 * MIB), V7X_VMEM_BYTES - 8 * MIB))


def _params(semantics, vmem_estimate):
    return pltpu.CompilerParams(dimension_semantics=semantics, vmem_limit_bytes=_vmem_limit(vmem_estimate))


def _resident(shape, index_map):
    return pl.BlockSpec(shape, index_map, pipeline_mode=pl.Buffered(1))


def _rms(x, g):
    return x * lax.rsqrt(jnp.mean(x * x, axis=-1, keepdims=True) + EPS) * g


def _sigmoid(x):
    return 0.5 * jnp.tanh(0.5 * x) + 0.5


def _segment_ones():
    r = lax.broadcasted_iota(I32, (LANES, LANES), 0) // HEAD_DIM
    c = lax.broadcasted_iota(I32, (LANES, LANES), 1) // HEAD_DIM
    return (r == c).astype(BF16)


def _qk_post(p, gain, cos_t, sin_lo, sin_hi, seg, scale):
    cols = []
    for c in range(D_MODEL // LANES):
        pc = p[:, c * LANES:(c + 1) * LANES]
        ss = jnp.dot((pc * pc).astype(BF16), seg, preferred_element_type=F32)
        y = pc * lax.rsqrt(ss * (1.0 / HEAD_DIM) + EPS) * gain
        y = y * cos_t + pltpu.roll(y, LANES - ROT_HALF, 1) * sin_lo + pltpu.roll(y, ROT_HALF, 1) * sin_hi
        cols.append((y * scale).astype(BF16))
    return jnp.concatenate(cols, axis=1)


def _in_proj_kernel(x_ref, g_ref, w_ref, qg_ref, kg_ref, rope_ref,
                    lin_ref, lgate_ref, q_ref, k_ref, vt_ref, gl_ref, ga_ref):
    h = _rms(x_ref[...], g_ref[...]).astype(BF16)

    def proj(j):
        return jnp.dot(h, w_ref[:, j * D_MODEL:(j + 1) * D_MODEL], preferred_element_type=F32)

    lin_ref[...] = proj(0).astype(BF16)
    lgate_ref[...] = proj(1).astype(BF16)
    seg = _segment_ones()
    tab = rope_ref[...]
    seg_lane = lax.broadcasted_iota(I32, tab.shape, 1) % HEAD_DIM
    first, second = seg_lane < ROT_HALF, (seg_lane >= ROT_HALF) & (seg_lane < ROT_DIMS)
    cos_t = jnp.where(first, tab, jnp.where(second, pltpu.roll(tab, ROT_HALF, 1), 1.0))
    sin_lo = jnp.where(first, -pltpu.roll(tab, LANES - ROT_HALF, 1), 0.0)
    sin_hi = jnp.where(second, tab, 0.0)
    q_ref[...] = _qk_post(proj(2), qg_ref[...], cos_t, sin_lo, sin_hi, seg, HEAD_DIM ** -0.5 * LOG2_E)
    k_ref[...] = _qk_post(proj(3), kg_ref[...], cos_t, sin_lo, sin_hi, seg, 1.0)
    v = proj(4)
    for hd in range(ATTN_HEADS):
        vt_ref[hd * V_DIM:(hd + 1) * V_DIM, :] = v[:, hd * V_DIM:(hd + 1) * V_DIM].T.astype(BF16)
    gl_ref[...] = proj(5).astype(BF16)
    ga_ref[...] = proj(6).astype(BF16)


def _in_proj(x2d, g, w_in, qg, kg, rope, tm, seq_len):
    n = x2d.shape[0]
    per_seq = seq_len // tm
    row = lambda i: (i, 0)
    fixed = lambda i: (0, 0)
    tok = pl.BlockSpec((tm, D_MODEL), row)
    tab = pl.BlockSpec((tm, LANES), row)
    tok_out = jax.ShapeDtypeStruct((n, D_MODEL), BF16)
    vt_out = jax.ShapeDtypeStruct((n // seq_len, D_MODEL, seq_len), BF16)
    vt_spec = pl.BlockSpec((None, D_MODEL, tm), lambda i: (i // per_seq, 0, i % per_seq))
    est = (w_in.size * 2 + 2 * tm * D_MODEL * 4 + 6 * tm * LANES * 4 + N_PROJ * 2 * tm * D_MODEL * 2
           + 4 * tm * D_MODEL * 4)
    return pl.pallas_call(
        _in_proj_kernel,
        out_shape=[tok_out] * 4 + [vt_out] + [tok_out] * 2,
        grid=(n // tm,),
        in_specs=[tok, _resident((1, D_MODEL), fixed), _resident(w_in.shape, fixed),
                  _resident((1, LANES), fixed), _resident((1, LANES), fixed), tab],
        out_specs=[tok] * 4 + [vt_spec] + [tok] * 2,
        compiler_params=_params(("parallel",), est),
        name="in_proj",
    )(x2d, g, w_in, qg, kg, rope)


def _lru_kernel(lin_ref, lgate_ref, gl_ref, cw_ref, cb_ref, wax_ref, bax_ref, lam_ref, wo_ref,
                out_ref, xprev_ref, hprev_ref, a_ref, b_ref, h_ref):
    t = pl.program_id(1)

    @pl.when(t == 0)
    def _():
        xprev_ref[...] = jnp.zeros_like(xprev_ref)
        hprev_ref[...] = jnp.zeros_like(hprev_ref)

    x = lin_ref[...].astype(F32)
    tt = x.shape[0]
    nblk = tt // SUBLANES
    xp = jnp.concatenate([xprev_ref[...], x], axis=0)
    cw = cw_ref[...]
    xc = cb_ref[...] + cw[3:4] * x
    for j in range(1, CONV_WIDTH):
        xc = xc + cw[CONV_WIDTH - 1 - j:CONV_WIDTH - j] * xp[SUBLANES - j:SUBLANES - j + tt]
    xprev_ref[...] = x[tt - SUBLANES:tt]

    xcb = xc.astype(BF16)
    ra, ri = [], []
    for n in range(LRU_BLOCKS):
        g = jnp.dot(xcb[:, n * LRU_BLOCK_WIDTH:(n + 1) * LRU_BLOCK_WIDTH], wax_ref[n], preferred_element_type=F32)
        ra.append(g[:, :LRU_BLOCK_WIDTH])
        ri.append(g[:, LRU_BLOCK_WIDTH:])
    bax = bax_ref[...]
    r = _sigmoid(jnp.concatenate(ra, axis=1) + bax[0:1])
    i = _sigmoid(jnp.concatenate(ri, axis=1) + bax[1:2])
    log_a = -LRU_C * r * jax.nn.softplus(-lam_ref[...])
    a = jnp.exp(log_a)
    gain2 = -jnp.tanh(log_a) * (a * a + 1.0)
    b = jnp.where(gain2 > 0.0, gain2 * lax.rsqrt(gain2), 0.0) * i * xc

    a3 = a.reshape(nblk, SUBLANES, D_MODEL)
    b3 = b.reshape(nblk, SUBLANES, D_MODEL)
    sub = lax.broadcasted_iota(I32, (nblk, SUBLANES, D_MODEL), 1)
    shift = 1
    while shift < SUBLANES:
        keep = sub >= shift
        a_sh = pltpu.roll(a3, shift, 1)
        b_sh = pltpu.roll(b3, shift, 1)
        b3 = jnp.where(keep, a3 * b_sh + b3, b3)
        a3 = jnp.where(keep, a3 * a_sh, a3)
        shift *= 2
    a_ref[...] = a3
    b_ref[...] = b3

    def carry_step(blk, h_last):
        h = a_ref[blk] * h_last + b_ref[blk]
        h_ref[blk] = h
        return jnp.broadcast_to(h[SUBLANES - 1:SUBLANES], (SUBLANES, D_MODEL))

    hprev_ref[...] = lax.fori_loop(0, nblk, carry_step, hprev_ref[...], unroll=4)
    hr = h_ref[...].reshape(tt, D_MODEL)

    y = (jax.nn.gelu(lgate_ref[...].astype(F32)) * hr).astype(BF16)
    yl = jnp.dot(y, wo_ref[...], preferred_element_type=F32)
    out_ref[...] = (_sigmoid(gl_ref[...].astype(F32)) * yl).astype(BF16)


def _lru(lin, lgate, gl, cw, cb, wax, bax, lam, wo, tt):
    bsz, s, _ = lin.shape
    seq = pl.BlockSpec((None, tt, D_MODEL), lambda b, t: (b, t, 0))
    fix2 = lambda b, t: (0, 0)
    fix3 = lambda b, t: (0, 0, 0)
    nblk = tt // SUBLANES
    est = 4 * 2 * tt * D_MODEL * 2 + wo.size * 2 + wax.size * 2 + 3 * tt * D_MODEL * 4 + 10 * tt * D_MODEL * 4
    return pl.pallas_call(
        _lru_kernel,
        out_shape=jax.ShapeDtypeStruct((bsz, s, D_MODEL), BF16),
        grid=(bsz, s // tt),
        in_specs=[seq, seq, seq, _resident(cw.shape, fix2), _resident(cb.shape, fix2), _resident(wax.shape, fix3),
                  _resident(bax.shape, fix2), _resident(lam.shape, fix2), _resident(wo.shape, fix2)],
        out_specs=seq,
        scratch_shapes=[pltpu.VMEM((SUBLANES, D_MODEL), F32), pltpu.VMEM((SUBLANES, D_MODEL), F32),
                        pltpu.VMEM((nblk, SUBLANES, D_MODEL), F32), pltpu.VMEM((nblk, SUBLANES, D_MODEL), F32),
                        pltpu.VMEM((nblk, SUBLANES, D_MODEL), F32)],
        compiler_params=_params(("parallel", "arbitrary"), est),
        name="lru",
    )(lin, lgate, gl, cw, cb, wax, bax, lam, wo)


def _attn_kernel(lam_ref, q_ref, k_ref, vt_ref, sub_ref, o_ref, m_ref, l_ref, acc_ref, qz_ref, sa_ref, sb_ref, *,
                 tq, out_scale):
    i = pl.program_id(2)
    heads = range(ATTN_HEADS_PER_STEP)
    m_ref[...] = jnp.full_like(m_ref, NEG_INF)
    l_ref[...] = jnp.zeros_like(l_ref)
    acc_ref[...] = jnp.zeros_like(acc_ref)
    row = lax.broadcasted_iota(I32, (V_DIM, tq), 0)
    for h in heads:
        qt = q_ref[:, h * V_DIM:(h + 1) * V_DIM].astype(F32).T
        zero = jnp.zeros_like(qt)
        qz_ref[h] = jnp.concatenate([jnp.where(row < HEAD_DIM, qt, zero), jnp.where(row >= HEAD_DIM, qt, zero)],
                                    axis=1).astype(BF16)
    cw = ATTN_COL_GROUP

    def block_off(j):
        return pl.multiple_of(j * tq, tq)

    def scores(j, s_ref):
        for h in heads:
            kb = k_ref[pl.ds(block_off(j), tq), h * V_DIM:(h + 1) * V_DIM]
            s_ref[h] = jnp.dot(kb, qz_ref[h], preferred_element_type=F32)

    def softmax_pv(j, s_ref, diagonal):
        for h in heads:
            vtb = vt_ref[h * V_DIM:(h + 1) * V_DIM, pl.ds(block_off(j), tq)]
            for g in range(2 * tq // cw):
                cols = pl.ds(g * cw, cw)
                s = s_ref[h, :, cols]
                if diagonal:
                    key = lax.broadcasted_iota(I32, (tq, cw), 0)
                    qry = (lax.broadcasted_iota(I32, (tq, cw), 1) + g * cw) % tq
                    s = jnp.where((key // CHUNK) <= (qry // CHUNK), s, NEG_INF)
                m_prev = m_ref[h, :, cols]
                m_new = jnp.maximum(m_prev, jnp.max(s, axis=0, keepdims=True))
                alpha = jnp.exp2(m_prev - m_new)
                p = jnp.exp2(s - m_new)
                l_ref[h, :, cols] = alpha * l_ref[h, :, cols] + jnp.sum(p, axis=0, keepdims=True)
                acc_ref[h, :, cols] = alpha * acc_ref[h, :, cols] + jnp.dot(vtb, p.astype(BF16),
                                                                            preferred_element_type=F32)
                m_ref[h, :, cols] = m_new

    scores(0, sa_ref)

    def pair(p, carry):
        j = 2 * p
        scores(j + 1, sb_ref)
        softmax_pv(j, sa_ref, False)
        scores(j + 2, sa_ref)
        softmax_pv(j + 1, sb_ref, False)
        return carry

    lax.fori_loop(0, i // 2, pair, 0)

    @pl.when(i % 2 == 1)
    def _():
        scores(i, sb_ref)
        softmax_pv(i - 1, sa_ref, False)
        softmax_pv(i, sb_ref, True)

    @pl.when(i % 2 == 0)
    def _():
        softmax_pv(i, sa_ref, True)

    for h in heads:
        o12 = acc_ref[h] * (1.0 / l_ref[h])
        ot = o12[:, :tq] - lam_ref[0, 0] * o12[:, tq:]
        ot = ot * lax.rsqrt(jnp.mean(ot * ot, axis=0, keepdims=True) + EPS) * sub_ref[...] * out_scale
        o_ref[:, h * V_DIM:(h + 1) * V_DIM] = ot.T.astype(BF16)


def _diff_attn(lam, q, k, vt, sub, lambda_init, tq):
    bsz, s, _ = q.shape
    hps = ATTN_HEADS_PER_STEP
    width = hps * V_DIM
    qspec = pl.BlockSpec((None, tq, width), lambda b, h, i: (b, i, h))
    kspec = pl.BlockSpec((None, s, width), lambda b, h, i: (b, 0, h))
    vtspec = pl.BlockSpec((None, width, s), lambda b, h, i: (b, h, 0))
    est = 2 * 2 * s * width * 2 + 4 * tq * width * 2 + hps * (V_DIM * 2 * tq * 6 + 2 * tq * 2 * tq * 4) \
        + 3 * 2 * tq * tq * 4
    return pl.pallas_call(
        functools.partial(_attn_kernel, tq=tq, out_scale=1.0 - lambda_init),
        out_shape=jax.ShapeDtypeStruct((bsz, s, ATTN_HEADS * V_DIM), BF16),
        grid=(bsz, ATTN_HEADS // hps, s // tq),
        in_specs=[pl.BlockSpec(memory_space=pltpu.SMEM), qspec, kspec, vtspec,
                  pl.BlockSpec((V_DIM, 1), lambda b, h, i: (0, 0))],
        out_specs=qspec,
        scratch_shapes=[pltpu.VMEM((hps, 1, 2 * tq), F32), pltpu.VMEM((hps, 1, 2 * tq), F32),
                        pltpu.VMEM((hps, V_DIM, 2 * tq), F32), pltpu.VMEM((hps, V_DIM, 2 * tq), BF16),
                        pltpu.VMEM((hps, tq, 2 * tq), F32), pltpu.VMEM((hps, tq, 2 * tq), F32)],
        compiler_params=_params(("parallel", "parallel", "parallel"), est),
        name="diff_attn",
    )(lam, q, k, vt, sub)


def _mix_out_kernel(x_ref, o_ref, ga_ref, ml_ref, wao_ref, wout_ref, x1_ref):
    ya = jnp.dot(o_ref[...], wao_ref[...], preferred_element_type=F32)
    mixed = ml_ref[...].astype(F32) + _sigmoid(ga_ref[...].astype(F32)) * ya
    x1_ref[...] = x_ref[...] + jnp.dot(mixed.astype(BF16), wout_ref[...], preferred_element_type=F32)


def _mix_out(x2d, o, ga, ml, wao, wout, tm):
    n = x2d.shape[0]
    tok = pl.BlockSpec((tm, D_MODEL), lambda i: (i, 0))
    fixed = lambda i: (0, 0)
    est = 2 * 2 * tm * D_MODEL * 4 + 3 * 2 * tm * D_MODEL * 2 + 2 * D_MODEL * D_MODEL * 2 + 3 * tm * D_MODEL * 4
    return pl.pallas_call(
        _mix_out_kernel,
        out_shape=jax.ShapeDtypeStruct((n, D_MODEL), F32),
        grid=(n // tm,),
        in_specs=[tok, tok, tok, tok, _resident(wao.shape, fixed), _resident(wout.shape, fixed)],
        out_specs=tok,
        compiler_params=_params(("parallel",), est),
        name="mix_out",
    )(x2d, o, ga, ml, wao, wout)


def _mem_kv_kernel(mem_ref, g_ref, w_ref, ckg_ref, k_ref, v_ref):
    h = _rms(mem_ref[...], g_ref[...]).astype(BF16)
    kv = jnp.dot(h, w_ref[...], preferred_element_type=F32)
    ks = [_rms(kv[:, hd * MEM_HEAD_DIM:(hd + 1) * MEM_HEAD_DIM], ckg_ref[...]) for hd in range(MEM_HEADS)]
    k_ref[...] = jnp.concatenate(ks, axis=1).astype(BF16)
    v_ref[...] = kv[:, MEM_WIDTH:].astype(BF16)


def _mem_kv(mem, g, w, ckg):
    bsz, m, _ = mem.shape
    fixed = lambda b: (0, 0)
    out = pl.BlockSpec((None, m, MEM_WIDTH), lambda b: (b, 0, 0))
    est = 2 * m * D_MODEL * 4 + w.size * 2 + 4 * m * MEM_WIDTH * 2 + 4 * m * D_MODEL * 4
    return pl.pallas_call(
        _mem_kv_kernel,
        out_shape=[jax.ShapeDtypeStruct((bsz, m, MEM_WIDTH), BF16)] * 2,
        grid=(bsz,),
        in_specs=[pl.BlockSpec((None, m, D_MODEL), lambda b: (b, 0, 0)), _resident((1, D_MODEL), fixed),
                  _resident(w.shape, fixed), _resident((1, MEM_HEAD_DIM), fixed)],
        out_specs=[out, out],
        compiler_params=_params(("parallel",), est),
        name="mem_kv",
    )(mem, g, w, ckg)


def _split_bf16(x):
    hi = x.astype(BF16)
    return hi, (x - hi.astype(F32)).astype(BF16)


def _cross_router_kernel(x1_ref, gcx_ref, wcq_ref, cqg_ref, kc_ref, vc_ref, wco_ref, gffn_ref, wrh_ref, wrl_ref,
                         br_ref, x2_ref, hn_ref, eid_ref, ew_ref, hist_ref):
    x1 = x1_ref[...]
    q = jnp.dot(_rms(x1, gcx_ref[...]).astype(BF16), wcq_ref[...], preferred_element_type=F32)
    outs = []
    for hd in range(MEM_HEADS):
        sl = slice(hd * MEM_HEAD_DIM, (hd + 1) * MEM_HEAD_DIM)
        qh = _rms(q[:, sl], cqg_ref[...]) * MEM_HEAD_DIM ** -0.5
        s = lax.dot_general(qh.astype(BF16), kc_ref[:, sl], (((1,), (1,)), ((), ())), preferred_element_type=F32)
        p = jnp.exp(s - jnp.max(s, axis=1, keepdims=True))
        o = jnp.dot(p.astype(BF16), vc_ref[:, sl], preferred_element_type=F32)
        outs.append(o / jnp.sum(p, axis=1, keepdims=True))
    x2 = x1 + jnp.dot(jnp.concatenate(outs, axis=1).astype(BF16), wco_ref[...], preferred_element_type=F32)
    x2_ref[...] = x2

    hn = _rms(x2, gffn_ref[...])
    hn_ref[...] = hn.astype(BF16)
    h_hi, h_lo = _split_bf16(hn)
    logits = (jnp.dot(h_hi, wrh_ref[...], preferred_element_type=F32)
              + jnp.dot(h_lo, wrh_ref[...], preferred_element_type=F32)
              + jnp.dot(h_hi, wrl_ref[...], preferred_element_type=F32)) + br_ref[...]
    lane = lax.broadcasted_iota(I32, logits.shape, 1)
    is_group = lane < N_GROUPS
    gl = jnp.where(is_group, logits, NEG_INF)
    gmax = jnp.max(gl, axis=1, keepdims=True)
    gval = 1.0 / jnp.sum(jnp.where(is_group, jnp.exp(gl - gmax), 0.0), axis=1, keepdims=True)
    gidx = jnp.min(jnp.where(gl == gmax, lane, LANES), axis=1, keepdims=True)
    lane_group = lax.shift_right_logical(lane + (EXPERTS_PER_GROUP - N_GROUPS), 3) - 1
    chosen = lane_group == gidx
    el = jnp.where(chosen, logits, NEG_INF)
    v1 = jnp.max(el, axis=1, keepdims=True)
    i1 = jnp.min(jnp.where(chosen & (el == v1), lane, LANES), axis=1, keepdims=True)
    rest = chosen & (lane != i1)
    el2 = jnp.where(rest, logits, NEG_INF)
    v2 = jnp.max(el2, axis=1, keepdims=True)
    i2 = jnp.min(jnp.where(rest & (el2 == v2), lane, LANES), axis=1, keepdims=True)
    t = jnp.exp(v2 - v1)
    w1 = gval / (1.0 + t)
    w2 = gval * t / (1.0 + t)
    e1, e2 = i1 - N_GROUPS, i2 - N_GROUPS
    eid_ref[...] = jnp.where(lane == 0, e1, jnp.where(lane == 1, e2, 0))
    ew_ref[...] = jnp.where(lane == 0, w1, jnp.where(lane == 1, w2, 0.0))
    count = jnp.sum((lane == e1).astype(I32) + (lane == e2).astype(I32), axis=0, keepdims=True)
    hist_ref[...] = jnp.broadcast_to(count, hist_ref.shape)


def _cross_router(x1, gcx, wcq, cqg, kc, vc, wco, gffn, wrh, wrl, br, tm):
    bsz, s, _ = x1.shape
    m = kc.shape[1]
    tok = pl.BlockSpec((None, tm, D_MODEL), lambda b, i: (b, i, 0))
    small = pl.BlockSpec((None, tm, LANES), lambda b, i: (b, i, 0))
    memb = pl.BlockSpec((None, m, MEM_WIDTH), lambda b, i: (b, 0, 0))
    fixed = lambda b, i: (0, 0)
    est = (3 * 2 * tm * D_MODEL * 4 + 2 * 2 * tm * LANES * 4 + 2 * 2 * m * MEM_WIDTH * 2 + 2 * D_MODEL * MEM_WIDTH * 2
           + 2 * D_MODEL * LANES * 2 + 6 * tm * D_MODEL * 4)
    return pl.pallas_call(
        _cross_router_kernel,
        out_shape=[jax.ShapeDtypeStruct((bsz, s, D_MODEL), F32), jax.ShapeDtypeStruct((bsz, s, D_MODEL), BF16),
                   jax.ShapeDtypeStruct((bsz, s, LANES), I32), jax.ShapeDtypeStruct((bsz, s, LANES), F32),
                   jax.ShapeDtypeStruct((bsz, s // tm, SUBLANES, LANES), I32)],
        grid=(bsz, s // tm),
        in_specs=[tok, _resident((1, D_MODEL), fixed), _resident(wcq.shape, fixed), _resident((1, MEM_HEAD_DIM), fixed),
                  memb, memb, _resident(wco.shape, fixed), _resident((1, D_MODEL), fixed),
                  _resident(wrh.shape, fixed), _resident(wrl.shape, fixed), _resident((1, LANES), fixed)],
        out_specs=[tok, tok, small, small,
                   pl.BlockSpec((None, None, SUBLANES, LANES), lambda b, i: (b, i, 0, 0))],
        compiler_params=_params(("parallel", "parallel"), est),
        name="cross_router",
    )(x1, gcx, wcq, cqg, kc, vc, wco, gffn, wrh, wrl, br)


MOE_CHUNK = SUBLANES


def _local_rows(tm):
    return TOP_K * tm + N_EXPERTS * MOE_CHUNK


def _segment_loop(t, lst_ref, gofs_ref, nch_ref, fn):
    def per_expert(e, carry):
        k = t * N_EXPERTS + e
        lst, gofs = lst_ref[k], gofs_ref[k]

        def per_chunk(c, cc):
            fn(pl.multiple_of(lst + c * MOE_CHUNK, MOE_CHUNK), pl.multiple_of(gofs + c * MOE_CHUNK, MOE_CHUNK))
            return cc
        lax.fori_loop(0, nch_ref[k], per_chunk, 0)
        return carry
    lax.fori_loop(0, N_EXPERTS, per_expert, 0)


def _repeat(count, fn):
    def body(c, carry):
        fn()
        return carry
    lax.fori_loop(0, count, body, 0)


MOE_WAIT_GROUP = 8


def _wait_chunks(count, wait_rows):
    _repeat(lax.shift_right_logical(count, 3), lambda: wait_rows(MOE_WAIT_GROUP * MOE_CHUNK))
    _repeat(count & (MOE_WAIT_GROUP - 1), lambda: wait_rows(MOE_CHUNK))


def _dispatch_kernel(gofs_ref, lst_ref, nch_ref, ntot_ref, tail_ref, tailn_ref, tailtot_ref,
                     hn_ref, eid_ref, lrow_ref, xs_hbm, before_ref, xloc_ref, zero_ref, sem, *, tm):
    t = pl.program_id(0)
    last = pl.num_programs(0) - 1
    slot = t % 2
    na = TOP_K * tm
    loc = xloc_ref.shape[1]

    @pl.when(t == 0)
    def _():
        r = lax.broadcasted_iota(I32, (na, na), 0)
        c = lax.broadcasted_iota(I32, (na, na), 1)
        before_ref[...] = (r < c).astype(BF16)
        zero_ref[...] = jnp.zeros_like(zero_ref)

    et = eid_ref[...].astype(F32).T
    e_row = jnp.concatenate([et[0:1], et[1:2]], axis=1)
    hit = lax.broadcasted_iota(I32, (LANES, na), 0).astype(F32) == e_row
    hit_b = hit.astype(BF16)
    rank = jnp.dot(hit_b, before_ref[...], preferred_element_type=F32)
    start = jnp.dot(lrow_ref[...].astype(BF16), hit_b, preferred_element_type=F32)[0:1] * MOE_CHUNK
    pos = (start + jnp.sum(jnp.where(hit, rank, 0.0), axis=0, keepdims=True)).astype(I32)
    r = lax.broadcasted_iota(I32, (loc, tm), 0)
    sel = ((r == pos[:, :tm]) | (r == pos[:, tm:])).astype(BF16)
    xloc_ref[slot] = jnp.dot(sel, hn_ref[...], preferred_element_type=F32)

    def copy_out(local_row, global_row, s, rows=MOE_CHUNK):
        return pltpu.make_async_copy(xloc_ref.at[s, pl.ds(local_row, rows), :],
                                     xs_hbm.at[pl.ds(global_row, rows), :], sem.at[s])

    _segment_loop(t, lst_ref, gofs_ref, nch_ref, lambda lr, gr: copy_out(lr, gr, slot).start())

    @pl.when(t > 0)
    def _():
        _wait_chunks(ntot_ref[jnp.maximum(t - 1, 0)], lambda rows: copy_out(0, 0, 1 - slot, rows).wait())

    @pl.when(t == last)
    def _():
        _wait_chunks(ntot_ref[t], lambda rows: copy_out(0, 0, slot, rows).wait())

        def zero_out(global_row):
            return pltpu.make_async_copy(zero_ref, xs_hbm.at[pl.ds(global_row, MOE_CHUNK), :], sem.at[0])

        def per_expert(e, carry):
            def per_chunk(c, cc):
                zero_out(pl.multiple_of(tail_ref[e] + c * MOE_CHUNK, MOE_CHUNK)).start()
                return cc
            lax.fori_loop(0, tailn_ref[e], per_chunk, 0)
            return carry
        lax.fori_loop(0, N_EXPERTS, per_expert, 0)
        _repeat(tailtot_ref[0], lambda: zero_out(0).wait())


def _dispatch(plan, hn2d, eid2d, lrow, p_rows, tm):
    n = hn2d.shape[0]
    na = TOP_K * tm
    loc = _local_rows(tm)
    pre = (plan["gofs"], plan["lst"], plan["nch"], plan["ntot"], plan["tail"], plan["tailn"], plan["tailtot"])
    est = 2 * tm * D_MODEL * 2 + na * na * 2 + 2 * loc * D_MODEL * 4 + loc * tm * 4 + 4 * LANES * na * 4
    grid_spec = pltpu.PrefetchScalarGridSpec(
        num_scalar_prefetch=len(pre),
        grid=(n // tm,),
        in_specs=[pl.BlockSpec((tm, D_MODEL), lambda t, *_: (t, 0)), pl.BlockSpec((tm, LANES), lambda t, *_: (t, 0)),
                  pl.BlockSpec((None, SUBLANES, LANES), lambda t, *_: (t, 0, 0))],
        out_specs=pl.BlockSpec(memory_space=pl.ANY),
        scratch_shapes=[pltpu.VMEM((na, na), BF16), pltpu.VMEM((2, loc, D_MODEL), F32),
                        pltpu.VMEM((MOE_CHUNK, D_MODEL), F32), pltpu.SemaphoreType.DMA((2,))],
    )
    return pl.pallas_call(
        functools.partial(_dispatch_kernel, tm=tm),
        out_shape=jax.ShapeDtypeStruct((p_rows, D_MODEL), F32),
        grid_spec=grid_spec,
        compiler_params=_params(("arbitrary",), est),
        name="moe_dispatch",
    )(*pre, hn2d, eid2d, lrow)


def _expert_kernel(blk_e_ref, blk_src_ref, blk_n_ref, xs_ref, wgu_ref, wd_ref, ys_ref, wgu_bf, wd_bf):
    i = pl.program_id(0)

    @pl.when(blk_n_ref[i] > 0)
    def _():
        @pl.when((i == 0) | (blk_e_ref[i] != blk_e_ref[jnp.maximum(i - 1, 0)]))
        def _():
            wgu_bf[...] = wgu_ref[0].astype(BF16)
            wd_bf[...] = wd_ref[0].astype(BF16)

        gu = jnp.dot(xs_ref[...].astype(BF16), wgu_bf[...], preferred_element_type=F32)
        act = (jax.nn.silu(gu[:, :EXPERT_FF]) * gu[:, EXPERT_FF:]).astype(BF16)
        ys_ref[...] = jnp.dot(act, wd_bf[...], preferred_element_type=F32)


def _experts(plan, xs, wgu, wd, eb):
    n_blk = xs.shape[0] // eb
    rows = pl.BlockSpec((eb, D_MODEL), lambda i, be, bs, bn: (bs[i], 0))
    w_elems = wgu.shape[1] * wgu.shape[2] + wd.shape[1] * wd.shape[2]
    est = 2 * 2 * eb * D_MODEL * 4 + 2 * w_elems * 4 + w_elems * 2 + 4 * eb * D_MODEL * 4
    grid_spec = pltpu.PrefetchScalarGridSpec(
        num_scalar_prefetch=3,
        grid=(n_blk,),
        in_specs=[rows, pl.BlockSpec((1,) + wgu.shape[1:], lambda i, be, bs, bn: (be[i], 0, 0)),
                  pl.BlockSpec((1,) + wd.shape[1:], lambda i, be, bs, bn: (be[i], 0, 0))],
        out_specs=rows,
        scratch_shapes=[pltpu.VMEM(wgu.shape[1:], BF16), pltpu.VMEM(wd.shape[1:], BF16)],
    )
    return pl.pallas_call(
        _expert_kernel,
        out_shape=jax.ShapeDtypeStruct(xs.shape, F32),
        grid_spec=grid_spec,
        compiler_params=_params(("arbitrary",), est),
        name="experts",
    )(plan["blk_e"], plan["blk_src"], plan["blk_n"], xs, wgu, wd)


def _combine_kernel(gofs_ref, lst_ref, nch_ref, ntot_ref, x2_ref, ew_ref, eid_ref, lrow_ref, ys_hbm, o_ref,
                    before_ref, yloc_ref, sem, *, tm):
    t = pl.program_id(0)
    last = pl.num_programs(0) - 1
    slot = t % 2
    na = TOP_K * tm
    loc = yloc_ref.shape[1]

    def copy_in(local_row, global_row, s, rows=MOE_CHUNK):
        return pltpu.make_async_copy(ys_hbm.at[pl.ds(global_row, rows), :],
                                     yloc_ref.at[s, pl.ds(local_row, rows), :], sem.at[s])

    def fetch(tile, s):
        _segment_loop(tile, lst_ref, gofs_ref, nch_ref, lambda lr, gr: copy_in(lr, gr, s).start())

    @pl.when(t == 0)
    def _():
        r = lax.broadcasted_iota(I32, (na, na), 0)
        c = lax.broadcasted_iota(I32, (na, na), 1)
        before_ref[...] = (c < r).astype(BF16)
        yloc_ref[...] = jnp.zeros_like(yloc_ref)
        fetch(0, 0)

    @pl.when(t < last)
    def _():
        fetch(t + 1, 1 - slot)

    _wait_chunks(ntot_ref[t], lambda rows: copy_in(0, 0, slot, rows).wait())

    eid = eid_ref[...]
    lane = lax.broadcasted_iota(I32, (tm, LANES), 1)
    hit = jnp.concatenate([lane == eid[:, 0:1], lane == eid[:, 1:2]], axis=0)
    rank = jnp.dot(before_ref[...], hit.astype(BF16), preferred_element_type=F32)
    start = lrow_ref[0:1, :] * MOE_CHUNK
    pos = jnp.sum(jnp.where(hit, rank + start, 0.0), axis=1, keepdims=True).astype(I32)
    col = lax.broadcasted_iota(I32, (tm, loc), 1)
    yb = yloc_ref[slot].astype(BF16)
    y0 = jnp.dot((col == pos[:tm]).astype(BF16), yb, preferred_element_type=F32)
    y1 = jnp.dot((col == pos[tm:]).astype(BF16), yb, preferred_element_type=F32)
    ew = ew_ref[...]
    o_ref[...] = x2_ref[...] + (ew[:, 0:1] * y0 + ew[:, 1:2] * y1)


def _combine(plan, x2, ew, eid2d, lrow, ys, tm):
    n = x2.shape[0]
    na = TOP_K * tm
    loc = _local_rows(tm)
    pre = (plan["gofs"], plan["lst"], plan["nch"], plan["ntot"])
    tok = pl.BlockSpec((tm, D_MODEL), lambda t, *_: (t, 0))
    small = pl.BlockSpec((tm, LANES), lambda t, *_: (t, 0))
    est = 2 * 2 * tm * D_MODEL * 4 + na * na * 2 + 2 * loc * D_MODEL * 4 + loc * D_MODEL * 2 + 2 * tm * loc * 4 \
        + 3 * tm * D_MODEL * 4
    grid_spec = pltpu.PrefetchScalarGridSpec(
        num_scalar_prefetch=len(pre),
        grid=(n // tm,),
        in_specs=[tok, small, small, pl.BlockSpec((None, SUBLANES, LANES), lambda t, *_: (t, 0, 0)),
                  pl.BlockSpec(memory_space=pl.ANY)],
        out_specs=tok,
        scratch_shapes=[pltpu.VMEM((na, na), BF16), pltpu.VMEM((2, loc, D_MODEL), F32),
                        pltpu.SemaphoreType.DMA((2,))],
    )
    return pl.pallas_call(
        functools.partial(_combine_kernel, tm=tm),
        out_shape=jax.ShapeDtypeStruct((n, D_MODEL), F32),
        grid_spec=grid_spec,
        compiler_params=_params(("arbitrary",), est),
        name="moe_combine",
    )(*pre, x2, ew, eid2d, lrow, ys)


def _rope_table(positions):
    inv_freq = jnp.exp(-math.log(ROPE_THETA) * jnp.arange(ROT_HALF, dtype=F32) / ROT_HALF)
    ang = positions.astype(F32).reshape(-1, 1) * inv_freq
    seg = jnp.concatenate([jnp.cos(ang), jnp.sin(ang), jnp.zeros((ang.shape[0], HEAD_DIM - ROT_DIMS), F32)], axis=1)
    return jnp.tile(seg, (1, LANES // HEAD_DIM))


def _plan_rows(n_tok, tm, eb):
    worst = n_tok * TOP_K + (n_tok // tm) * N_EXPERTS * (MOE_CHUNK - 1)
    return (worst + eb - 1) // eb * eb + N_EXPERTS * eb


def _moe_plan(hist, p_rows, eb):
    n_tiles = hist.shape[0]
    seg = (hist + MOE_CHUNK - 1) // MOE_CHUNK * MOE_CHUNK
    tot = jnp.sum(seg, axis=0)
    region = (tot + eb - 1) // eb * eb
    pend = jnp.cumsum(region)
    pstart = pend - region
    gofs = pstart[None, :] + jnp.cumsum(seg, axis=0) - seg
    lst = jnp.cumsum(seg, axis=1) - seg
    nch = seg // MOE_CHUNK
    n_blk = p_rows // eb
    blk_first = jnp.arange(n_blk, dtype=I32) * eb
    blk_e = jnp.minimum(jnp.sum(blk_first[:, None] >= pend[None, :], axis=1), N_EXPERTS - 1).astype(I32)
    blk_n = jnp.clip(pstart[blk_e] + tot[blk_e] - blk_first, 0, eb)
    blk_src = jnp.minimum(jnp.arange(n_blk, dtype=I32), jnp.maximum(pend[-1] // eb - 1, 0))
    tailn = (region - tot) // MOE_CHUNK
    lrow = jnp.pad((lst // MOE_CHUNK).astype(F32), ((0, 0), (0, LANES - N_EXPERTS)))
    flat = lambda a: a.reshape(-1).astype(I32)
    plan = dict(gofs=flat(gofs), lst=flat(lst), nch=flat(nch), ntot=flat(jnp.sum(nch, axis=1)),
                tail=flat(pstart + tot), tailn=flat(tailn), tailtot=flat(jnp.sum(tailn)),
                blk_e=blk_e, blk_src=flat(blk_src), blk_n=flat(blk_n))
    return plan, jnp.broadcast_to(lrow[:, None, :], (n_tiles, SUBLANES, LANES))


def _tile(n, pref):
    t = min(n, pref)
    assert n % t == 0, (n, pref)
    return t


def kernel(x, mem, positions, norm_mix, w_in, conv_w, conv_b, lru_wa, lru_ba, lru_wx, lru_bx, lru_lambda, w_lru_o, q_norm, k_norm, lambda_q1, lambda_k1, lambda_q2, lambda_k2, subln, w_attn_o, w_out, norm_cx, norm_mem, w_cq, w_ckv, cq_norm, ck_norm, w_co, norm_ffn, w_group, b_group, w_router, b_router, w_gate_up, w_down):
    bsz, s, d = x.shape
    assert d == D_MODEL and w_in.shape[-1] == N_PROJ * D_MODEL
    n = bsz * s
    depth = w_in.shape[0]
    tm = _tile(n, 512)
    ts = _tile(s, 512)
    tq = _tile(s, 512)
    assert tq % CHUNK == 0 and ts % SUBLANES == 0
    eb = _tile(n, 512)
    rope = _rope_table(positions)
    row = lambda v: v.reshape(1, -1).astype(F32)
    rep = LANES // HEAD_DIM

    for layer in range(depth):
        lambda_init = 0.8 - 0.6 * math.exp(-0.3 * layer)
        lin, lgate, q, k, vt, gl, ga = _in_proj(
            x.reshape(n, d), row(norm_mix[layer]), w_in[layer].astype(BF16),
            jnp.tile(row(q_norm[layer]), (1, rep)), jnp.tile(row(k_norm[layer]), (1, rep)), rope, ts, s)
        seq = lambda a: a.reshape(bsz, s, d)

        wax = jnp.concatenate([lru_wa[layer], lru_wx[layer]], axis=-1).astype(BF16)
        bax = jnp.stack([lru_ba[layer], lru_bx[layer]]).astype(F32)
        ml = _lru(seq(lin), seq(lgate), seq(gl), conv_w[layer].astype(F32), row(conv_b[layer]), wax, bax,
                  row(lru_lambda[layer]), w_lru_o[layer].astype(BF16), ts)

        lam = (jnp.exp(jnp.sum(lambda_q1[layer].astype(F32) * lambda_k1[layer].astype(F32)))
               - jnp.exp(jnp.sum(lambda_q2[layer].astype(F32) * lambda_k2[layer].astype(F32))) + lambda_init)
        o = _diff_attn(lam.reshape(1, 1), seq(q), seq(k), vt, subln[layer].astype(F32).reshape(V_DIM, 1),
                       lambda_init, tq)

        x1 = _mix_out(x.reshape(n, d), o.reshape(n, d), ga, ml.reshape(n, d), w_attn_o[layer].astype(BF16),
                      w_out[layer].astype(BF16), tm)

        kc, vc = _mem_kv(mem, row(norm_mem[layer]), w_ckv[layer].astype(BF16), row(ck_norm[layer]))
        w_r = jnp.concatenate([w_group[layer], w_router[layer],
                               jnp.zeros((d, LANES - N_GROUPS - N_EXPERTS), F32)], axis=1).astype(F32)
        b_r = jnp.concatenate([b_group[layer], b_router[layer],
                               jnp.zeros((LANES - N_GROUPS - N_EXPERTS,), F32)]).reshape(1, LANES).astype(F32)
        wrh, wrl = _split_bf16(w_r)
        x2, hn, eid, ew, hist = _cross_router(
            seq(x1), row(norm_cx[layer]), w_cq[layer].astype(BF16), row(cq_norm[layer]), kc, vc,
            w_co[layer].astype(BF16), row(norm_ffn[layer]), wrh, wrl, b_r, ts)

        p_rows = _plan_rows(n, ts, eb)
        plan, lrow = _moe_plan(hist[:, :, 0, :N_EXPERTS].reshape(n // ts, N_EXPERTS), p_rows, eb)
        eid2d = eid.reshape(n, LANES)
        xs = _dispatch(plan, hn.reshape(n, d), eid2d, lrow, p_rows, ts)
        ys = _experts(plan, xs, w_gate_up[layer], w_down[layer], eb)
        x = _combine(plan, x2.reshape(n, d), ew.reshape(n, LANES), eid2d, lrow, ys, ts).reshape(bsz, s, d)
    return x
```

```python
import functools
import math

import jax
import jax.numpy as jnp
from jax import lax
from jax.experimental import pallas as pl
from jax.experimental.pallas import tpu as pltpu

F32 = jnp.float32
BF16 = jnp.bfloat16
I32 = jnp.int32

D_MODEL = 1024
CHUNK = 64
LRU_BLOCKS = 8
LRU_BLOCK_WIDTH = D_MODEL // LRU_BLOCKS
CONV_WIDTH = 4
LRU_C = 8.0
ATTN_HEADS = 8
HEAD_DIM = 64
V_DIM = 2 * HEAD_DIM
ROPE_THETA = 500000.0
ROT_DIMS = HEAD_DIM // 4
ROT_HALF = ROT_DIMS // 2
MEM_HEADS = 4
MEM_HEAD_DIM = 128
MEM_WIDTH = MEM_HEADS * MEM_HEAD_DIM
N_GROUPS = 4
EXPERTS_PER_GROUP = 8
N_EXPERTS = N_GROUPS * EXPERTS_PER_GROUP
TOP_K = 2
EXPERT_FF = 512
N_PROJ = 7
EPS = 1e-6
NEG_INF = -1e30
LOG2_E = math.log2(math.e)
ATTN_COL_GROUP = 512
ATTN_HEADS_PER_STEP = 4

LANES = 128
SUBLANES = 8
V7X_VMEM_BYTES = 64 * 1024 * 1024
MIB = 1024 * 1024


def _vmem_limit(estimate_bytes):
    return int(min(max(estimate_bytes * 3 // 2, 16 * MIB), V7X_VMEM_BYTES - 8 * MIB))


def _params(semantics, vmem_estimate):
    return pltpu.CompilerParams(dimension_semantics=semantics, vmem_limit_bytes=_vmem_limit(vmem_estimate))


def _resident(shape, index_map):
    return pl.BlockSpec(shape, index_map, pipeline_mode=pl.Buffered(1))


def _rms(x, g):
    return x * lax.rsqrt(jnp.mean(x * x, axis=-1, keepdims=True) + EPS) * g


def _sigmoid(x):
    return 0.5 * jnp.tanh(0.5 * x) + 0.5


def _segment_ones():
    r = lax.broadcasted_iota(I32, (LANES, LANES), 0) // HEAD_DIM
    c = lax.broadcasted_iota(I32, (LANES, LANES), 1) // HEAD_DIM
    return (r == c).astype(BF16)


def _qk_post(p, gain, cos_t, sin_lo, sin_hi, seg, scale):
    cols = []
    for c in range(D_MODEL // LANES):
        pc = p[:, c * LANES:(c + 1) * LANES]
        ss = jnp.dot((pc * pc).astype(BF16), seg, preferred_element_type=F32)
        y = pc * lax.rsqrt(ss * (1.0 / HEAD_DIM) + EPS) * gain
        y = y * cos_t + pltpu.roll(y, LANES - ROT_HALF, 1) * sin_lo + pltpu.roll(y, ROT_HALF, 1) * sin_hi
        cols.append((y * scale).astype(BF16))
    return jnp.concatenate(cols, axis=1)


def _in_proj_kernel(x_ref, g_ref, w_ref, qg_ref, kg_ref, rope_ref, cw_ref, cb_ref, wax_ref, bax_ref, lam_ref, wo_ref,
                    ml_ref, q_ref, k_ref, vt_ref, ga_ref, xprev_ref, hprev_ref, *, per_seq):
    @pl.when(pl.program_id(0) % per_seq == 0)
    def _():
        xprev_ref[...] = jnp.zeros_like(xprev_ref)
        hprev_ref[...] = jnp.zeros_like(hprev_ref)

    h = _rms(x_ref[...], g_ref[...]).astype(BF16)

    def proj(j):
        return jnp.dot(h, w_ref[:, j * D_MODEL:(j + 1) * D_MODEL], preferred_element_type=F32)

    xc, pre_a, pre_x = _lru_conv_gates(proj(0), cw_ref, cb_ref, wax_ref, xprev_ref)
    seg = _segment_ones()
    tab = rope_ref[...]
    seg_lane = lax.broadcasted_iota(I32, tab.shape, 1) % HEAD_DIM
    first, second = seg_lane < ROT_HALF, (seg_lane >= ROT_HALF) & (seg_lane < ROT_DIMS)
    cos_t = jnp.where(first, tab, jnp.where(second, pltpu.roll(tab, ROT_HALF, 1), 1.0))
    sin_lo = jnp.where(first, -pltpu.roll(tab, LANES - ROT_HALF, 1), 0.0)
    sin_hi = jnp.where(second, tab, 0.0)
    q_ref[...] = _qk_post(proj(2), qg_ref[...], cos_t, sin_lo, sin_hi, seg, HEAD_DIM ** -0.5 * LOG2_E)
    k_ref[...] = _qk_post(proj(3), kg_ref[...], cos_t, sin_lo, sin_hi, seg, 1.0)
    hr = _lru_scan(xc, pre_a, pre_x, bax_ref, lam_ref, hprev_ref)
    gate, merge_gate = proj(1), proj(5)
    v = proj(4)
    for hd in range(ATTN_HEADS):
        vt_ref[hd * V_DIM:(hd + 1) * V_DIM, :] = v[:, hd * V_DIM:(hd + 1) * V_DIM].T.astype(BF16)
    ml_ref[...] = _lru_out(hr, gate, merge_gate, wo_ref)
    ga_ref[...] = proj(6).astype(BF16)


def _in_proj(x2d, g, w_in, qg, kg, rope, lru_params, tm, seq_len):
    n = x2d.shape[0]
    per_seq = seq_len // tm
    row = lambda i: (i, 0)
    tok = pl.BlockSpec((tm, D_MODEL), row)
    tok_out = jax.ShapeDtypeStruct((n, D_MODEL), BF16)
    vt_out = jax.ShapeDtypeStruct((n // seq_len, D_MODEL, seq_len), BF16)
    vt_spec = pl.BlockSpec((None, D_MODEL, tm), lambda i: (i // per_seq, 0, i % per_seq))
    whole = lambda a: _resident(a.shape, lambda i: (0,) * a.ndim)
    consts = (g, w_in, qg, kg)
    est = (sum(a.size * a.dtype.itemsize for a in consts + tuple(lru_params)) + 2 * tm * D_MODEL * 4
           + 2 * tm * LANES * 4 + 5 * 2 * tm * D_MODEL * 2 + 16 * tm * D_MODEL * 4)
    return pl.pallas_call(
        functools.partial(_in_proj_kernel, per_seq=per_seq),
        out_shape=[tok_out] * 3 + [vt_out, tok_out],
        grid=(n // tm,),
        in_specs=[tok] + [whole(a) for a in consts] + [pl.BlockSpec((tm, LANES), row)]
        + [whole(a) for a in lru_params],
        out_specs=[tok] * 3 + [vt_spec, tok],
        scratch_shapes=[pltpu.VMEM((SUBLANES, D_MODEL), F32), pltpu.VMEM((SUBLANES, D_MODEL), F32)],
        compiler_params=_params(("arbitrary",), est),
        name="in_proj_lru",
    )(x2d, *consts, rope, *lru_params)


def _lru_conv_gates(x, cw_ref, cb_ref, wax_ref, xprev_ref):
    tt = x.shape[0]
    xp = jnp.concatenate([xprev_ref[...], x], axis=0)
    cw = cw_ref[...]
    xc = cb_ref[...] + cw[3:4] * x
    for j in range(1, CONV_WIDTH):
        xc = xc + cw[CONV_WIDTH - 1 - j:CONV_WIDTH - j] * xp[SUBLANES - j:SUBLANES - j + tt]
    xprev_ref[...] = x[tt - SUBLANES:tt]

    xcb = xc.astype(BF16)
    ra, ri = [], []
    for n in range(LRU_BLOCKS):
        g = jnp.dot(xcb[:, n * LRU_BLOCK_WIDTH:(n + 1) * LRU_BLOCK_WIDTH], wax_ref[n], preferred_element_type=F32)
        ra.append(g[:, :LRU_BLOCK_WIDTH])
        ri.append(g[:, LRU_BLOCK_WIDTH:])
    return xc, jnp.concatenate(ra, axis=1), jnp.concatenate(ri, axis=1)


def _lru_scan(xc, pre_a, pre_x, bax_ref, lam_ref, hprev_ref):
    nblk = xc.shape[0] // SUBLANES
    bax = bax_ref[...]
    r = _sigmoid(pre_a + bax[0:1])
    i = _sigmoid(pre_x + bax[1:2])
    log_a = -LRU_C * r * jax.nn.softplus(-lam_ref[...])
    a = jnp.exp(log_a)
    gain2 = -jnp.tanh(log_a) * (a * a + 1.0)
    b = jnp.where(gain2 > 0.0, gain2 * lax.rsqrt(gain2), 0.0) * i * xc

    a3 = a.reshape(nblk, SUBLANES, D_MODEL)
    b3 = b.reshape(nblk, SUBLANES, D_MODEL)
    sub = lax.broadcasted_iota(I32, (nblk, SUBLANES, D_MODEL), 1)
    shift = 1
    while shift < SUBLANES:
        keep = sub >= shift
        a_sh = pltpu.roll(a3, shift, 1)
        b_sh = pltpu.roll(b3, shift, 1)
        b3 = jnp.where(keep, a3 * b_sh + b3, b3)
        a3 = jnp.where(keep, a3 * a_sh, a3)
        shift *= 2
    h_last = hprev_ref[...]
    groups = []
    for blk in range(nblk):
        hb = a3[blk] * h_last + b3[blk]
        groups.append(hb)
        h_last = jnp.broadcast_to(hb[SUBLANES - 1:SUBLANES], (SUBLANES, D_MODEL))
    hprev_ref[...] = h_last
    return jnp.concatenate(groups, axis=0)


def _lru_out(hr, gate, merge_gate, wo_ref):
    y = (jax.nn.gelu(gate) * hr).astype(BF16)
    yl = jnp.dot(y, wo_ref[...], preferred_element_type=F32)
    return (_sigmoid(merge_gate) * yl).astype(BF16)


def _attn_kernel(lam_ref, q_ref, qnext_ref, k_ref, vt_ref, sub_ref, o_ref, m_ref, l_ref, acc_ref, qz_ref, sa_ref,
                 sb_ref, *, tq, out_scale):
    i = pl.program_id(2)
    last = pl.num_programs(2) - 1
    heads = range(ATTN_HEADS_PER_STEP)
    m_ref[...] = jnp.full_like(m_ref, NEG_INF)
    l_ref[...] = jnp.zeros_like(l_ref)
    acc_ref[...] = jnp.zeros_like(acc_ref)
    cw = ATTN_COL_GROUP

    def load_queries(src_ref):
        row = lax.broadcasted_iota(I32, (V_DIM, tq), 0)
        for h in heads:
            qt = src_ref[:, h * V_DIM:(h + 1) * V_DIM].astype(F32).T
            zero = jnp.zeros_like(qt)
            qz_ref[h] = jnp.concatenate([jnp.where(row < HEAD_DIM, qt, zero), jnp.where(row >= HEAD_DIM, qt, zero)],
                                        axis=1).astype(BF16)

    def block_off(j):
        return pl.multiple_of(j * tq, tq)

    def scores(j, s_ref):
        for h in heads:
            kb = k_ref[pl.ds(block_off(j), tq), h * V_DIM:(h + 1) * V_DIM]
            s_ref[h] = jnp.dot(kb, qz_ref[h], preferred_element_type=F32)

    def softmax_pv(j, s_ref, diagonal):
        for h in heads:
            vtb = vt_ref[h * V_DIM:(h + 1) * V_DIM, pl.ds(block_off(j), tq)]
            for g in range(2 * tq // cw):
                cols = pl.ds(g * cw, cw)
                s = s_ref[h, :, cols]
                if diagonal:
                    key = lax.broadcasted_iota(I32, (tq, cw), 0)
                    qry = (lax.broadcasted_iota(I32, (tq, cw), 1) + g * cw) % tq
                    s = jnp.where((key // CHUNK) <= (qry // CHUNK), s, NEG_INF)
                m_prev = m_ref[h, :, cols]
                m_new = jnp.maximum(m_prev, jnp.max(s, axis=0, keepdims=True))
                alpha = jnp.exp2(m_prev - m_new)
                p = jnp.exp2(s - m_new)
                l_ref[h, :, cols] = alpha * l_ref[h, :, cols] + jnp.sum(p, axis=0, keepdims=True)
                acc_ref[h, :, cols] = alpha * acc_ref[h, :, cols] + jnp.dot(vtb, p.astype(BF16),
                                                                            preferred_element_type=F32)
                m_ref[h, :, cols] = m_new

    @pl.when(i == 0)
    def _():
        load_queries(q_ref)
        scores(0, sa_ref)

    def pair(p, carry):
        j = 2 * p
        scores(j + 1, sb_ref)
        softmax_pv(j, sa_ref, False)
        scores(j + 2, sa_ref)
        softmax_pv(j + 1, sb_ref, False)
        return carry

    lax.fori_loop(0, i // 2, pair, 0)

    @pl.when(i % 2 == 1)
    def _():
        scores(i, sb_ref)
        softmax_pv(i - 1, sa_ref, False)
        softmax_pv(i, sb_ref, True)

    @pl.when(i % 2 == 0)
    def _():
        softmax_pv(i, sa_ref, True)

    def write_output():
        for h in heads:
            o12 = acc_ref[h] * (1.0 / l_ref[h])
            ot = o12[:, :tq] - lam_ref[0, 0] * o12[:, tq:]
            ot = ot * lax.rsqrt(jnp.mean(ot * ot, axis=0, keepdims=True) + EPS) * sub_ref[...] * out_scale
            o_ref[:, h * V_DIM:(h + 1) * V_DIM] = ot.T.astype(BF16)

    @pl.when(i < last)
    def _():
        load_queries(qnext_ref)
        scores(0, sa_ref)
        write_output()

    @pl.when(i == last)
    def _():
        write_output()


def _diff_attn(lam, q, k, vt, sub, lambda_init, tq):
    bsz, s, _ = q.shape
    hps = ATTN_HEADS_PER_STEP
    width = hps * V_DIM
    nq = s // tq
    qspec = pl.BlockSpec((None, tq, width), lambda b, h, i: (b, i, h))
    qnext_spec = pl.BlockSpec((None, tq, width), lambda b, h, i: (b, jnp.minimum(i + 1, nq - 1), h))
    kspec = pl.BlockSpec((None, s, width), lambda b, h, i: (b, 0, h))
    vtspec = pl.BlockSpec((None, width, s), lambda b, h, i: (b, h, 0))
    est = 2 * 2 * s * width * 2 + 4 * tq * width * 2 + hps * (V_DIM * 2 * tq * 6 + 2 * tq * 2 * tq * 4) \
        + 3 * 2 * tq * tq * 4
    return pl.pallas_call(
        functools.partial(_attn_kernel, tq=tq, out_scale=1.0 - lambda_init),
        out_shape=jax.ShapeDtypeStruct((bsz, s, ATTN_HEADS * V_DIM), BF16),
        grid=(bsz, ATTN_HEADS // hps, nq),
        in_specs=[pl.BlockSpec(memory_space=pltpu.SMEM), qspec, qnext_spec, kspec, vtspec,
                  pl.BlockSpec((V_DIM, 1), lambda b, h, i: (0, 0))],
        out_specs=qspec,
        scratch_shapes=[pltpu.VMEM((hps, 1, 2 * tq), F32), pltpu.VMEM((hps, 1, 2 * tq), F32),
                        pltpu.VMEM((hps, V_DIM, 2 * tq), F32), pltpu.VMEM((hps, V_DIM, 2 * tq), BF16),
                        pltpu.VMEM((hps, tq, 2 * tq), F32), pltpu.VMEM((hps, tq, 2 * tq), F32)],
        compiler_params=_params(("parallel", "parallel", "arbitrary"), est),
        name="diff_attn",
    )(lam, q, q, k, vt, sub)


def _mix_out_kernel(x_ref, o_ref, ga_ref, ml_ref, wao_ref, wout_ref, x1_ref):
    ya = jnp.dot(o_ref[...], wao_ref[...], preferred_element_type=F32)
    mixed = ml_ref[...].astype(F32) + _sigmoid(ga_ref[...].astype(F32)) * ya
    x1_ref[...] = x_ref[...] + jnp.dot(mixed.astype(BF16), wout_ref[...], preferred_element_type=F32)


def _mix_out(x2d, o, ga, ml, wao, wout, tm):
    n = x2d.shape[0]
    tok = pl.BlockSpec((tm, D_MODEL), lambda i: (i, 0))
    fixed = lambda i: (0, 0)
    est = 2 * 2 * tm * D_MODEL * 4 + 3 * 2 * tm * D_MODEL * 2 + 2 * D_MODEL * D_MODEL * 2 + 3 * tm * D_MODEL * 4
    return pl.pallas_call(
        _mix_out_kernel,
        out_shape=jax.ShapeDtypeStruct((n, D_MODEL), F32),
        grid=(n // tm,),
        in_specs=[tok, tok, tok, tok, _resident(wao.shape, fixed), _resident(wout.shape, fixed)],
        out_specs=tok,
        compiler_params=_params(("parallel",), est),
        name="mix_out",
    )(x2d, o, ga, ml, wao, wout)


def _mem_kv_kernel(mem_ref, g_ref, w_ref, ckg_ref, k_ref, v_ref):
    h = _rms(mem_ref[...], g_ref[...]).astype(BF16)
    kv = jnp.dot(h, w_ref[...], preferred_element_type=F32)
    ks = [_rms(kv[:, hd * MEM_HEAD_DIM:(hd + 1) * MEM_HEAD_DIM], ckg_ref[...]) for hd in range(MEM_HEADS)]
    k_ref[...] = jnp.concatenate(ks, axis=1).astype(BF16)
    v_ref[...] = kv[:, MEM_WIDTH:].astype(BF16)


def _mem_kv(mem, g, w, ckg):
    bsz, m, _ = mem.shape
    fixed = lambda b: (0, 0)
    out = pl.BlockSpec((None, m, MEM_WIDTH), lambda b: (b, 0, 0))
    est = 2 * m * D_MODEL * 4 + w.size * 2 + 4 * m * MEM_WIDTH * 2 + 4 * m * D_MODEL * 4
    return pl.pallas_call(
        _mem_kv_kernel,
        out_shape=[jax.ShapeDtypeStruct((bsz, m, MEM_WIDTH), BF16)] * 2,
        grid=(bsz,),
        in_specs=[pl.BlockSpec((None, m, D_MODEL), lambda b: (b, 0, 0)), _resident((1, D_MODEL), fixed),
                  _resident(w.shape, fixed), _resident((1, MEM_HEAD_DIM), fixed)],
        out_specs=[out, out],
        compiler_params=_params(("parallel",), est),
        name="mem_kv",
    )(mem, g, w, ckg)


def _split_bf16(x):
    hi = x.astype(BF16)
    return hi, (x - hi.astype(F32)).astype(BF16)


def _cross_router_kernel(x1_ref, gcx_ref, wcq_ref, cqg_ref, kc_ref, vc_ref, wco_ref, gffn_ref, wrh_ref, wrl_ref,
                         br_ref, x2_ref, hn_ref, eid_ref, ew_ref, hist_ref):
    x1 = x1_ref[...]
    q = jnp.dot(_rms(x1, gcx_ref[...]).astype(BF16), wcq_ref[...], preferred_element_type=F32)
    outs = []
    for hd in range(MEM_HEADS):
        sl = slice(hd * MEM_HEAD_DIM, (hd + 1) * MEM_HEAD_DIM)
        qh = _rms(q[:, sl], cqg_ref[...]) * MEM_HEAD_DIM ** -0.5
        s = lax.dot_general(qh.astype(BF16), kc_ref[:, sl], (((1,), (1,)), ((), ())), preferred_element_type=F32)
        p = jnp.exp(s - jnp.max(s, axis=1, keepdims=True))
        o = jnp.dot(p.astype(BF16), vc_ref[:, sl], preferred_element_type=F32)
        outs.append(o / jnp.sum(p, axis=1, keepdims=True))
    x2 = x1 + jnp.dot(jnp.concatenate(outs, axis=1).astype(BF16), wco_ref[...], preferred_element_type=F32)
    x2_ref[...] = x2

    hn = _rms(x2, gffn_ref[...])
    hn_ref[...] = hn.astype(BF16)
    h_hi, h_lo = _split_bf16(hn)
    logits = (jnp.dot(h_hi, wrh_ref[...], preferred_element_type=F32)
              + jnp.dot(h_lo, wrh_ref[...], preferred_element_type=F32)
              + jnp.dot(h_hi, wrl_ref[...], preferred_element_type=F32)) + br_ref[...]
    lane = lax.broadcasted_iota(I32, logits.shape, 1)
    is_group = lane < N_GROUPS
    gl = jnp.where(is_group, logits, NEG_INF)
    gmax = jnp.max(gl, axis=1, keepdims=True)
    gval = 1.0 / jnp.sum(jnp.where(is_group, jnp.exp(gl - gmax), 0.0), axis=1, keepdims=True)
    gidx = jnp.min(jnp.where(gl == gmax, lane, LANES), axis=1, keepdims=True)
    lane_group = lax.shift_right_logical(lane + (EXPERTS_PER_GROUP - N_GROUPS), 3) - 1
    chosen = lane_group == gidx
    el = jnp.where(chosen, logits, NEG_INF)
    v1 = jnp.max(el, axis=1, keepdims=True)
    i1 = jnp.min(jnp.where(chosen & (el == v1), lane, LANES), axis=1, keepdims=True)
    rest = chosen & (lane != i1)
    el2 = jnp.where(rest, logits, NEG_INF)
    v2 = jnp.max(el2, axis=1, keepdims=True)
    i2 = jnp.min(jnp.where(rest & (el2 == v2), lane, LANES), axis=1, keepdims=True)
    t = jnp.exp(v2 - v1)
    w1 = gval / (1.0 + t)
    w2 = gval * t / (1.0 + t)
    e1, e2 = i1 - N_GROUPS, i2 - N_GROUPS
    eid_ref[...] = jnp.where(lane == 0, e1, jnp.where(lane == 1, e2, 0))
    ew_ref[...] = jnp.where(lane == 0, w1, jnp.where(lane == 1, w2, 0.0))
    count = jnp.sum((lane == e1).astype(I32) + (lane == e2).astype(I32), axis=0, keepdims=True)
    hist_ref[...] = jnp.broadcast_to(count, hist_ref.shape)


def _cross_router(x1, gcx, wcq, cqg, kc, vc, wco, gffn, wrh, wrl, br, tm):
    bsz, s, _ = x1.shape
    m = kc.shape[1]
    tok = pl.BlockSpec((None, tm, D_MODEL), lambda b, i: (b, i, 0))
    small = pl.BlockSpec((None, tm, LANES), lambda b, i: (b, i, 0))
    memb = pl.BlockSpec((None, m, MEM_WIDTH), lambda b, i: (b, 0, 0))
    fixed = lambda b, i: (0, 0)
    est = (3 * 2 * tm * D_MODEL * 4 + 2 * 2 * tm * LANES * 4 + 2 * 2 * m * MEM_WIDTH * 2 + 2 * D_MODEL * MEM_WIDTH * 2
           + 2 * D_MODEL * LANES * 2 + 6 * tm * D_MODEL * 4)
    return pl.pallas_call(
        _cross_router_kernel,
        out_shape=[jax.ShapeDtypeStruct((bsz, s, D_MODEL), F32), jax.ShapeDtypeStruct((bsz, s, D_MODEL), BF16),
                   jax.ShapeDtypeStruct((bsz, s, LANES), I32), jax.ShapeDtypeStruct((bsz, s, LANES), F32),
                   jax.ShapeDtypeStruct((bsz, s // tm, SUBLANES, LANES), I32)],
        grid=(bsz, s // tm),
        in_specs=[tok, _resident((1, D_MODEL), fixed), _resident(wcq.shape, fixed), _resident((1, MEM_HEAD_DIM), fixed),
                  memb, memb, _resident(wco.shape, fixed), _resident((1, D_MODEL), fixed),
                  _resident(wrh.shape, fixed), _resident(wrl.shape, fixed), _resident((1, LANES), fixed)],
        out_specs=[tok, tok, small, small,
                   pl.BlockSpec((None, None, SUBLANES, LANES), lambda b, i: (b, i, 0, 0))],
        compiler_params=_params(("parallel", "parallel"), est),
        name="cross_router",
    )(x1, gcx, wcq, cqg, kc, vc, wco, gffn, wrh, wrl, br)


MOE_CHUNK = SUBLANES


def _local_rows(tm):
    return TOP_K * tm + N_EXPERTS * MOE_CHUNK


def _segment_loop(t, lst_ref, gofs_ref, nch_ref, fn):
    def per_expert(e, carry):
        k = t * N_EXPERTS + e
        lst, gofs = lst_ref[k], gofs_ref[k]

        def per_chunk(c, cc):
            fn(pl.multiple_of(lst + c * MOE_CHUNK, MOE_CHUNK), pl.multiple_of(gofs + c * MOE_CHUNK, MOE_CHUNK))
            return cc
        lax.fori_loop(0, nch_ref[k], per_chunk, 0)
        return carry
    lax.fori_loop(0, N_EXPERTS, per_expert, 0)


def _repeat(count, fn):
    def body(c, carry):
        fn()
        return carry
    lax.fori_loop(0, count, body, 0)


MOE_WAIT_GROUP = 8


def _wait_chunks(count, wait_rows):
    _repeat(lax.shift_right_logical(count, 3), lambda: wait_rows(MOE_WAIT_GROUP * MOE_CHUNK))
    _repeat(count & (MOE_WAIT_GROUP - 1), lambda: wait_rows(MOE_CHUNK))


def _dispatch_kernel(gofs_ref, lst_ref, nch_ref, ntot_ref, tail_ref, tailn_ref, tailtot_ref,
                     hn_ref, eid_ref, lrow_ref, xs_hbm, before_ref, xloc_ref, zero_ref, sem, *, tm):
    t = pl.program_id(0)
    last = pl.num_programs(0) - 1
    slot = t % 2
    na = TOP_K * tm
    loc = xloc_ref.shape[1]

    @pl.when(t == 0)
    def _():
        r = lax.broadcasted_iota(I32, (na, na), 0)
        c = lax.broadcasted_iota(I32, (na, na), 1)
        before_ref[...] = (r < c).astype(BF16)
        zero_ref[...] = jnp.zeros_like(zero_ref)

    et = eid_ref[...].astype(F32).T
    e_row = jnp.concatenate([et[0:1], et[1:2]], axis=1)
    hit = lax.broadcasted_iota(I32, (LANES, na), 0).astype(F32) == e_row
    hit_b = hit.astype(BF16)
    rank = jnp.dot(hit_b, before_ref[...], preferred_element_type=F32)
    start = jnp.dot(lrow_ref[...].astype(BF16), hit_b, preferred_element_type=F32)[0:1] * MOE_CHUNK
    pos = (start + jnp.sum(jnp.where(hit, rank, 0.0), axis=0, keepdims=True)).astype(I32)
    r = lax.broadcasted_iota(I32, (loc, tm), 0)
    sel = ((r == pos[:, :tm]) | (r == pos[:, tm:])).astype(BF16)
    xloc_ref[slot] = jnp.dot(sel, hn_ref[...], preferred_element_type=F32)

    def copy_out(local_row, global_row, s, rows=MOE_CHUNK):
        return pltpu.make_async_copy(xloc_ref.at[s, pl.ds(local_row, rows), :],
                                     xs_hbm.at[pl.ds(global_row, rows), :], sem.at[s])

    _segment_loop(t, lst_ref, gofs_ref, nch_ref, lambda lr, gr: copy_out(lr, gr, slot).start())

    @pl.when(t > 0)
    def _():
        _wait_chunks(ntot_ref[jnp.maximum(t - 1, 0)], lambda rows: copy_out(0, 0, 1 - slot, rows).wait())

    @pl.when(t == last)
    def _():
        _wait_chunks(ntot_ref[t], lambda rows: copy_out(0, 0, slot, rows).wait())

        def zero_out(global_row):
            return pltpu.make_async_copy(zero_ref, xs_hbm.at[pl.ds(global_row, MOE_CHUNK), :], sem.at[0])

        def per_expert(e, carry):
            def per_chunk(c, cc):
                zero_out(pl.multiple_of(tail_ref[e] + c * MOE_CHUNK, MOE_CHUNK)).start()
                return cc
            lax.fori_loop(0, tailn_ref[e], per_chunk, 0)
            return carry
        lax.fori_loop(0, N_EXPERTS, per_expert, 0)
        _repeat(tailtot_ref[0], lambda: zero_out(0).wait())


def _dispatch(plan, hn2d, eid2d, lrow, p_rows, tm):
    n = hn2d.shape[0]
    na = TOP_K * tm
    loc = _local_rows(tm)
    pre = (plan["gofs"], plan["lst"], plan["nch"], plan["ntot"], plan["tail"], plan["tailn"], plan["tailtot"])
    est = 2 * tm * D_MODEL * 2 + na * na * 2 + 2 * loc * D_MODEL * 4 + loc * tm * 4 + 4 * LANES * na * 4
    grid_spec = pltpu.PrefetchScalarGridSpec(
        num_scalar_prefetch=len(pre),
        grid=(n // tm,),
        in_specs=[pl.BlockSpec((tm, D_MODEL), lambda t, *_: (t, 0)), pl.BlockSpec((tm, LANES), lambda t, *_: (t, 0)),
                  pl.BlockSpec((None, SUBLANES, LANES), lambda t, *_: (t, 0, 0))],
        out_specs=pl.BlockSpec(memory_space=pl.ANY),
        scratch_shapes=[pltpu.VMEM((na, na), BF16), pltpu.VMEM((2, loc, D_MODEL), F32),
                        pltpu.VMEM((MOE_CHUNK, D_MODEL), F32), pltpu.SemaphoreType.DMA((2,))],
    )
    return pl.pallas_call(
        functools.partial(_dispatch_kernel, tm=tm),
        out_shape=jax.ShapeDtypeStruct((p_rows, D_MODEL), F32),
        grid_spec=grid_spec,
        compiler_params=_params(("arbitrary",), est),
        name="moe_dispatch",
    )(*pre, hn2d, eid2d, lrow)


def _expert_kernel(blk_e_ref, blk_src_ref, blk_n_ref, xs_ref, wgu_ref, wd_ref, ys_ref, wgu_bf, wd_bf):
    i = pl.program_id(0)

    @pl.when(blk_n_ref[i] > 0)
    def _():
        @pl.when((i == 0) | (blk_e_ref[i] != blk_e_ref[jnp.maximum(i - 1, 0)]))
        def _():
            wgu_bf[...] = wgu_ref[0].astype(BF16)
            wd_bf[...] = wd_ref[0].astype(BF16)

        gu = jnp.dot(xs_ref[...].astype(BF16), wgu_bf[...], preferred_element_type=F32)
        act = (jax.nn.silu(gu[:, :EXPERT_FF]) * gu[:, EXPERT_FF:]).astype(BF16)
        ys_ref[...] = jnp.dot(act, wd_bf[...], preferred_element_type=F32)


def _experts(plan, xs, wgu, wd, eb):
    n_blk = xs.shape[0] // eb
    rows = pl.BlockSpec((eb, D_MODEL), lambda i, be, bs, bn: (bs[i], 0))
    w_elems = wgu.shape[1] * wgu.shape[2] + wd.shape[1] * wd.shape[2]
    est = 2 * 2 * eb * D_MODEL * 4 + 2 * w_elems * 4 + w_elems * 2 + 4 * eb * D_MODEL * 4
    grid_spec = pltpu.PrefetchScalarGridSpec(
        num_scalar_prefetch=3,
        grid=(n_blk,),
        in_specs=[rows, pl.BlockSpec((1,) + wgu.shape[1:], lambda i, be, bs, bn: (be[i], 0, 0)),
                  pl.BlockSpec((1,) + wd.shape[1:], lambda i, be, bs, bn: (be[i], 0, 0))],
        out_specs=rows,
        scratch_shapes=[pltpu.VMEM(wgu.shape[1:], BF16), pltpu.VMEM(wd.shape[1:], BF16)],
    )
    return pl.pallas_call(
        _expert_kernel,
        out_shape=jax.ShapeDtypeStruct(xs.shape, F32),
        grid_spec=grid_spec,
        compiler_params=_params(("arbitrary",), est),
        name="experts",
    )(plan["blk_e"], plan["blk_src"], plan["blk_n"], xs, wgu, wd)


def _combine_kernel(gofs_ref, lst_ref, nch_ref, ntot_ref, x2_ref, ew_ref, eid_ref, lrow_ref, ys_hbm, o_ref,
                    before_ref, yloc_ref, sem, *, tm):
    t = pl.program_id(0)
    last = pl.num_programs(0) - 1
    slot = t % 2
    na = TOP_K * tm
    loc = yloc_ref.shape[1]

    def copy_in(local_row, global_row, s, rows=MOE_CHUNK):
        return pltpu.make_async_copy(ys_hbm.at[pl.ds(global_row, rows), :],
                                     yloc_ref.at[s, pl.ds(local_row, rows), :], sem.at[s])

    def fetch(tile, s):
        _segment_loop(tile, lst_ref, gofs_ref, nch_ref, lambda lr, gr: copy_in(lr, gr, s).start())

    @pl.when(t == 0)
    def _():
        r = lax.broadcasted_iota(I32, (na, na), 0)
        c = lax.broadcasted_iota(I32, (na, na), 1)
        before_ref[...] = (c < r).astype(BF16)
        yloc_ref[...] = jnp.zeros_like(yloc_ref)
        fetch(0, 0)

    @pl.when(t < last)
    def _():
        fetch(t + 1, 1 - slot)

    _wait_chunks(ntot_ref[t], lambda rows: copy_in(0, 0, slot, rows).wait())

    eid = eid_ref[...]
    lane = lax.broadcasted_iota(I32, (tm, LANES), 1)
    hit = jnp.concatenate([lane == eid[:, 0:1], lane == eid[:, 1:2]], axis=0)
    rank = jnp.dot(before_ref[...], hit.astype(BF16), preferred_element_type=F32)
    start = lrow_ref[0:1, :] * MOE_CHUNK
    pos = jnp.sum(jnp.where(hit, rank + start, 0.0), axis=1, keepdims=True).astype(I32)
    col = lax.broadcasted_iota(I32, (tm, loc), 1)
    yb = yloc_ref[slot].astype(BF16)
    y0 = jnp.dot((col == pos[:tm]).astype(BF16), yb, preferred_element_type=F32)
    y1 = jnp.dot((col == pos[tm:]).astype(BF16), yb, preferred_element_type=F32)
    ew = ew_ref[...]
    o_ref[...] = x2_ref[...] + (ew[:, 0:1] * y0 + ew[:, 1:2] * y1)


def _combine(plan, x2, ew, eid2d, lrow, ys, tm):
    n = x2.shape[0]
    na = TOP_K * tm
    loc = _local_rows(tm)
    pre = (plan["gofs"], plan["lst"], plan["nch"], plan["ntot"])
    tok = pl.BlockSpec((tm, D_MODEL), lambda t, *_: (t, 0))
    small = pl.BlockSpec((tm, LANES), lambda t, *_: (t, 0))
    est = 2 * 2 * tm * D_MODEL * 4 + na * na * 2 + 2 * loc * D_MODEL * 4 + loc * D_MODEL * 2 + 2 * tm * loc * 4 \
        + 3 * tm * D_MODEL * 4
    grid_spec = pltpu.PrefetchScalarGridSpec(
        num_scalar_prefetch=len(pre),
        grid=(n // tm,),
        in_specs=[tok, small, small, pl.BlockSpec((None, SUBLANES, LANES), lambda t, *_: (t, 0, 0)),
                  pl.BlockSpec(memory_space=pl.ANY)],
        out_specs=tok,
        scratch_shapes=[pltpu.VMEM((na, na), BF16), pltpu.VMEM((2, loc, D_MODEL), F32),
                        pltpu.SemaphoreType.DMA((2,))],
    )
    return pl.pallas_call(
        functools.partial(_combine_kernel, tm=tm),
        out_shape=jax.ShapeDtypeStruct((n, D_MODEL), F32),
        grid_spec=grid_spec,
        compiler_params=_params(("arbitrary",), est),
        name="moe_combine",
    )(*pre, x2, ew, eid2d, lrow, ys)


def _rope_table(positions):
    inv_freq = jnp.exp(-math.log(ROPE_THETA) * jnp.arange(ROT_HALF, dtype=F32) / ROT_HALF)
    ang = positions.astype(F32).reshape(-1, 1) * inv_freq
    seg = jnp.concatenate([jnp.cos(ang), jnp.sin(ang), jnp.zeros((ang.shape[0], HEAD_DIM - ROT_DIMS), F32)], axis=1)
    return jnp.tile(seg, (1, LANES // HEAD_DIM))


def _plan_rows(n_tok, tm, eb):
    worst = n_tok * TOP_K + (n_tok // tm) * N_EXPERTS * (MOE_CHUNK - 1)
    return (worst + eb - 1) // eb * eb + N_EXPERTS * eb


def _moe_plan(hist, p_rows, eb):
    n_tiles = hist.shape[0]
    seg = (hist + MOE_CHUNK - 1) // MOE_CHUNK * MOE_CHUNK
    tot = jnp.sum(seg, axis=0)
    region = (tot + eb - 1) // eb * eb
    pend = jnp.cumsum(region)
    pstart = pend - region
    gofs = pstart[None, :] + jnp.cumsum(seg, axis=0) - seg
    lst = jnp.cumsum(seg, axis=1) - seg
    nch = seg // MOE_CHUNK
    n_blk = p_rows // eb
    blk_first = jnp.arange(n_blk, dtype=I32) * eb
    blk_e = jnp.minimum(jnp.sum(blk_first[:, None] >= pend[None, :], axis=1), N_EXPERTS - 1).astype(I32)
    blk_n = jnp.clip(pstart[blk_e] + tot[blk_e] - blk_first, 0, eb)
    blk_src = jnp.minimum(jnp.arange(n_blk, dtype=I32), jnp.maximum(pend[-1] // eb - 1, 0))
    tailn = (region - tot) // MOE_CHUNK
    lrow = jnp.pad((lst // MOE_CHUNK).astype(F32), ((0, 0), (0, LANES - N_EXPERTS)))
    flat = lambda a: a.reshape(-1).astype(I32)
    plan = dict(gofs=flat(gofs), lst=flat(lst), nch=flat(nch), ntot=flat(jnp.sum(nch, axis=1)),
                tail=flat(pstart + tot), tailn=flat(tailn), tailtot=flat(jnp.sum(tailn)),
                blk_e=blk_e, blk_src=flat(blk_src), blk_n=flat(blk_n))
    return plan, jnp.broadcast_to(lrow[:, None, :], (n_tiles, SUBLANES, LANES))


def _tile(n, pref):
    t = min(n, pref)
    assert n % t == 0, (n, pref)
    return t


def kernel(x, mem, positions, norm_mix, w_in, conv_w, conv_b, lru_wa, lru_ba, lru_wx, lru_bx, lru_lambda, w_lru_o, q_norm, k_norm, lambda_q1, lambda_k1, lambda_q2, lambda_k2, subln, w_attn_o, w_out, norm_cx, norm_mem, w_cq, w_ckv, cq_norm, ck_norm, w_co, norm_ffn, w_group, b_group, w_router, b_router, w_gate_up, w_down):
    bsz, s, d = x.shape
    assert d == D_MODEL and w_in.shape[-1] == N_PROJ * D_MODEL
    n = bsz * s
    depth = w_in.shape[0]
    tm = _tile(n, 512)
    ts = _tile(s, 512)
    tq = _tile(s, 512)
    assert tq % CHUNK == 0 and ts % SUBLANES == 0
    eb = _tile(n, 512)
    rope = _rope_table(positions)
    row = lambda v: v.reshape(1, -1).astype(F32)
    rep = LANES // HEAD_DIM

    for layer in range(depth):
        lambda_init = 0.8 - 0.6 * math.exp(-0.3 * layer)
        wax = jnp.concatenate([lru_wa[layer], lru_wx[layer]], axis=-1).astype(BF16)
        bax = jnp.stack([lru_ba[layer], lru_bx[layer]]).astype(F32)
        lru_params = (conv_w[layer].astype(F32), row(conv_b[layer]), wax, bax, row(lru_lambda[layer]),
                      w_lru_o[layer].astype(BF16))
        ml, q, k, vt, ga = _in_proj(
            x.reshape(n, d), row(norm_mix[layer]), w_in[layer].astype(BF16),
            jnp.tile(row(q_norm[layer]), (1, rep)), jnp.tile(row(k_norm[layer]), (1, rep)), rope, lru_params, ts, s)
        seq = lambda a: a.reshape(bsz, s, d)

        lam = (jnp.exp(jnp.sum(lambda_q1[layer].astype(F32) * lambda_k1[layer].astype(F32)))
               - jnp.exp(jnp.sum(lambda_q2[layer].astype(F32) * lambda_k2[layer].astype(F32))) + lambda_init)
        o = _diff_attn(lam.reshape(1, 1), seq(q), seq(k), vt, subln[layer].astype(F32).reshape(V_DIM, 1),
                       lambda_init, tq)

        x1 = _mix_out(x.reshape(n, d), o.reshape(n, d), ga, ml.reshape(n, d), w_attn_o[layer].astype(BF16),
                      w_out[layer].astype(BF16), tm)

        kc, vc = _mem_kv(mem, row(norm_mem[layer]), w_ckv[layer].astype(BF16), row(ck_norm[layer]))
        w_r = jnp.concatenate([w_group[layer], w_router[layer],
                               jnp.zeros((d, LANES - N_GROUPS - N_EXPERTS), F32)], axis=1).astype(F32)
        b_r = jnp.concatenate([b_group[layer], b_router[layer],
                               jnp.zeros((LANES - N_GROUPS - N_EXPERTS,), F32)]).reshape(1, LANES).astype(F32)
        wrh, wrl = _split_bf16(w_r)
        x2, hn, eid, ew, hist = _cross_router(
            seq(x1), row(norm_cx[layer]), w_cq[layer].astype(BF16), row(cq_norm[layer]), kc, vc,
            w_co[layer].astype(BF16), row(norm_ffn[layer]), wrh, wrl, b_r, ts)

        p_rows = _plan_rows(n, ts, eb)
        plan, lrow = _moe_plan(hist[:, :, 0, :N_EXPERTS].reshape(n // ts, N_EXPERTS), p_rows, eb)
        eid2d = eid.reshape(n, LANES)
        xs = _dispatch(plan, hn.reshape(n, d), eid2d, lrow, p_rows, ts)
        ys = _experts(plan, xs, w_gate_up[layer], w_down[layer], eb)
        x = _combine(plan, x2.reshape(n, d), ew.reshape(n, LANES), eid2d, lrow, ys, ts).reshape(bsz, s, d)
    return x
```

```python
import functools
import math

import jax
import jax.numpy as jnp
from jax import lax
from jax.experimental import pallas as pl
from jax.experimental.pallas import tpu as pltpu

F32 = jnp.float32
BF16 = jnp.bfloat16
I32 = jnp.int32

D_MODEL = 1024
CHUNK = 64
LRU_BLOCKS = 8
LRU_BLOCK_WIDTH = D_MODEL // LRU_BLOCKS
CONV_WIDTH = 4
LRU_C = 8.0
ATTN_HEADS = 8
HEAD_DIM = 64
V_DIM = 2 * HEAD_DIM
ROPE_THETA = 500000.0
ROT_DIMS = HEAD_DIM // 4
ROT_HALF = ROT_DIMS // 2
MEM_HEADS = 4
MEM_HEAD_DIM = 128
MEM_WIDTH = MEM_HEADS * MEM_HEAD_DIM
N_GROUPS = 4
EXPERTS_PER_GROUP = 8
N_EXPERTS = N_GROUPS * EXPERTS_PER_GROUP
TOP_K = 2
EXPERT_FF = 512
N_PROJ = 7
EPS = 1e-6
NEG_INF = -1e30
LOG2_E = math.log2(math.e)
ATTN_COL_GROUP = 512
ATTN_HEADS_PER_STEP = 4

LANES = 128
SUBLANES = 8
V7X_VMEM_BYTES = 64 * 1024 * 1024
MIB = 1024 * 1024


def _vmem_limit(estimate_bytes):
    return int(min(max(estimate_bytes * 3 // 2, 16 * MIB), V7X_VMEM_BYTES - 8 * MIB))


def _params(semantics, vmem_estimate):
    return pltpu.CompilerParams(dimension_semantics=semantics, vmem_limit_bytes=_vmem_limit(vmem_estimate))


def _resident(shape, index_map):
    return pl.BlockSpec(shape, index_map, pipeline_mode=pl.Buffered(1))


def _rms(x, g):
    return x * lax.rsqrt(jnp.mean(x * x, axis=-1, keepdims=True) + EPS) * g


def _sigmoid(x):
    return 0.5 * jnp.tanh(0.5 * x) + 0.5


def _segment_ones():
    r = lax.broadcasted_iota(I32, (LANES, LANES), 0) // HEAD_DIM
    c = lax.broadcasted_iota(I32, (LANES, LANES), 1) // HEAD_DIM
    return (r == c).astype(BF16)


def _qk_post(p, gain, cos_t, sin_lo, sin_hi, seg, scale):
    cols = []
    for c in range(D_MODEL // LANES):
        pc = p[:, c * LANES:(c + 1) * LANES]
        ss = jnp.dot((pc * pc).astype(BF16), seg, preferred_element_type=F32)
        y = pc * lax.rsqrt(ss * (1.0 / HEAD_DIM) + EPS) * gain
        y = y * cos_t + pltpu.roll(y, LANES - ROT_HALF, 1) * sin_lo + pltpu.roll(y, ROT_HALF, 1) * sin_hi
        cols.append((y * scale).astype(BF16))
    return jnp.concatenate(cols, axis=1)


def _in_proj_kernel(x_ref, g_ref, w_ref, qg_ref, kg_ref, rope_ref, cw_ref, cb_ref, wax_ref, bax_ref, lam_ref, wo_ref,
                    ml_ref, q_ref, k_ref, vt_ref, ga_ref, xpad_ref, hprev_ref, *, per_seq):
    @pl.when(pl.program_id(0) % per_seq == 0)
    def _():
        xpad_ref[0:SUBLANES, :] = jnp.zeros((SUBLANES, D_MODEL), F32)
        hprev_ref[...] = jnp.zeros_like(hprev_ref)

    h = _rms(x_ref[...], g_ref[...]).astype(BF16)

    def proj(j):
        return jnp.dot(h, w_ref[:, j * D_MODEL:(j + 1) * D_MODEL], preferred_element_type=F32)

    xc, pre_a, pre_x = _lru_conv_gates(proj(0), cw_ref, cb_ref, wax_ref, xpad_ref)
    seg = _segment_ones()
    tab = rope_ref[...]
    seg_lane = lax.broadcasted_iota(I32, tab.shape, 1) % HEAD_DIM
    first, second = seg_lane < ROT_HALF, (seg_lane >= ROT_HALF) & (seg_lane < ROT_DIMS)
    cos_t = jnp.where(first, tab, jnp.where(second, pltpu.roll(tab, ROT_HALF, 1), 1.0))
    sin_lo = jnp.where(first, -pltpu.roll(tab, LANES - ROT_HALF, 1), 0.0)
    sin_hi = jnp.where(second, tab, 0.0)
    q_ref[...] = _qk_post(proj(2), qg_ref[...], cos_t, sin_lo, sin_hi, seg, HEAD_DIM ** -0.5 * LOG2_E)
    k_ref[...] = _qk_post(proj(3), kg_ref[...], cos_t, sin_lo, sin_hi, seg, 1.0)
    hr = _lru_scan(xc, pre_a, pre_x, bax_ref, lam_ref, hprev_ref)
    gate, merge_gate = proj(1), proj(5)
    v = proj(4)
    for hd in range(ATTN_HEADS):
        vt_ref[hd * V_DIM:(hd + 1) * V_DIM, :] = v[:, hd * V_DIM:(hd + 1) * V_DIM].T.astype(BF16)
    ml_ref[...] = _lru_out(hr, gate, merge_gate, wo_ref)
    ga_ref[...] = proj(6).astype(BF16)


def _in_proj(x2d, g, w_in, qg, kg, rope, lru_params, tm, seq_len):
    n = x2d.shape[0]
    per_seq = seq_len // tm
    row = lambda i: (i, 0)
    tok = pl.BlockSpec((tm, D_MODEL), row)
    tok_out = jax.ShapeDtypeStruct((n, D_MODEL), BF16)
    vt_out = jax.ShapeDtypeStruct((n // seq_len, D_MODEL, seq_len), BF16)
    vt_spec = pl.BlockSpec((None, D_MODEL, tm), lambda i: (i // per_seq, 0, i % per_seq))
    whole = lambda a: _resident(a.shape, lambda i: (0,) * a.ndim)
    consts = (g, w_in, qg, kg)
    est = (sum(a.size * a.dtype.itemsize for a in consts + tuple(lru_params)) + 2 * tm * D_MODEL * 4
           + 2 * tm * LANES * 4 + 5 * 2 * tm * D_MODEL * 2 + 16 * tm * D_MODEL * 4)
    return pl.pallas_call(
        functools.partial(_in_proj_kernel, per_seq=per_seq),
        out_shape=[tok_out] * 3 + [vt_out, tok_out],
        grid=(n // tm,),
        in_specs=[tok] + [whole(a) for a in consts] + [pl.BlockSpec((tm, LANES), row)]
        + [whole(a) for a in lru_params],
        out_specs=[tok] * 3 + [vt_spec, tok],
        scratch_shapes=[pltpu.VMEM((tm + SUBLANES, D_MODEL), F32), pltpu.VMEM((SUBLANES, D_MODEL), F32)],
        compiler_params=_params(("arbitrary",), est),
        name="in_proj_lru",
    )(x2d, *consts, rope, *lru_params)


def _lru_conv_gates(x, cw_ref, cb_ref, wax_ref, xpad_ref):
    tt = x.shape[0]
    xpad_ref[SUBLANES:SUBLANES + tt, :] = x
    cw = cw_ref[...]
    xc = cb_ref[...] + cw[3:4] * x
    for j in range(1, CONV_WIDTH):
        xc = xc + cw[CONV_WIDTH - 1 - j:CONV_WIDTH - j] * xpad_ref[SUBLANES - j:SUBLANES - j + tt, :]
    xpad_ref[0:SUBLANES, :] = x[tt - SUBLANES:tt]

    xcb = xc.astype(BF16)
    ra, ri = [], []
    for n in range(LRU_BLOCKS):
        g = jnp.dot(xcb[:, n * LRU_BLOCK_WIDTH:(n + 1) * LRU_BLOCK_WIDTH], wax_ref[n], preferred_element_type=F32)
        ra.append(g[:, :LRU_BLOCK_WIDTH])
        ri.append(g[:, LRU_BLOCK_WIDTH:])
    return xc, jnp.concatenate(ra, axis=1), jnp.concatenate(ri, axis=1)


def _lru_scan(xc, pre_a, pre_x, bax_ref, lam_ref, hprev_ref):
    nblk = xc.shape[0] // SUBLANES
    bax = bax_ref[...]
    t_r = jnp.tanh(pre_a + bax[0:1])
    i = 0.5 * jnp.tanh(pre_x + bax[1:2]) + 0.5
    u = (0.5 * LRU_C) * jax.nn.softplus(-lam_ref[...]) * (t_r + 1.0)
    a = jnp.exp(-u)
    gain2 = jnp.tanh(u) * (a * a + 1.0)
    b = jnp.where(gain2 > 0.0, gain2 * lax.rsqrt(gain2), 0.0) * i * xc

    a3 = a.reshape(nblk, SUBLANES, D_MODEL)
    b3 = b.reshape(nblk, SUBLANES, D_MODEL)
    sub = lax.broadcasted_iota(I32, (nblk, SUBLANES, D_MODEL), 1)
    shift = 1
    while shift < SUBLANES:
        keep = sub >= shift
        a_sh = pltpu.roll(a3, shift, 1)
        b_sh = pltpu.roll(b3, shift, 1)
        b3 = jnp.where(keep, a3 * b_sh + b3, b3)
        a3 = jnp.where(keep, a3 * a_sh, a3)
        shift *= 2
    h_last = hprev_ref[...]
    groups = []
    for blk in range(nblk):
        hb = a3[blk] * h_last + b3[blk]
        groups.append(hb)
        h_last = jnp.broadcast_to(hb[SUBLANES - 1:SUBLANES], (SUBLANES, D_MODEL))
    hprev_ref[...] = h_last
    return jnp.concatenate(groups, axis=0)


def _lru_out(hr, gate, merge_gate, wo_ref):
    y = (jax.nn.gelu(gate) * hr).astype(BF16)
    yl = jnp.dot(y, wo_ref[...], preferred_element_type=F32)
    return (_sigmoid(merge_gate) * yl).astype(BF16)


def _attn_kernel(lam_ref, q_ref, qnext_ref, k_ref, vt_ref, sub_ref, o_ref, m_ref, l_ref, acc_ref, qz_ref, sa_ref,
                 sb_ref, *, tq, out_scale):
    i = pl.program_id(2)
    last = pl.num_programs(2) - 1
    heads = range(ATTN_HEADS_PER_STEP)
    m_ref[...] = jnp.full_like(m_ref, NEG_INF)
    l_ref[...] = jnp.zeros_like(l_ref)
    acc_ref[...] = jnp.zeros_like(acc_ref)
    cw = ATTN_COL_GROUP

    def load_queries(src_ref):
        row = lax.broadcasted_iota(I32, (V_DIM, tq), 0)
        for h in heads:
            qt = src_ref[:, h * V_DIM:(h + 1) * V_DIM].astype(F32).T
            zero = jnp.zeros_like(qt)
            qz_ref[h] = jnp.concatenate([jnp.where(row < HEAD_DIM, qt, zero), jnp.where(row >= HEAD_DIM, qt, zero)],
                                        axis=1).astype(BF16)

    def block_off(j):
        return pl.multiple_of(j * tq, tq)

    def scores(j, s_ref):
        for h in heads:
            kb = k_ref[pl.ds(block_off(j), tq), h * V_DIM:(h + 1) * V_DIM]
            s_ref[h] = jnp.dot(kb, qz_ref[h], preferred_element_type=F32)

    def softmax_pv(j, s_ref, diagonal):
        for h in heads:
            vtb = vt_ref[h * V_DIM:(h + 1) * V_DIM, pl.ds(block_off(j), tq)]
            for g in range(2 * tq // cw):
                cols = pl.ds(g * cw, cw)
                s = s_ref[h, :, cols]
                if diagonal:
                    key = lax.broadcasted_iota(I32, (tq, cw), 0)
                    qry = (lax.broadcasted_iota(I32, (tq, cw), 1) + g * cw) % tq
                    s = jnp.where((key // CHUNK) <= (qry // CHUNK), s, NEG_INF)
                m_prev = m_ref[h, :, cols]
                m_new = jnp.maximum(m_prev, jnp.max(s, axis=0, keepdims=True))
                alpha = jnp.exp2(m_prev - m_new)
                p = jnp.exp2(s - m_new)
                l_ref[h, :, cols] = alpha * l_ref[h, :, cols] + jnp.sum(p, axis=0, keepdims=True)
                acc_ref[h, :, cols] = alpha * acc_ref[h, :, cols] + jnp.dot(vtb, p.astype(BF16),
                                                                            preferred_element_type=F32)
                m_ref[h, :, cols] = m_new

    @pl.when(i == 0)
    def _():
        load_queries(q_ref)
        scores(0, sa_ref)

    def pair(p, carry):
        j = 2 * p
        scores(j + 1, sb_ref)
        softmax_pv(j, sa_ref, False)
        scores(j + 2, sa_ref)
        softmax_pv(j + 1, sb_ref, False)
        return carry

    lax.fori_loop(0, i // 2, pair, 0)

    @pl.when(i % 2 == 1)
    def _():
        scores(i, sb_ref)
        softmax_pv(i - 1, sa_ref, False)
        softmax_pv(i, sb_ref, True)

    @pl.when(i % 2 == 0)
    def _():
        softmax_pv(i, sa_ref, True)

    def write_output():
        for h in heads:
            o12 = acc_ref[h] * (1.0 / l_ref[h])
            ot = o12[:, :tq] - lam_ref[0, 0] * o12[:, tq:]
            ot = ot * lax.rsqrt(jnp.mean(ot * ot, axis=0, keepdims=True) + EPS) * sub_ref[...] * out_scale
            o_ref[:, h * V_DIM:(h + 1) * V_DIM] = ot.T.astype(BF16)

    @pl.when(i < last)
    def _():
        load_queries(qnext_ref)
        scores(0, sa_ref)
        write_output()

    @pl.when(i == last)
    def _():
        write_output()


def _diff_attn(lam, q, k, vt, sub, lambda_init, tq):
    bsz, s, _ = q.shape
    hps = ATTN_HEADS_PER_STEP
    width = hps * V_DIM
    nq = s // tq
    qspec = pl.BlockSpec((None, tq, width), lambda b, h, i: (b, i, h))
    qnext_spec = pl.BlockSpec((None, tq, width), lambda b, h, i: (b, jnp.minimum(i + 1, nq - 1), h))
    kspec = pl.BlockSpec((None, s, width), lambda b, h, i: (b, 0, h))
    vtspec = pl.BlockSpec((None, width, s), lambda b, h, i: (b, h, 0))
    est = 2 * 2 * s * width * 2 + 4 * tq * width * 2 + hps * (V_DIM * 2 * tq * 6 + 2 * tq * 2 * tq * 4) \
        + 3 * 2 * tq * tq * 4
    return pl.pallas_call(
        functools.partial(_attn_kernel, tq=tq, out_scale=1.0 - lambda_init),
        out_shape=jax.ShapeDtypeStruct((bsz, s, ATTN_HEADS * V_DIM), BF16),
        grid=(bsz, ATTN_HEADS // hps, nq),
        in_specs=[pl.BlockSpec(memory_space=pltpu.SMEM), qspec, qnext_spec, kspec, vtspec,
                  pl.BlockSpec((V_DIM, 1), lambda b, h, i: (0, 0))],
        out_specs=qspec,
        scratch_shapes=[pltpu.VMEM((hps, 1, 2 * tq), F32), pltpu.VMEM((hps, 1, 2 * tq), F32),
                        pltpu.VMEM((hps, V_DIM, 2 * tq), F32), pltpu.VMEM((hps, V_DIM, 2 * tq), BF16),
                        pltpu.VMEM((hps, tq, 2 * tq), F32), pltpu.VMEM((hps, tq, 2 * tq), F32)],
        compiler_params=_params(("parallel", "parallel", "arbitrary"), est),
        name="diff_attn",
    )(lam, q, q, k, vt, sub)


def _mix_out(x, o, ga, ml, wao_ref, wout_ref):
    ya = jnp.dot(o, wao_ref[...], preferred_element_type=F32)
    mixed = ml.astype(F32) + _sigmoid(ga.astype(F32)) * ya
    return x + jnp.dot(mixed.astype(BF16), wout_ref[...], preferred_element_type=F32)


def _mem_kv_kernel(mem_ref, g_ref, w_ref, ckg_ref, k_ref, v_ref):
    h = _rms(mem_ref[...], g_ref[...]).astype(BF16)
    kv = jnp.dot(h, w_ref[...], preferred_element_type=F32)
    ks = [_rms(kv[:, hd * MEM_HEAD_DIM:(hd + 1) * MEM_HEAD_DIM], ckg_ref[...]) for hd in range(MEM_HEADS)]
    k_ref[...] = jnp.concatenate(ks, axis=1).astype(BF16)
    v_ref[...] = kv[:, MEM_WIDTH:].astype(BF16)


def _mem_kv(mem, g, w, ckg):
    bsz, m, _ = mem.shape
    fixed = lambda b: (0, 0)
    out = pl.BlockSpec((None, m, MEM_WIDTH), lambda b: (b, 0, 0))
    est = 2 * m * D_MODEL * 4 + w.size * 2 + 4 * m * MEM_WIDTH * 2 + 4 * m * D_MODEL * 4
    return pl.pallas_call(
        _mem_kv_kernel,
        out_shape=[jax.ShapeDtypeStruct((bsz, m, MEM_WIDTH), BF16)] * 2,
        grid=(bsz,),
        in_specs=[pl.BlockSpec((None, m, D_MODEL), lambda b: (b, 0, 0)), _resident((1, D_MODEL), fixed),
                  _resident(w.shape, fixed), _resident((1, MEM_HEAD_DIM), fixed)],
        out_specs=[out, out],
        compiler_params=_params(("parallel",), est),
        name="mem_kv",
    )(mem, g, w, ckg)


def _split_bf16(x):
    hi = x.astype(BF16)
    return hi, (x - hi.astype(F32)).astype(BF16)


def _cross_router_kernel(x_ref, o_ref, ga_ref, ml_ref, wao_ref, wout_ref, gcx_ref, wcq_ref, cqg_ref, kc_ref, vc_ref,
                         wco_ref, gffn_ref, wrh_ref, wrl_ref, br_ref, x2_ref, hn_ref, eid_ref, ew_ref, hist_ref):
    x1 = _mix_out(x_ref[...], o_ref[...], ga_ref[...], ml_ref[...], wao_ref, wout_ref)
    q = jnp.dot(_rms(x1, gcx_ref[...]).astype(BF16), wcq_ref[...], preferred_element_type=F32)
    outs = []
    for hd in range(MEM_HEADS):
        sl = slice(hd * MEM_HEAD_DIM, (hd + 1) * MEM_HEAD_DIM)
        qh = _rms(q[:, sl], cqg_ref[...]) * MEM_HEAD_DIM ** -0.5
        s = lax.dot_general(qh.astype(BF16), kc_ref[:, sl], (((1,), (1,)), ((), ())), preferred_element_type=F32)
        p = jnp.exp(s - jnp.max(s, axis=1, keepdims=True))
        o = jnp.dot(p.astype(BF16), vc_ref[:, sl], preferred_element_type=F32)
        outs.append(o / jnp.sum(p, axis=1, keepdims=True))
    x2 = x1 + jnp.dot(jnp.concatenate(outs, axis=1).astype(BF16), wco_ref[...], preferred_element_type=F32)
    x2_ref[...] = x2

    hn = _rms(x2, gffn_ref[...])
    hn_ref[...] = hn.astype(BF16)
    h_hi, h_lo = _split_bf16(hn)
    logits = (jnp.dot(h_hi, wrh_ref[...], preferred_element_type=F32)
              + jnp.dot(h_lo, wrh_ref[...], preferred_element_type=F32)
              + jnp.dot(h_hi, wrl_ref[...], preferred_element_type=F32)) + br_ref[...]
    lane = lax.broadcasted_iota(I32, logits.shape, 1)
    is_group = lane < N_GROUPS
    gl = jnp.where(is_group, logits, NEG_INF)
    gmax = jnp.max(gl, axis=1, keepdims=True)
    gval = 1.0 / jnp.sum(jnp.where(is_group, jnp.exp(gl - gmax), 0.0), axis=1, keepdims=True)
    gidx = jnp.min(jnp.where(gl == gmax, lane, LANES), axis=1, keepdims=True)
    lane_group = lax.shift_right_logical(lane + (EXPERTS_PER_GROUP - N_GROUPS), 3) - 1
    chosen = lane_group == gidx
    el = jnp.where(chosen, logits, NEG_INF)
    v1 = jnp.max(el, axis=1, keepdims=True)
    i1 = jnp.min(jnp.where(chosen & (el == v1), lane, LANES), axis=1, keepdims=True)
    rest = chosen & (lane != i1)
    el2 = jnp.where(rest, logits, NEG_INF)
    v2 = jnp.max(el2, axis=1, keepdims=True)
    i2 = jnp.min(jnp.where(rest & (el2 == v2), lane, LANES), axis=1, keepdims=True)
    t = jnp.exp(v2 - v1)
    w1 = gval / (1.0 + t)
    w2 = gval * t / (1.0 + t)
    e1, e2 = i1 - N_GROUPS, i2 - N_GROUPS
    eid_ref[...] = jnp.where(lane == 0, e1, jnp.where(lane == 1, e2, 0))
    ew_ref[...] = jnp.where(lane == 0, w1, jnp.where(lane == 1, w2, 0.0))
    count = jnp.sum((lane == e1).astype(I32) + (lane == e2).astype(I32), axis=0, keepdims=True)
    hist_ref[...] = jnp.broadcast_to(count, hist_ref.shape)


def _cross_router(x, o, ga, ml, wao, wout, gcx, wcq, cqg, kc, vc, wco, gffn, wrh, wrl, br, tm):
    bsz, s, _ = x.shape
    m = kc.shape[1]
    tok = pl.BlockSpec((None, tm, D_MODEL), lambda b, i: (b, i, 0))
    small = pl.BlockSpec((None, tm, LANES), lambda b, i: (b, i, 0))
    memb = pl.BlockSpec((None, m, MEM_WIDTH), lambda b, i: (b, 0, 0))
    fixed = lambda b, i: (0, 0)
    est = (3 * 2 * tm * D_MODEL * 4 + 4 * 2 * tm * D_MODEL * 2 + 2 * 2 * tm * LANES * 4 + 2 * 2 * m * MEM_WIDTH * 2
           + 2 * D_MODEL * D_MODEL * 2 + 2 * D_MODEL * MEM_WIDTH * 2 + 2 * D_MODEL * LANES * 2 + 8 * tm * D_MODEL * 4)
    return pl.pallas_call(
        _cross_router_kernel,
        out_shape=[jax.ShapeDtypeStruct((bsz, s, D_MODEL), F32), jax.ShapeDtypeStruct((bsz, s, D_MODEL), BF16),
                   jax.ShapeDtypeStruct((bsz, s, LANES), I32), jax.ShapeDtypeStruct((bsz, s, LANES), F32),
                   jax.ShapeDtypeStruct((bsz, s // tm, SUBLANES, LANES), I32)],
        grid=(bsz, s // tm),
        in_specs=[tok, tok, tok, tok, _resident(wao.shape, fixed), _resident(wout.shape, fixed),
                  _resident((1, D_MODEL), fixed), _resident(wcq.shape, fixed), _resident((1, MEM_HEAD_DIM), fixed),
                  memb, memb, _resident(wco.shape, fixed), _resident((1, D_MODEL), fixed),
                  _resident(wrh.shape, fixed), _resident(wrl.shape, fixed), _resident((1, LANES), fixed)],
        out_specs=[tok, tok, small, small,
                   pl.BlockSpec((None, None, SUBLANES, LANES), lambda b, i: (b, i, 0, 0))],
        compiler_params=_params(("parallel", "parallel"), est),
        name="mix_cross_router",
    )(x, o, ga, ml, wao, wout, gcx, wcq, cqg, kc, vc, wco, gffn, wrh, wrl, br)


MOE_CHUNK = SUBLANES


def _local_rows(tm):
    return TOP_K * tm + N_EXPERTS * MOE_CHUNK


def _segment_loop(t, lst_ref, gofs_ref, nch_ref, fn):
    def per_expert(e, carry):
        k = t * N_EXPERTS + e
        lst, gofs = lst_ref[k], gofs_ref[k]

        def per_chunk(c, cc):
            fn(pl.multiple_of(lst + c * MOE_CHUNK, MOE_CHUNK), pl.multiple_of(gofs + c * MOE_CHUNK, MOE_CHUNK))
            return cc
        lax.fori_loop(0, nch_ref[k], per_chunk, 0)
        return carry
    lax.fori_loop(0, N_EXPERTS, per_expert, 0)


def _repeat(count, fn):
    def body(c, carry):
        fn()
        return carry
    lax.fori_loop(0, count, body, 0)


MOE_WAIT_GROUP = 8


def _wait_chunks(count, wait_rows):
    _repeat(lax.shift_right_logical(count, 3), lambda: wait_rows(MOE_WAIT_GROUP * MOE_CHUNK))
    _repeat(count & (MOE_WAIT_GROUP - 1), lambda: wait_rows(MOE_CHUNK))


def _dispatch_kernel(gofs_ref, lst_ref, nch_ref, ntot_ref, tail_ref, tailn_ref, tailtot_ref,
                     hn_ref, eid_ref, lrow_ref, xs_hbm, before_ref, xloc_ref, zero_ref, sem, *, tm):
    t = pl.program_id(0)
    last = pl.num_programs(0) - 1
    slot = t % 2
    na = TOP_K * tm
    loc = xloc_ref.shape[1]

    @pl.when(t == 0)
    def _():
        r = lax.broadcasted_iota(I32, (na, na), 0)
        c = lax.broadcasted_iota(I32, (na, na), 1)
        before_ref[...] = (r < c).astype(BF16)
        zero_ref[...] = jnp.zeros_like(zero_ref)

    et = eid_ref[...].astype(F32).T
    e_row = jnp.concatenate([et[0:1], et[1:2]], axis=1)
    hit = lax.broadcasted_iota(I32, (LANES, na), 0).astype(F32) == e_row
    hit_b = hit.astype(BF16)
    rank = jnp.dot(hit_b, before_ref[...], preferred_element_type=F32)
    start = jnp.dot(lrow_ref[...].astype(BF16), hit_b, preferred_element_type=F32)[0:1] * MOE_CHUNK
    pos = (start + jnp.sum(jnp.where(hit, rank, 0.0), axis=0, keepdims=True)).astype(I32)
    r = lax.broadcasted_iota(I32, (loc, tm), 0)
    sel = ((r == pos[:, :tm]) | (r == pos[:, tm:])).astype(BF16)
    xloc_ref[slot] = jnp.dot(sel, hn_ref[...], preferred_element_type=F32)

    def copy_out(local_row, global_row, s, rows=MOE_CHUNK):
        return pltpu.make_async_copy(xloc_ref.at[s, pl.ds(local_row, rows), :],
                                     xs_hbm.at[pl.ds(global_row, rows), :], sem.at[s])

    _segment_loop(t, lst_ref, gofs_ref, nch_ref, lambda lr, gr: copy_out(lr, gr, slot).start())

    @pl.when(t > 0)
    def _():
        _wait_chunks(ntot_ref[jnp.maximum(t - 1, 0)], lambda rows: copy_out(0, 0, 1 - slot, rows).wait())

    @pl.when(t == last)
    def _():
        _wait_chunks(ntot_ref[t], lambda rows: copy_out(0, 0, slot, rows).wait())

        def zero_out(global_row):
            return pltpu.make_async_copy(zero_ref, xs_hbm.at[pl.ds(global_row, MOE_CHUNK), :], sem.at[0])

        def per_expert(e, carry):
            def per_chunk(c, cc):
                zero_out(pl.multiple_of(tail_ref[e] + c * MOE_CHUNK, MOE_CHUNK)).start()
                return cc
            lax.fori_loop(0, tailn_ref[e], per_chunk, 0)
            return carry
        lax.fori_loop(0, N_EXPERTS, per_expert, 0)
        _repeat(tailtot_ref[0], lambda: zero_out(0).wait())


def _dispatch(plan, hn2d, eid2d, lrow, p_rows, tm):
    n = hn2d.shape[0]
    na = TOP_K * tm
    loc = _local_rows(tm)
    pre = (plan["gofs"], plan["lst"], plan["nch"], plan["ntot"], plan["tail"], plan["tailn"], plan["tailtot"])
    est = 2 * tm * D_MODEL * 2 + na * na * 2 + 2 * loc * D_MODEL * 4 + loc * tm * 4 + 4 * LANES * na * 4
    grid_spec = pltpu.PrefetchScalarGridSpec(
        num_scalar_prefetch=len(pre),
        grid=(n // tm,),
        in_specs=[pl.BlockSpec((tm, D_MODEL), lambda t, *_: (t, 0)), pl.BlockSpec((tm, LANES), lambda t, *_: (t, 0)),
                  pl.BlockSpec((None, SUBLANES, LANES), lambda t, *_: (t, 0, 0))],
        out_specs=pl.BlockSpec(memory_space=pl.ANY),
        scratch_shapes=[pltpu.VMEM((na, na), BF16), pltpu.VMEM((2, loc, D_MODEL), F32),
                        pltpu.VMEM((MOE_CHUNK, D_MODEL), F32), pltpu.SemaphoreType.DMA((2,))],
    )
    return pl.pallas_call(
        functools.partial(_dispatch_kernel, tm=tm),
        out_shape=jax.ShapeDtypeStruct((p_rows, D_MODEL), F32),
        grid_spec=grid_spec,
        compiler_params=_params(("arbitrary",), est),
        name="moe_dispatch",
    )(*pre, hn2d, eid2d, lrow)


def _expert_kernel(blk_e_ref, blk_src_ref, blk_n_ref, xs_ref, wgu_ref, wd_ref, ys_ref, wgu_bf, wd_bf):
    i = pl.program_id(0)

    @pl.when(blk_n_ref[i] > 0)
    def _():
        @pl.when((i == 0) | (blk_e_ref[i] != blk_e_ref[jnp.maximum(i - 1, 0)]))
        def _():
            wgu_bf[...] = wgu_ref[0].astype(BF16)
            wd_bf[...] = wd_ref[0].astype(BF16)

        gu = jnp.dot(xs_ref[...].astype(BF16), wgu_bf[...], preferred_element_type=F32)
        act = (jax.nn.silu(gu[:, :EXPERT_FF]) * gu[:, EXPERT_FF:]).astype(BF16)
        ys_ref[...] = jnp.dot(act, wd_bf[...], preferred_element_type=F32)


def _experts(plan, xs, wgu, wd, eb):
    n_blk = xs.shape[0] // eb
    rows = pl.BlockSpec((eb, D_MODEL), lambda i, be, bs, bn: (bs[i], 0))
    w_elems = wgu.shape[1] * wgu.shape[2] + wd.shape[1] * wd.shape[2]
    est = 2 * 2 * eb * D_MODEL * 4 + 2 * w_elems * 4 + w_elems * 2 + 4 * eb * D_MODEL * 4
    grid_spec = pltpu.PrefetchScalarGridSpec(
        num_scalar_prefetch=3,
        grid=(n_blk,),
        in_specs=[rows, pl.BlockSpec((1,) + wgu.shape[1:], lambda i, be, bs, bn: (be[i], 0, 0)),
                  pl.BlockSpec((1,) + wd.shape[1:], lambda i, be, bs, bn: (be[i], 0, 0))],
        out_specs=rows,
        scratch_shapes=[pltpu.VMEM(wgu.shape[1:], BF16), pltpu.VMEM(wd.shape[1:], BF16)],
    )
    return pl.pallas_call(
        _expert_kernel,
        out_shape=jax.ShapeDtypeStruct(xs.shape, F32),
        grid_spec=grid_spec,
        compiler_params=_params(("arbitrary",), est),
        name="experts",
    )(plan["blk_e"], plan["blk_src"], plan["blk_n"], xs, wgu, wd)


def _combine_kernel(gofs_ref, lst_ref, nch_ref, ntot_ref, x2_ref, ew_ref, eid_ref, lrow_ref, ys_hbm, o_ref,
                    before_ref, yloc_ref, sem, *, tm):
    t = pl.program_id(0)
    last = pl.num_programs(0) - 1
    slot = t % 2
    na = TOP_K * tm
    loc = yloc_ref.shape[1]

    def copy_in(local_row, global_row, s, rows=MOE_CHUNK):
        return pltpu.make_async_copy(ys_hbm.at[pl.ds(global_row, rows), :],
                                     yloc_ref.at[s, pl.ds(local_row, rows), :], sem.at[s])

    def fetch(tile, s):
        _segment_loop(tile, lst_ref, gofs_ref, nch_ref, lambda lr, gr: copy_in(lr, gr, s).start())

    @pl.when(t == 0)
    def _():
        r = lax.broadcasted_iota(I32, (na, na), 0)
        c = lax.broadcasted_iota(I32, (na, na), 1)
        before_ref[...] = (c < r).astype(BF16)
        yloc_ref[...] = jnp.zeros_like(yloc_ref)
        fetch(0, 0)

    @pl.when(t < last)
    def _():
        fetch(t + 1, 1 - slot)

    _wait_chunks(ntot_ref[t], lambda rows: copy_in(0, 0, slot, rows).wait())

    eid = eid_ref[...]
    lane = lax.broadcasted_iota(I32, (tm, LANES), 1)
    hit = jnp.concatenate([lane == eid[:, 0:1], lane == eid[:, 1:2]], axis=0)
    rank = jnp.dot(before_ref[...], hit.astype(BF16), preferred_element_type=F32)
    start = lrow_ref[0:1, :] * MOE_CHUNK
    pos = jnp.sum(jnp.where(hit, rank + start, 0.0), axis=1, keepdims=True).astype(I32)
    col = lax.broadcasted_iota(I32, (tm, loc), 1)
    ew = ew_ref[...]
    sel = jnp.where(col == pos[:tm], ew[:, 0:1], 0.0) + jnp.where(col == pos[tm:], ew[:, 1:2], 0.0)
    o_ref[...] = x2_ref[...] + jnp.dot(sel.astype(BF16), yloc_ref[slot].astype(BF16), preferred_element_type=F32)


def _combine(plan, x2, ew, eid2d, lrow, ys, tm):
    n = x2.shape[0]
    na = TOP_K * tm
    loc = _local_rows(tm)
    pre = (plan["gofs"], plan["lst"], plan["nch"], plan["ntot"])
    tok = pl.BlockSpec((tm, D_MODEL), lambda t, *_: (t, 0))
    small = pl.BlockSpec((tm, LANES), lambda t, *_: (t, 0))
    est = 2 * 2 * tm * D_MODEL * 4 + na * na * 2 + 2 * loc * D_MODEL * 4 + loc * D_MODEL * 2 + 2 * tm * loc * 4 \
        + 3 * tm * D_MODEL * 4
    grid_spec = pltpu.PrefetchScalarGridSpec(
        num_scalar_prefetch=len(pre),
        grid=(n // tm,),
        in_specs=[tok, small, small, pl.BlockSpec((None, SUBLANES, LANES), lambda t, *_: (t, 0, 0)),
                  pl.BlockSpec(memory_space=pl.ANY)],
        out_specs=tok,
        scratch_shapes=[pltpu.VMEM((na, na), BF16), pltpu.VMEM((2, loc, D_MODEL), F32),
                        pltpu.SemaphoreType.DMA((2,))],
    )
    return pl.pallas_call(
        functools.partial(_combine_kernel, tm=tm),
        out_shape=jax.ShapeDtypeStruct((n, D_MODEL), F32),
        grid_spec=grid_spec,
        compiler_params=_params(("arbitrary",), est),
        name="moe_combine",
    )(*pre, x2, ew, eid2d, lrow, ys)


def _rope_table(positions):
    inv_freq = jnp.exp(-math.log(ROPE_THETA) * jnp.arange(ROT_HALF, dtype=F32) / ROT_HALF)
    ang = positions.astype(F32).reshape(-1, 1) * inv_freq
    seg = jnp.concatenate([jnp.cos(ang), jnp.sin(ang), jnp.zeros((ang.shape[0], HEAD_DIM - ROT_DIMS), F32)], axis=1)
    return jnp.tile(seg, (1, LANES // HEAD_DIM))


def _plan_rows(n_tok, tm, eb):
    worst = n_tok * TOP_K + (n_tok // tm) * N_EXPERTS * (MOE_CHUNK - 1)
    return (worst + eb - 1) // eb * eb + N_EXPERTS * eb


def _moe_plan(hist, p_rows, eb):
    n_tiles = hist.shape[0]
    seg = (hist + MOE_CHUNK - 1) // MOE_CHUNK * MOE_CHUNK
    tot = jnp.sum(seg, axis=0)
    region = (tot + eb - 1) // eb * eb
    pend = jnp.cumsum(region)
    pstart = pend - region
    gofs = pstart[None, :] + jnp.cumsum(seg, axis=0) - seg
    lst = jnp.cumsum(seg, axis=1) - seg
    nch = seg // MOE_CHUNK
    n_blk = p_rows // eb
    blk_first = jnp.arange(n_blk, dtype=I32) * eb
    blk_e = jnp.minimum(jnp.sum(blk_first[:, None] >= pend[None, :], axis=1), N_EXPERTS - 1).astype(I32)
    blk_n = jnp.clip(pstart[blk_e] + tot[blk_e] - blk_first, 0, eb)
    blk_src = jnp.minimum(jnp.arange(n_blk, dtype=I32), jnp.maximum(pend[-1] // eb - 1, 0))
    tailn = (region - tot) // MOE_CHUNK
    lrow = jnp.pad((lst // MOE_CHUNK).astype(F32), ((0, 0), (0, LANES - N_EXPERTS)))
    flat = lambda a: a.reshape(-1).astype(I32)
    plan = dict(gofs=flat(gofs), lst=flat(lst), nch=flat(nch), ntot=flat(jnp.sum(nch, axis=1)),
                tail=flat(pstart + tot), tailn=flat(tailn), tailtot=flat(jnp.sum(tailn)),
                blk_e=blk_e, blk_src=flat(blk_src), blk_n=flat(blk_n))
    return plan, jnp.broadcast_to(lrow[:, None, :], (n_tiles, SUBLANES, LANES))


def _tile(n, pref):
    t = min(n, pref)
    assert n % t == 0, (n, pref)
    return t


def kernel(x, mem, positions, norm_mix, w_in, conv_w, conv_b, lru_wa, lru_ba, lru_wx, lru_bx, lru_lambda, w_lru_o, q_norm, k_norm, lambda_q1, lambda_k1, lambda_q2, lambda_k2, subln, w_attn_o, w_out, norm_cx, norm_mem, w_cq, w_ckv, cq_norm, ck_norm, w_co, norm_ffn, w_group, b_group, w_router, b_router, w_gate_up, w_down):
    bsz, s, d = x.shape
    assert d == D_MODEL and w_in.shape[-1] == N_PROJ * D_MODEL
    n = bsz * s
    depth = w_in.shape[0]
    tm = _tile(n, 512)
    ts = _tile(s, 512)
    tq = _tile(s, 512)
    assert tq % CHUNK == 0 and ts % SUBLANES == 0
    eb = _tile(n, 512)
    rope = _rope_table(positions)
    row = lambda v: v.reshape(1, -1).astype(F32)
    rep = LANES // HEAD_DIM

    for layer in range(depth):
        lambda_init = 0.8 - 0.6 * math.exp(-0.3 * layer)
        wax = (0.5 * jnp.concatenate([lru_wa[layer], lru_wx[layer]], axis=-1)).astype(BF16)
        bax = 0.5 * jnp.stack([lru_ba[layer], lru_bx[layer]]).astype(F32)
        lru_params = (conv_w[layer].astype(F32), row(conv_b[layer]), wax, bax, row(lru_lambda[layer]),
                      w_lru_o[layer].astype(BF16))
        ml, q, k, vt, ga = _in_proj(
            x.reshape(n, d), row(norm_mix[layer]), w_in[layer].astype(BF16),
            jnp.tile(row(q_norm[layer]), (1, rep)), jnp.tile(row(k_norm[layer]), (1, rep)), rope, lru_params, ts, s)
        seq = lambda a: a.reshape(bsz, s, d)

        lam = (jnp.exp(jnp.sum(lambda_q1[layer].astype(F32) * lambda_k1[layer].astype(F32)))
               - jnp.exp(jnp.sum(lambda_q2[layer].astype(F32) * lambda_k2[layer].astype(F32))) + lambda_init)
        o = _diff_attn(lam.reshape(1, 1), seq(q), seq(k), vt, subln[layer].astype(F32).reshape(V_DIM, 1),
                       lambda_init, tq)

        kc, vc = _mem_kv(mem, row(norm_mem[layer]), w_ckv[layer].astype(BF16), row(ck_norm[layer]))
        w_r = jnp.concatenate([w_group[layer], w_router[layer],
                               jnp.zeros((d, LANES - N_GROUPS - N_EXPERTS), F32)], axis=1).astype(F32)
        b_r = jnp.concatenate([b_group[layer], b_router[layer],
                               jnp.zeros((LANES - N_GROUPS - N_EXPERTS,), F32)]).reshape(1, LANES).astype(F32)
        wrh, wrl = _split_bf16(w_r)
        x2, hn, eid, ew, hist = _cross_router(
            x, o, seq(ga), seq(ml), w_attn_o[layer].astype(BF16), w_out[layer].astype(BF16),
            row(norm_cx[layer]), w_cq[layer].astype(BF16), row(cq_norm[layer]), kc, vc,
            w_co[layer].astype(BF16), row(norm_ffn[layer]), wrh, wrl, b_r, ts)

        p_rows = _plan_rows(n, ts, eb)
        plan, lrow = _moe_plan(hist[:, :, 0, :N_EXPERTS].reshape(n // ts, N_EXPERTS), p_rows, eb)
        eid2d = eid.reshape(n, LANES)
        xs = _dispatch(plan, hn.reshape(n, d), eid2d, lrow, p_rows, ts)
        ys = _experts(plan, xs, w_gate_up[layer], w_down[layer], eb)
        x = _combine(plan, x2.reshape(n, d), ew.reshape(n, LANES), eid2d, lrow, ys, ts).reshape(bsz, s, d)
    return x
```

```python
import functools
import math

import jax
import jax.numpy as jnp
from jax import lax
from jax.experimental import pallas as pl
from jax.experimental.pallas import tpu as pltpu

F32 = jnp.float32
BF16 = jnp.bfloat16
I32 = jnp.int32

D_MODEL = 1024
CHUNK = 64
LRU_BLOCKS = 8
LRU_BLOCK_WIDTH = D_MODEL // LRU_BLOCKS
CONV_WIDTH = 4
LRU_C = 8.0
ATTN_HEADS = 8
HEAD_DIM = 64
V_DIM = 2 * HEAD_DIM
ROPE_THETA = 500000.0
ROT_DIMS = HEAD_DIM // 4
ROT_HALF = ROT_DIMS // 2
MEM_HEADS = 4
MEM_HEAD_DIM = 128
MEM_WIDTH = MEM_HEADS * MEM_HEAD_DIM
N_GROUPS = 4
EXPERTS_PER_GROUP = 8
N_EXPERTS = N_GROUPS * EXPERTS_PER_GROUP
TOP_K = 2
EXPERT_FF = 512
N_PROJ = 7
EPS = 1e-6
NEG_INF = -1e30
LOG2_E = math.log2(math.e)
ATTN_COL_GROUP = 512
ATTN_HEADS_PER_STEP = 4

LANES = 128
SUBLANES = 8
V7X_VMEM_BYTES = 64 * 1024 * 1024
MIB = 1024 * 1024


def _vmem_limit(estimate_bytes):
    return int(min(max(estimate_bytes * 3 // 2, 16 * MIB), V7X_VMEM_BYTES - 8 * MIB))


def _params(semantics, vmem_estimate):
    return pltpu.CompilerParams(dimension_semantics=semantics, vmem_limit_bytes=_vmem_limit(vmem_estimate))


def _resident(shape, index_map):
    return pl.BlockSpec(shape, index_map, pipeline_mode=pl.Buffered(1))


def _rms(x, g):
    return x * lax.rsqrt(jnp.mean(x * x, axis=-1, keepdims=True) + EPS) * g


def _sigmoid(x):
    return 0.5 * jnp.tanh(0.5 * x) + 0.5


def _segment_ones():
    r = lax.broadcasted_iota(I32, (LANES, LANES), 0) // HEAD_DIM
    c = lax.broadcasted_iota(I32, (LANES, LANES), 1) // HEAD_DIM
    return (r == c).astype(BF16)


def _qk_post(p, gain, cos_t, sin_lo, sin_hi, seg, scale):
    cols = []
    for c in range(D_MODEL // LANES):
        pc = p[:, c * LANES:(c + 1) * LANES]
        ss = jnp.dot((pc * pc).astype(BF16), seg, preferred_element_type=F32)
        y = pc * lax.rsqrt(ss * (1.0 / HEAD_DIM) + EPS) * gain
        y = y * cos_t + pltpu.roll(y, LANES - ROT_HALF, 1) * sin_lo + pltpu.roll(y, ROT_HALF, 1) * sin_hi
        cols.append((y * scale).astype(BF16))
    return jnp.concatenate(cols, axis=1)


def _in_proj_kernel(x_ref, g_ref, w_ref, qg_ref, kg_ref, rope_ref, cw_ref, cb_ref, wax_ref, bax_ref, lam_ref, wo_ref,
                    ml_ref, q_ref, k_ref, vt_ref, ga_ref, xpad_ref, hprev_ref, *, per_seq):
    @pl.when(pl.program_id(0) % per_seq == 0)
    def _():
        xpad_ref[0:SUBLANES, :] = jnp.zeros((SUBLANES, D_MODEL), F32)
        hprev_ref[...] = jnp.zeros_like(hprev_ref)

    h = _rms(x_ref[...], g_ref[...]).astype(BF16)

    def proj(j):
        return jnp.dot(h, w_ref[:, j * D_MODEL:(j + 1) * D_MODEL], preferred_element_type=F32)

    xc, pre_a, pre_x = _lru_conv_gates(proj(0), cw_ref, cb_ref, wax_ref, xpad_ref)
    seg = _segment_ones()
    tab = rope_ref[...]
    seg_lane = lax.broadcasted_iota(I32, tab.shape, 1) % HEAD_DIM
    first, second = seg_lane < ROT_HALF, (seg_lane >= ROT_HALF) & (seg_lane < ROT_DIMS)
    cos_t = jnp.where(first, tab, jnp.where(second, pltpu.roll(tab, ROT_HALF, 1), 1.0))
    sin_lo = jnp.where(first, -pltpu.roll(tab, LANES - ROT_HALF, 1), 0.0)
    sin_hi = jnp.where(second, tab, 0.0)
    q_ref[...] = _qk_post(proj(2), qg_ref[...], cos_t, sin_lo, sin_hi, seg, HEAD_DIM ** -0.5 * LOG2_E)
    k_ref[...] = _qk_post(proj(3), kg_ref[...], cos_t, sin_lo, sin_hi, seg, 1.0)
    hr = _lru_scan(xc, pre_a, pre_x, bax_ref, lam_ref, hprev_ref)
    gate, merge_gate = proj(1), proj(5)
    v = proj(4)
    for hd in range(ATTN_HEADS):
        vt_ref[hd * V_DIM:(hd + 1) * V_DIM, :] = v[:, hd * V_DIM:(hd + 1) * V_DIM].T.astype(BF16)
    ml_ref[...] = _lru_out(hr, gate, merge_gate, wo_ref)
    ga_ref[...] = proj(6).astype(BF16)


def _in_proj(x2d, g, w_in, qg, kg, rope, lru_params, tm, seq_len):
    n = x2d.shape[0]
    per_seq = seq_len // tm
    row = lambda i: (i, 0)
    tok = pl.BlockSpec((tm, D_MODEL), row)
    tok_out = jax.ShapeDtypeStruct((n, D_MODEL), BF16)
    vt_out = jax.ShapeDtypeStruct((n // seq_len, D_MODEL, seq_len), BF16)
    vt_spec = pl.BlockSpec((None, D_MODEL, tm), lambda i: (i // per_seq, 0, i % per_seq))
    whole = lambda a: _resident(a.shape, lambda i: (0,) * a.ndim)
    consts = (g, w_in, qg, kg)
    est = (sum(a.size * a.dtype.itemsize for a in consts + tuple(lru_params)) + 2 * tm * D_MODEL * 4
           + 2 * tm * LANES * 4 + 5 * 2 * tm * D_MODEL * 2 + 16 * tm * D_MODEL * 4)
    return pl.pallas_call(
        functools.partial(_in_proj_kernel, per_seq=per_seq),
        out_shape=[tok_out] * 3 + [vt_out, tok_out],
        grid=(n // tm,),
        in_specs=[tok] + [whole(a) for a in consts] + [pl.BlockSpec((tm, LANES), row)]
        + [whole(a) for a in lru_params],
        out_specs=[tok] * 3 + [vt_spec, tok],
        scratch_shapes=[pltpu.VMEM((tm + SUBLANES, D_MODEL), F32), pltpu.VMEM((SUBLANES, D_MODEL), F32)],
        compiler_params=_params(("arbitrary",), est),
        name="in_proj_lru",
    )(x2d, *consts, rope, *lru_params)


def _lru_conv_gates(x, cw_ref, cb_ref, wax_ref, xpad_ref):
    tt = x.shape[0]
    xpad_ref[SUBLANES:SUBLANES + tt, :] = x
    cw = cw_ref[...]
    xc = cb_ref[...] + cw[3:4] * x
    for j in range(1, CONV_WIDTH):
        xc = xc + cw[CONV_WIDTH - 1 - j:CONV_WIDTH - j] * xpad_ref[SUBLANES - j:SUBLANES - j + tt, :]
    xpad_ref[0:SUBLANES, :] = x[tt - SUBLANES:tt]

    xcb = xc.astype(BF16)
    ra, ri = [], []
    for n in range(LRU_BLOCKS):
        g = jnp.dot(xcb[:, n * LRU_BLOCK_WIDTH:(n + 1) * LRU_BLOCK_WIDTH], wax_ref[n], preferred_element_type=F32)
        ra.append(g[:, :LRU_BLOCK_WIDTH])
        ri.append(g[:, LRU_BLOCK_WIDTH:])
    return xc, jnp.concatenate(ra, axis=1), jnp.concatenate(ri, axis=1)


def _lru_scan(xc, pre_a, pre_x, bax_ref, lam_ref, hprev_ref):
    nblk = xc.shape[0] // SUBLANES
    bax = bax_ref[...]
    t_r = jnp.tanh(pre_a + bax[0:1])
    i = 0.5 * jnp.tanh(pre_x + bax[1:2]) + 0.5
    u = (0.5 * LRU_C) * jax.nn.softplus(-lam_ref[...]) * (t_r + 1.0)
    a = jnp.exp(-u)
    gain2 = jnp.tanh(u) * (a * a + 1.0)
    b = jnp.where(gain2 > 0.0, gain2 * lax.rsqrt(gain2), 0.0) * i * xc

    a3 = a.reshape(nblk, SUBLANES, D_MODEL)
    b3 = b.reshape(nblk, SUBLANES, D_MODEL)
    sub = lax.broadcasted_iota(I32, (nblk, SUBLANES, D_MODEL), 1)
    shift = 1
    while shift < SUBLANES:
        keep = sub >= shift
        a_sh = pltpu.roll(a3, shift, 1)
        b_sh = pltpu.roll(b3, shift, 1)
        b3 = jnp.where(keep, a3 * b_sh + b3, b3)
        a3 = jnp.where(keep, a3 * a_sh, a3)
        shift *= 2
    h_last = hprev_ref[...]
    groups = []
    for blk in range(nblk):
        hb = a3[blk] * h_last + b3[blk]
        groups.append(hb)
        h_last = jnp.broadcast_to(hb[SUBLANES - 1:SUBLANES], (SUBLANES, D_MODEL))
    hprev_ref[...] = h_last
    return jnp.concatenate(groups, axis=0)


def _lru_out(hr, gate, merge_gate, wo_ref):
    y = (jax.nn.gelu(gate) * hr).astype(BF16)
    yl = jnp.dot(y, wo_ref[...], preferred_element_type=F32)
    return (_sigmoid(merge_gate) * yl).astype(BF16)


def _attn_kernel(lam_ref, q_ref, qnext_ref, k_ref, vt_ref, sub_ref, o_ref, m_ref, l_ref, acc_ref, qz_ref, sa_ref,
                 sb_ref, *, tq, out_scale):
    i = pl.program_id(2)
    last = pl.num_programs(2) - 1
    heads = range(ATTN_HEADS_PER_STEP)
    m_ref[...] = jnp.full_like(m_ref, NEG_INF)
    l_ref[...] = jnp.zeros_like(l_ref)
    acc_ref[...] = jnp.zeros_like(acc_ref)
    cw = ATTN_COL_GROUP

    def load_queries(src_ref):
        row = lax.broadcasted_iota(I32, (V_DIM, tq), 0)
        for h in heads:
            qt = src_ref[:, h * V_DIM:(h + 1) * V_DIM].astype(F32).T
            zero = jnp.zeros_like(qt)
            qz_ref[h] = jnp.concatenate([jnp.where(row < HEAD_DIM, qt, zero), jnp.where(row >= HEAD_DIM, qt, zero)],
                                        axis=1).astype(BF16)

    def block_off(j):
        return pl.multiple_of(j * tq, tq)

    def scores(j, s_ref):
        for h in heads:
            kb = k_ref[pl.ds(block_off(j), tq), h * V_DIM:(h + 1) * V_DIM]
            s_ref[h] = jnp.dot(kb, qz_ref[h], preferred_element_type=F32)

    def softmax_pv(j, s_ref, diagonal):
        for h in heads:
            vtb = vt_ref[h * V_DIM:(h + 1) * V_DIM, pl.ds(block_off(j), tq)]
            for g in range(2 * tq // cw):
                cols = pl.ds(g * cw, cw)
                s = s_ref[h, :, cols]
                if diagonal:
                    key = lax.broadcasted_iota(I32, (tq, cw), 0)
                    qry = (lax.broadcasted_iota(I32, (tq, cw), 1) + g * cw) % tq
                    s = jnp.where((key // CHUNK) <= (qry // CHUNK), s, NEG_INF)
                m_prev = m_ref[h, :, cols]
                m_new = jnp.maximum(m_prev, jnp.max(s, axis=0, keepdims=True))
                alpha = jnp.exp2(m_prev - m_new)
                p = jnp.exp2(s - m_new)
                l_ref[h, :, cols] = alpha * l_ref[h, :, cols] + jnp.sum(p, axis=0, keepdims=True)
                acc_ref[h, :, cols] = alpha * acc_ref[h, :, cols] + jnp.dot(vtb, p.astype(BF16),
                                                                            preferred_element_type=F32)
                m_ref[h, :, cols] = m_new

    @pl.when(i == 0)
    def _():
        load_queries(q_ref)
        scores(0, sa_ref)

    def pair(p, carry):
        j = 2 * p
        scores(j + 1, sb_ref)
        softmax_pv(j, sa_ref, False)
        scores(j + 2, sa_ref)
        softmax_pv(j + 1, sb_ref, False)
        return carry

    lax.fori_loop(0, i // 2, pair, 0)

    @pl.when(i % 2 == 1)
    def _():
        scores(i, sb_ref)
        softmax_pv(i - 1, sa_ref, False)
        softmax_pv(i, sb_ref, True)

    @pl.when(i % 2 == 0)
    def _():
        softmax_pv(i, sa_ref, True)

    def write_output():
        for h in heads:
            o12 = acc_ref[h] * (1.0 / l_ref[h])
            ot = o12[:, :tq] - lam_ref[0, 0] * o12[:, tq:]
            ot = ot * lax.rsqrt(jnp.mean(ot * ot, axis=0, keepdims=True) + EPS) * sub_ref[...] * out_scale
            o_ref[:, h * V_DIM:(h + 1) * V_DIM] = ot.T.astype(BF16)

    @pl.when(i < last)
    def _():
        load_queries(qnext_ref)
        scores(0, sa_ref)
        write_output()

    @pl.when(i == last)
    def _():
        write_output()


def _diff_attn(lam, q, k, vt, sub, lambda_init, tq):
    bsz, s, _ = q.shape
    hps = ATTN_HEADS_PER_STEP
    width = hps * V_DIM
    nq = s // tq
    qspec = pl.BlockSpec((None, tq, width), lambda b, h, i: (b, i, h))
    qnext_spec = pl.BlockSpec((None, tq, width), lambda b, h, i: (b, jnp.minimum(i + 1, nq - 1), h))
    kspec = pl.BlockSpec((None, s, width), lambda b, h, i: (b, 0, h))
    vtspec = pl.BlockSpec((None, width, s), lambda b, h, i: (b, h, 0))
    est = 2 * 2 * s * width * 2 + 4 * tq * width * 2 + hps * (V_DIM * 2 * tq * 6 + 2 * tq * 2 * tq * 4) \
        + 3 * 2 * tq * tq * 4
    return pl.pallas_call(
        functools.partial(_attn_kernel, tq=tq, out_scale=1.0 - lambda_init),
        out_shape=jax.ShapeDtypeStruct((bsz, s, ATTN_HEADS * V_DIM), BF16),
        grid=(bsz, ATTN_HEADS // hps, nq),
        in_specs=[pl.BlockSpec(memory_space=pltpu.SMEM), qspec, qnext_spec, kspec, vtspec,
                  pl.BlockSpec((V_DIM, 1), lambda b, h, i: (0, 0))],
        out_specs=qspec,
        scratch_shapes=[pltpu.VMEM((hps, 1, 2 * tq), F32), pltpu.VMEM((hps, 1, 2 * tq), F32),
                        pltpu.VMEM((hps, V_DIM, 2 * tq), F32), pltpu.VMEM((hps, V_DIM, 2 * tq), BF16),
                        pltpu.VMEM((hps, tq, 2 * tq), F32), pltpu.VMEM((hps, tq, 2 * tq), F32)],
        compiler_params=_params(("parallel", "parallel", "arbitrary"), est),
        name="diff_attn",
    )(lam, q, q, k, vt, sub)


def _mix_out(x, o, ga, ml, wao_ref, wout_ref):
    ya = jnp.dot(o, wao_ref[...], preferred_element_type=F32)
    mixed = ml.astype(F32) + _sigmoid(ga.astype(F32)) * ya
    return x + jnp.dot(mixed.astype(BF16), wout_ref[...], preferred_element_type=F32)


def _mem_kv_kernel(mem_ref, g_ref, w_ref, ckg_ref, k_ref, v_ref):
    h = _rms(mem_ref[...], g_ref[...]).astype(BF16)
    kv = jnp.dot(h, w_ref[...], preferred_element_type=F32)
    ks = [_rms(kv[:, hd * MEM_HEAD_DIM:(hd + 1) * MEM_HEAD_DIM], ckg_ref[...]) for hd in range(MEM_HEADS)]
    k_ref[...] = jnp.concatenate(ks, axis=1).astype(BF16)
    v_ref[...] = kv[:, MEM_WIDTH:].astype(BF16)


def _mem_kv(mem, g, w, ckg):
    bsz, m, _ = mem.shape
    fixed = lambda b: (0, 0)
    out = pl.BlockSpec((None, m, MEM_WIDTH), lambda b: (b, 0, 0))
    est = 2 * m * D_MODEL * 4 + w.size * 2 + 4 * m * MEM_WIDTH * 2 + 4 * m * D_MODEL * 4
    return pl.pallas_call(
        _mem_kv_kernel,
        out_shape=[jax.ShapeDtypeStruct((bsz, m, MEM_WIDTH), BF16)] * 2,
        grid=(bsz,),
        in_specs=[pl.BlockSpec((None, m, D_MODEL), lambda b: (b, 0, 0)), _resident((1, D_MODEL), fixed),
                  _resident(w.shape, fixed), _resident((1, MEM_HEAD_DIM), fixed)],
        out_specs=[out, out],
        compiler_params=_params(("parallel",), est),
        name="mem_kv",
    )(mem, g, w, ckg)


def _split_bf16(x):
    hi = x.astype(BF16)
    return hi, (x - hi.astype(F32)).astype(BF16)


def _cross_router_kernel(x_ref, o_ref, ga_ref, ml_ref, wao_ref, wout_ref, gcx_ref, wcq_ref, cqg_ref, kc_ref, vc_ref,
                         wco_ref, gffn_ref, wrh_ref, wrl_ref, br_ref, x2_ref, hn_ref, eid_ref, ew_ref, hist_ref):
    x1 = _mix_out(x_ref[...], o_ref[...], ga_ref[...], ml_ref[...], wao_ref, wout_ref)
    q = jnp.dot(_rms(x1, gcx_ref[...]).astype(BF16), wcq_ref[...], preferred_element_type=F32)
    outs = []
    for hd in range(MEM_HEADS):
        sl = slice(hd * MEM_HEAD_DIM, (hd + 1) * MEM_HEAD_DIM)
        qh = _rms(q[:, sl], cqg_ref[...]) * MEM_HEAD_DIM ** -0.5
        s = lax.dot_general(qh.astype(BF16), kc_ref[:, sl], (((1,), (1,)), ((), ())), preferred_element_type=F32)
        p = jnp.exp(s - jnp.max(s, axis=1, keepdims=True))
        o = jnp.dot(p.astype(BF16), vc_ref[:, sl], preferred_element_type=F32)
        outs.append(o / jnp.sum(p, axis=1, keepdims=True))
    x2 = x1 + jnp.dot(jnp.concatenate(outs, axis=1).astype(BF16), wco_ref[...], preferred_element_type=F32)
    x2_ref[...] = x2

    hn = _rms(x2, gffn_ref[...])
    hn_ref[...] = hn.astype(BF16)
    h_hi, h_lo = _split_bf16(hn)
    logits = (jnp.dot(h_hi, wrh_ref[...], preferred_element_type=F32)
              + jnp.dot(h_lo, wrh_ref[...], preferred_element_type=F32)
              + jnp.dot(h_hi, wrl_ref[...], preferred_element_type=F32)) + br_ref[...]
    lane = lax.broadcasted_iota(I32, logits.shape, 1)
    is_group = lane < N_GROUPS
    gl = jnp.where(is_group, logits, NEG_INF)
    gmax = jnp.max(gl, axis=1, keepdims=True)
    gval = 1.0 / jnp.sum(jnp.where(is_group, jnp.exp(gl - gmax), 0.0), axis=1, keepdims=True)
    gidx = jnp.min(jnp.where(gl == gmax, lane, LANES), axis=1, keepdims=True)
    lane_group = lax.shift_right_logical(lane + (EXPERTS_PER_GROUP - N_GROUPS), 3) - 1
    chosen = lane_group == gidx
    el = jnp.where(chosen, logits, NEG_INF)
    v1 = jnp.max(el, axis=1, keepdims=True)
    i1 = jnp.min(jnp.where(chosen & (el == v1), lane, LANES), axis=1, keepdims=True)
    rest = chosen & (lane != i1)
    el2 = jnp.where(rest, logits, NEG_INF)
    v2 = jnp.max(el2, axis=1, keepdims=True)
    i2 = jnp.min(jnp.where(rest & (el2 == v2), lane, LANES), axis=1, keepdims=True)
    t = jnp.exp(v2 - v1)
    w1 = gval / (1.0 + t)
    w2 = gval * t / (1.0 + t)
    e1, e2 = i1 - N_GROUPS, i2 - N_GROUPS
    eid_ref[...] = jnp.where(lane == 0, e1, jnp.where(lane == 1, e2, 0))
    ew_ref[...] = jnp.where(lane == 0, w1, jnp.where(lane == 1, w2, 0.0))
    count = jnp.sum((lane == e1).astype(I32) + (lane == e2).astype(I32), axis=0, keepdims=True)
    hist_ref[...] = jnp.broadcast_to(count, hist_ref.shape)


def _cross_router(x, o, ga, ml, wao, wout, gcx, wcq, cqg, kc, vc, wco, gffn, wrh, wrl, br, tm):
    bsz, s, _ = x.shape
    m = kc.shape[1]
    tok = pl.BlockSpec((None, tm, D_MODEL), lambda b, i: (b, i, 0))
    small = pl.BlockSpec((None, tm, LANES), lambda b, i: (b, i, 0))
    memb = pl.BlockSpec((None, m, MEM_WIDTH), lambda b, i: (b, 0, 0))
    fixed = lambda b, i: (0, 0)
    est = (3 * 2 * tm * D_MODEL * 4 + 4 * 2 * tm * D_MODEL * 2 + 2 * 2 * tm * LANES * 4 + 2 * 2 * m * MEM_WIDTH * 2
           + 2 * D_MODEL * D_MODEL * 2 + 2 * D_MODEL * MEM_WIDTH * 2 + 2 * D_MODEL * LANES * 2 + 8 * tm * D_MODEL * 4)
    return pl.pallas_call(
        _cross_router_kernel,
        out_shape=[jax.ShapeDtypeStruct((bsz, s, D_MODEL), F32), jax.ShapeDtypeStruct((bsz, s, D_MODEL), BF16),
                   jax.ShapeDtypeStruct((bsz, s, LANES), I32), jax.ShapeDtypeStruct((bsz, s, LANES), F32),
                   jax.ShapeDtypeStruct((bsz, s // tm, SUBLANES, LANES), I32)],
        grid=(bsz, s // tm),
        in_specs=[tok, tok, tok, tok, _resident(wao.shape, fixed), _resident(wout.shape, fixed),
                  _resident((1, D_MODEL), fixed), _resident(wcq.shape, fixed), _resident((1, MEM_HEAD_DIM), fixed),
                  memb, memb, _resident(wco.shape, fixed), _resident((1, D_MODEL), fixed),
                  _resident(wrh.shape, fixed), _resident(wrl.shape, fixed), _resident((1, LANES), fixed)],
        out_specs=[tok, tok, small, small,
                   pl.BlockSpec((None, None, SUBLANES, LANES), lambda b, i: (b, i, 0, 0))],
        compiler_params=_params(("parallel", "parallel"), est),
        name="mix_cross_router",
    )(x, o, ga, ml, wao, wout, gcx, wcq, cqg, kc, vc, wco, gffn, wrh, wrl, br)


MOE_CHUNK = SUBLANES
PACKED = D_MODEL // 2
U32 = jnp.uint32


def _pack_rows(x):
    hi = lax.bitcast_convert_type(x[:, :PACKED].astype(BF16).astype(F32), U32)
    lo = lax.bitcast_convert_type(x[:, PACKED:].astype(BF16).astype(F32), U32)
    return hi | lax.shift_right_logical(lo, jnp.full_like(lo, 16))


def _unpack_rows(p):
    a = lax.bitcast_convert_type(p & jnp.full_like(p, 0xFFFF0000), F32)
    b = lax.bitcast_convert_type(lax.shift_left(p, jnp.full_like(p, 16)), F32)
    return jnp.concatenate([a, b], axis=1).astype(BF16)


def _local_rows(tm):
    return TOP_K * tm + N_EXPERTS * MOE_CHUNK


def _segment_loop(t, lst_ref, gofs_ref, nch_ref, fn):
    def per_expert(e, carry):
        k = t * N_EXPERTS + e
        lst, gofs = lst_ref[k], gofs_ref[k]

        def per_chunk(c, cc):
            fn(pl.multiple_of(lst + c * MOE_CHUNK, MOE_CHUNK), pl.multiple_of(gofs + c * MOE_CHUNK, MOE_CHUNK))
            return cc
        lax.fori_loop(0, nch_ref[k], per_chunk, 0)
        return carry
    lax.fori_loop(0, N_EXPERTS, per_expert, 0)


def _repeat(count, fn):
    def body(c, carry):
        fn()
        return carry
    lax.fori_loop(0, count, body, 0)


MOE_WAIT_GROUP = 8


def _wait_chunks(count, wait_rows):
    _repeat(lax.shift_right_logical(count, 3), lambda: wait_rows(MOE_WAIT_GROUP * MOE_CHUNK))
    _repeat(count & (MOE_WAIT_GROUP - 1), lambda: wait_rows(MOE_CHUNK))


def _dispatch_kernel(gofs_ref, lst_ref, nch_ref, ntot_ref, tail_ref, tailn_ref, tailtot_ref,
                     hn_ref, eid_ref, lrow_ref, xs_hbm, before_ref, xloc_ref, zero_ref, sem, *, tm):
    t = pl.program_id(0)
    last = pl.num_programs(0) - 1
    slot = t % 2
    na = TOP_K * tm
    loc = xloc_ref.shape[1]

    @pl.when(t == 0)
    def _():
        r = lax.broadcasted_iota(I32, (na, na), 0)
        c = lax.broadcasted_iota(I32, (na, na), 1)
        before_ref[...] = (r < c).astype(BF16)
        zero_ref[...] = jnp.zeros_like(zero_ref)

    et = eid_ref[...].astype(F32).T
    e_row = jnp.concatenate([et[0:1], et[1:2]], axis=1)
    hit = lax.broadcasted_iota(I32, (LANES, na), 0).astype(F32) == e_row
    hit_b = hit.astype(BF16)
    rank = jnp.dot(hit_b, before_ref[...], preferred_element_type=F32)
    start = jnp.dot(lrow_ref[...].astype(BF16), hit_b, preferred_element_type=F32)[0:1] * MOE_CHUNK
    pos = (start + jnp.sum(jnp.where(hit, rank, 0.0), axis=0, keepdims=True)).astype(I32)
    r = lax.broadcasted_iota(I32, (loc, tm), 0)
    sel = ((r == pos[:, :tm]) | (r == pos[:, tm:])).astype(BF16)
    xloc_ref[slot] = _pack_rows(jnp.dot(sel, hn_ref[...], preferred_element_type=F32))

    def copy_out(local_row, global_row, s, rows=MOE_CHUNK):
        return pltpu.make_async_copy(xloc_ref.at[s, pl.ds(local_row, rows), :],
                                     xs_hbm.at[pl.ds(global_row, rows), :], sem.at[s])

    _segment_loop(t, lst_ref, gofs_ref, nch_ref, lambda lr, gr: copy_out(lr, gr, slot).start())

    @pl.when(t > 0)
    def _():
        _wait_chunks(ntot_ref[jnp.maximum(t - 1, 0)], lambda rows: copy_out(0, 0, 1 - slot, rows).wait())

    @pl.when(t == last)
    def _():
        _wait_chunks(ntot_ref[t], lambda rows: copy_out(0, 0, slot, rows).wait())

        def zero_out(global_row):
            return pltpu.make_async_copy(zero_ref, xs_hbm.at[pl.ds(global_row, MOE_CHUNK), :], sem.at[0])

        def per_expert(e, carry):
            def per_chunk(c, cc):
                zero_out(pl.multiple_of(tail_ref[e] + c * MOE_CHUNK, MOE_CHUNK)).start()
                return cc
            lax.fori_loop(0, tailn_ref[e], per_chunk, 0)
            return carry
        lax.fori_loop(0, N_EXPERTS, per_expert, 0)
        _repeat(tailtot_ref[0], lambda: zero_out(0).wait())


def _dispatch(plan, hn2d, eid2d, lrow, p_rows, tm):
    n = hn2d.shape[0]
    na = TOP_K * tm
    loc = _local_rows(tm)
    pre = (plan["gofs"], plan["lst"], plan["nch"], plan["ntot"], plan["tail"], plan["tailn"], plan["tailtot"])
    est = 2 * tm * D_MODEL * 2 + na * na * 2 + 2 * loc * PACKED * 4 + loc * D_MODEL * 8 + loc * tm * 4 \
        + 4 * LANES * na * 4
    grid_spec = pltpu.PrefetchScalarGridSpec(
        num_scalar_prefetch=len(pre),
        grid=(n // tm,),
        in_specs=[pl.BlockSpec((tm, D_MODEL), lambda t, *_: (t, 0)), pl.BlockSpec((tm, LANES), lambda t, *_: (t, 0)),
                  pl.BlockSpec((None, SUBLANES, LANES), lambda t, *_: (t, 0, 0))],
        out_specs=pl.BlockSpec(memory_space=pl.ANY),
        scratch_shapes=[pltpu.VMEM((na, na), BF16), pltpu.VMEM((2, loc, PACKED), U32),
                        pltpu.VMEM((MOE_CHUNK, PACKED), U32), pltpu.SemaphoreType.DMA((2,))],
    )
    return pl.pallas_call(
        functools.partial(_dispatch_kernel, tm=tm),
        out_shape=jax.ShapeDtypeStruct((p_rows, PACKED), U32),
        grid_spec=grid_spec,
        compiler_params=_params(("arbitrary",), est),
        name="moe_dispatch",
    )(*pre, hn2d, eid2d, lrow)


def _expert_kernel(blk_e_ref, blk_src_ref, blk_n_ref, xs_ref, wgu_ref, wd_ref, ys_ref, wgu_bf, wd_bf):
    i = pl.program_id(0)

    @pl.when(blk_n_ref[i] > 0)
    def _():
        @pl.when((i == 0) | (blk_e_ref[i] != blk_e_ref[jnp.maximum(i - 1, 0)]))
        def _():
            wgu_bf[...] = wgu_ref[0].astype(BF16)
            wd_bf[...] = wd_ref[0].astype(BF16)

        gu = jnp.dot(_unpack_rows(xs_ref[...]), wgu_bf[...], preferred_element_type=F32)
        act = (jax.nn.silu(gu[:, :EXPERT_FF]) * gu[:, EXPERT_FF:]).astype(BF16)
        ys_ref[...] = _pack_rows(jnp.dot(act, wd_bf[...], preferred_element_type=F32))


def _experts(plan, xs, wgu, wd, eb):
    n_blk = xs.shape[0] // eb
    rows = pl.BlockSpec((eb, PACKED), lambda i, be, bs, bn: (bs[i], 0))
    w_elems = wgu.shape[1] * wgu.shape[2] + wd.shape[1] * wd.shape[2]
    est = 2 * 2 * eb * PACKED * 4 + 2 * w_elems * 4 + w_elems * 2 + 6 * eb * D_MODEL * 4
    grid_spec = pltpu.PrefetchScalarGridSpec(
        num_scalar_prefetch=3,
        grid=(n_blk,),
        in_specs=[rows, pl.BlockSpec((1,) + wgu.shape[1:], lambda i, be, bs, bn: (be[i], 0, 0)),
                  pl.BlockSpec((1,) + wd.shape[1:], lambda i, be, bs, bn: (be[i], 0, 0))],
        out_specs=rows,
        scratch_shapes=[pltpu.VMEM(wgu.shape[1:], BF16), pltpu.VMEM(wd.shape[1:], BF16)],
    )
    return pl.pallas_call(
        _expert_kernel,
        out_shape=jax.ShapeDtypeStruct(xs.shape, U32),
        grid_spec=grid_spec,
        compiler_params=_params(("arbitrary",), est),
        name="experts",
    )(plan["blk_e"], plan["blk_src"], plan["blk_n"], xs, wgu, wd)


def _combine_kernel(gofs_ref, lst_ref, nch_ref, ntot_ref, x2_ref, ew_ref, eid_ref, lrow_ref, ys_hbm, o_ref,
                    before_ref, yloc_ref, sem, *, tm):
    t = pl.program_id(0)
    last = pl.num_programs(0) - 1
    slot = t % 2
    na = TOP_K * tm
    loc = yloc_ref.shape[1]

    def copy_in(local_row, global_row, s, rows=MOE_CHUNK):
        return pltpu.make_async_copy(ys_hbm.at[pl.ds(global_row, rows), :],
                                     yloc_ref.at[s, pl.ds(local_row, rows), :], sem.at[s])

    def fetch(tile, s):
        _segment_loop(tile, lst_ref, gofs_ref, nch_ref, lambda lr, gr: copy_in(lr, gr, s).start())

    @pl.when(t == 0)
    def _():
        r = lax.broadcasted_iota(I32, (na, na), 0)
        c = lax.broadcasted_iota(I32, (na, na), 1)
        before_ref[...] = (c < r).astype(BF16)
        yloc_ref[...] = jnp.zeros_like(yloc_ref)
        fetch(0, 0)

    @pl.when(t < last)
    def _():
        fetch(t + 1, 1 - slot)

    _wait_chunks(ntot_ref[t], lambda rows: copy_in(0, 0, slot, rows).wait())

    eid = eid_ref[...]
    lane = lax.broadcasted_iota(I32, (tm, LANES), 1)
    hit = jnp.concatenate([lane == eid[:, 0:1], lane == eid[:, 1:2]], axis=0)
    rank = jnp.dot(before_ref[...], hit.astype(BF16), preferred_element_type=F32)
    start = lrow_ref[0:1, :] * MOE_CHUNK
    pos = jnp.sum(jnp.where(hit, rank + start, 0.0), axis=1, keepdims=True).astype(I32)
    col = lax.broadcasted_iota(I32, (tm, loc), 1)
    ew = ew_ref[...]
    sel = jnp.where(col == pos[:tm], ew[:, 0:1], 0.0) + jnp.where(col == pos[tm:], ew[:, 1:2], 0.0)
    o_ref[...] = x2_ref[...] + jnp.dot(sel.astype(BF16), _unpack_rows(yloc_ref[slot]), preferred_element_type=F32)


def _combine(plan, x2, ew, eid2d, lrow, ys, tm):
    n = x2.shape[0]
    na = TOP_K * tm
    loc = _local_rows(tm)
    pre = (plan["gofs"], plan["lst"], plan["nch"], plan["ntot"])
    tok = pl.BlockSpec((tm, D_MODEL), lambda t, *_: (t, 0))
    small = pl.BlockSpec((tm, LANES), lambda t, *_: (t, 0))
    est = 2 * 2 * tm * D_MODEL * 4 + na * na * 2 + 2 * loc * PACKED * 4 + loc * D_MODEL * 10 + 2 * tm * loc * 4 \
        + 3 * tm * D_MODEL * 4
    grid_spec = pltpu.PrefetchScalarGridSpec(
        num_scalar_prefetch=len(pre),
        grid=(n // tm,),
        in_specs=[tok, small, small, pl.BlockSpec((None, SUBLANES, LANES), lambda t, *_: (t, 0, 0)),
                  pl.BlockSpec(memory_space=pl.ANY)],
        out_specs=tok,
        scratch_shapes=[pltpu.VMEM((na, na), BF16), pltpu.VMEM((2, loc, PACKED), U32),
                        pltpu.SemaphoreType.DMA((2,))],
    )
    return pl.pallas_call(
        functools.partial(_combine_kernel, tm=tm),
        out_shape=jax.ShapeDtypeStruct((n, D_MODEL), F32),
        grid_spec=grid_spec,
        compiler_params=_params(("arbitrary",), est),
        name="moe_combine",
    )(*pre, x2, ew, eid2d, lrow, ys)


def _rope_table(positions):
    inv_freq = jnp.exp(-math.log(ROPE_THETA) * jnp.arange(ROT_HALF, dtype=F32) / ROT_HALF)
    ang = positions.astype(F32).reshape(-1, 1) * inv_freq
    seg = jnp.concatenate([jnp.cos(ang), jnp.sin(ang), jnp.zeros((ang.shape[0], HEAD_DIM - ROT_DIMS), F32)], axis=1)
    return jnp.tile(seg, (1, LANES // HEAD_DIM))


def _plan_rows(n_tok, tm, eb):
    worst = n_tok * TOP_K + (n_tok // tm) * N_EXPERTS * (MOE_CHUNK - 1)
    return (worst + eb - 1) // eb * eb + N_EXPERTS * eb


def _moe_plan(hist, p_rows, eb):
    n_tiles = hist.shape[0]
    seg = (hist + MOE_CHUNK - 1) // MOE_CHUNK * MOE_CHUNK
    tot = jnp.sum(seg, axis=0)
    region = (tot + eb - 1) // eb * eb
    pend = jnp.cumsum(region)
    pstart = pend - region
    gofs = pstart[None, :] + jnp.cumsum(seg, axis=0) - seg
    lst = jnp.cumsum(seg, axis=1) - seg
    nch = seg // MOE_CHUNK
    n_blk = p_rows // eb
    blk_first = jnp.arange(n_blk, dtype=I32) * eb
    blk_e = jnp.minimum(jnp.sum(blk_first[:, None] >= pend[None, :], axis=1), N_EXPERTS - 1).astype(I32)
    blk_n = jnp.clip(pstart[blk_e] + tot[blk_e] - blk_first, 0, eb)
    blk_src = jnp.minimum(jnp.arange(n_blk, dtype=I32), jnp.maximum(pend[-1] // eb - 1, 0))
    tailn = (region - tot) // MOE_CHUNK
    lrow = jnp.pad((lst // MOE_CHUNK).astype(F32), ((0, 0), (0, LANES - N_EXPERTS)))
    flat = lambda a: a.reshape(-1).astype(I32)
    plan = dict(gofs=flat(gofs), lst=flat(lst), nch=flat(nch), ntot=flat(jnp.sum(nch, axis=1)),
                tail=flat(pstart + tot), tailn=flat(tailn), tailtot=flat(jnp.sum(tailn)),
                blk_e=blk_e, blk_src=flat(blk_src), blk_n=flat(blk_n))
    return plan, jnp.broadcast_to(lrow[:, None, :], (n_tiles, SUBLANES, LANES))


def _tile(n, pref):
    t = min(n, pref)
    assert n % t == 0, (n, pref)
    return t


def kernel(x, mem, positions, norm_mix, w_in, conv_w, conv_b, lru_wa, lru_ba, lru_wx, lru_bx, lru_lambda, w_lru_o, q_norm, k_norm, lambda_q1, lambda_k1, lambda_q2, lambda_k2, subln, w_attn_o, w_out, norm_cx, norm_mem, w_cq, w_ckv, cq_norm, ck_norm, w_co, norm_ffn, w_group, b_group, w_router, b_router, w_gate_up, w_down):
    bsz, s, d = x.shape
    assert d == D_MODEL and w_in.shape[-1] == N_PROJ * D_MODEL
    n = bsz * s
    depth = w_in.shape[0]
    tm = _tile(n, 512)
    ts = _tile(s, 512)
    tq = _tile(s, 512)
    assert tq % CHUNK == 0 and ts % SUBLANES == 0
    eb = _tile(n, 512)
    rope = _rope_table(positions)
    row = lambda v: v.reshape(1, -1).astype(F32)
    rep = LANES // HEAD_DIM

    for layer in range(depth):
        lambda_init = 0.8 - 0.6 * math.exp(-0.3 * layer)
        wax = (0.5 * jnp.concatenate([lru_wa[layer], lru_wx[layer]], axis=-1)).astype(BF16)
        bax = 0.5 * jnp.stack([lru_ba[layer], lru_bx[layer]]).astype(F32)
        lru_params = (conv_w[layer].astype(F32), row(conv_b[layer]), wax, bax, row(lru_lambda[layer]),
                      w_lru_o[layer].astype(BF16))
        ml, q, k, vt, ga = _in_proj(
            x.reshape(n, d), row(norm_mix[layer]), w_in[layer].astype(BF16),
            jnp.tile(row(q_norm[layer]), (1, rep)), jnp.tile(row(k_norm[layer]), (1, rep)), rope, lru_params, ts, s)
        seq = lambda a: a.reshape(bsz, s, d)

        lam = (jnp.exp(jnp.sum(lambda_q1[layer].astype(F32) * lambda_k1[layer].astype(F32)))
               - jnp.exp(jnp.sum(lambda_q2[layer].astype(F32) * lambda_k2[layer].astype(F32))) + lambda_init)
        o = _diff_attn(lam.reshape(1, 1), seq(q), seq(k), vt, subln[layer].astype(F32).reshape(V_DIM, 1),
                       lambda_init, tq)

        kc, vc = _mem_kv(mem, row(norm_mem[layer]), w_ckv[layer].astype(BF16), row(ck_norm[layer]))
        w_r = jnp.concatenate([w_group[layer], w_router[layer],
                               jnp.zeros((d, LANES - N_GROUPS - N_EXPERTS), F32)], axis=1).astype(F32)
        b_r = jnp.concatenate([b_group[layer], b_router[layer],
                               jnp.zeros((LANES - N_GROUPS - N_EXPERTS,), F32)]).reshape(1, LANES).astype(F32)
        wrh, wrl = _split_bf16(w_r)
        x2, hn, eid, ew, hist = _cross_router(
            x, o, seq(ga), seq(ml), w_attn_o[layer].astype(BF16), w_out[layer].astype(BF16),
            row(norm_cx[layer]), w_cq[layer].astype(BF16), row(cq_norm[layer]), kc, vc,
            w_co[layer].astype(BF16), row(norm_ffn[layer]), wrh, wrl, b_r, ts)

        p_rows = _plan_rows(n, ts, eb)
        plan, lrow = _moe_plan(hist[:, :, 0, :N_EXPERTS].reshape(n // ts, N_EXPERTS), p_rows, eb)
        eid2d = eid.reshape(n, LANES)
        xs = _dispatch(plan, hn.reshape(n, d), eid2d, lrow, p_rows, ts)
        ys = _experts(plan, xs, w_gate_up[layer], w_down[layer], eb)
        x = _combine(plan, x2.reshape(n, d), ew.reshape(n, LANES), eid2d, lrow, ys, ts).reshape(bsz, s, d)
    return x
```

```python
import functools
import math

import jax
import jax.numpy as jnp
from jax import lax
from jax.experimental import pallas as pl
from jax.experimental.pallas import tpu as pltpu

F32 = jnp.float32
BF16 = jnp.bfloat16
I32 = jnp.int32

D_MODEL = 1024
CHUNK = 64
LRU_BLOCKS = 8
LRU_BLOCK_WIDTH = D_MODEL // LRU_BLOCKS
CONV_WIDTH = 4
LRU_C = 8.0
ATTN_HEADS = 8
HEAD_DIM = 64
V_DIM = 2 * HEAD_DIM
ROPE_THETA = 500000.0
ROT_DIMS = HEAD_DIM // 4
ROT_HALF = ROT_DIMS // 2
MEM_HEADS = 4
MEM_HEAD_DIM = 128
MEM_WIDTH = MEM_HEADS * MEM_HEAD_DIM
N_GROUPS = 4
EXPERTS_PER_GROUP = 8
N_EXPERTS = N_GROUPS * EXPERTS_PER_GROUP
TOP_K = 2
EXPERT_FF = 512
N_PROJ = 7
EPS = 1e-6
NEG_INF = -1e30
LOG2_E = math.log2(math.e)
ATTN_COL_GROUP = 512
ATTN_HEADS_PER_STEP = 4

LANES = 128
SUBLANES = 8
V7X_VMEM_BYTES = 64 * 1024 * 1024
MIB = 1024 * 1024


def _vmem_limit(estimate_bytes):
    return int(min(max(estimate_bytes * 3 // 2, 16 * MIB), V7X_VMEM_BYTES - 8 * MIB))


def _params(semantics, vmem_estimate):
    return pltpu.CompilerParams(dimension_semantics=semantics, vmem_limit_bytes=_vmem_limit(vmem_estimate))


def _resident(shape, index_map):
    return pl.BlockSpec(shape, index_map, pipeline_mode=pl.Buffered(1))


def _rms(x, g):
    return x * lax.rsqrt(jnp.mean(x * x, axis=-1, keepdims=True) + EPS) * g


def _sigmoid(x):
    return 0.5 * jnp.tanh(0.5 * x) + 0.5


def _segment_ones():
    r = lax.broadcasted_iota(I32, (LANES, LANES), 0) // HEAD_DIM
    c = lax.broadcasted_iota(I32, (LANES, LANES), 1) // HEAD_DIM
    return (r == c).astype(BF16)


def _qk_post(p, gain, cos_t, sin_lo, sin_hi, seg, scale):
    cols = []
    for c in range(D_MODEL // LANES):
        pc = p[:, c * LANES:(c + 1) * LANES]
        ss = jnp.dot((pc * pc).astype(BF16), seg, preferred_element_type=F32)
        y = pc * lax.rsqrt(ss * (1.0 / HEAD_DIM) + EPS) * gain
        y = y * cos_t + pltpu.roll(y, LANES - ROT_HALF, 1) * sin_lo + pltpu.roll(y, ROT_HALF, 1) * sin_hi
        cols.append((y * scale).astype(BF16))
    return jnp.concatenate(cols, axis=1)


def _in_proj_kernel(x_ref, g_ref, w_ref, qg_ref, kg_ref, rope_ref, cw_ref, cb_ref, wax_ref, bax_ref, lam_ref, wo_ref,
                    ml_ref, q_ref, k_ref, vt_ref, ga_ref, xpad_ref, hprev_ref, *, per_seq):
    @pl.when(pl.program_id(0) % per_seq == 0)
    def _():
        xpad_ref[0:SUBLANES, :] = jnp.zeros((SUBLANES, D_MODEL), F32)
        hprev_ref[...] = jnp.zeros_like(hprev_ref)

    h = _rms(x_ref[...], g_ref[...]).astype(BF16)

    def proj(j):
        return jnp.dot(h, w_ref[:, j * D_MODEL:(j + 1) * D_MODEL], preferred_element_type=F32)

    xc, pre_a, pre_x = _lru_conv_gates(proj(0), cw_ref, cb_ref, wax_ref, xpad_ref)
    seg = _segment_ones()
    tab = rope_ref[...]
    seg_lane = lax.broadcasted_iota(I32, tab.shape, 1) % HEAD_DIM
    first, second = seg_lane < ROT_HALF, (seg_lane >= ROT_HALF) & (seg_lane < ROT_DIMS)
    cos_t = jnp.where(first, tab, jnp.where(second, pltpu.roll(tab, ROT_HALF, 1), 1.0))
    sin_lo = jnp.where(first, -pltpu.roll(tab, LANES - ROT_HALF, 1), 0.0)
    sin_hi = jnp.where(second, tab, 0.0)
    q_ref[...] = _qk_post(proj(2), qg_ref[...], cos_t, sin_lo, sin_hi, seg, HEAD_DIM ** -0.5 * LOG2_E)
    k_ref[...] = _qk_post(proj(3), kg_ref[...], cos_t, sin_lo, sin_hi, seg, 1.0)
    hr = _lru_scan(xc, pre_a, pre_x, bax_ref, lam_ref, hprev_ref)
    gate, merge_gate = proj(1), proj(5)
    v = proj(4)
    for hd in range(ATTN_HEADS):
        vt_ref[hd * V_DIM:(hd + 1) * V_DIM, :] = v[:, hd * V_DIM:(hd + 1) * V_DIM].T.astype(BF16)
    ml_ref[...] = _lru_out(hr, gate, merge_gate, wo_ref)
    ga_ref[...] = proj(6).astype(BF16)


def _in_proj(x2d, g, w_in, qg, kg, rope, lru_params, tm, seq_len):
    n = x2d.shape[0]
    per_seq = seq_len // tm
    row = lambda i: (i, 0)
    tok = pl.BlockSpec((tm, D_MODEL), row)
    tok_out = jax.ShapeDtypeStruct((n, D_MODEL), BF16)
    vt_out = jax.ShapeDtypeStruct((n // seq_len, D_MODEL, seq_len), BF16)
    vt_spec = pl.BlockSpec((None, D_MODEL, tm), lambda i: (i // per_seq, 0, i % per_seq))
    whole = lambda a: _resident(a.shape, lambda i: (0,) * a.ndim)
    consts = (g, w_in, qg, kg)
    est = (sum(a.size * a.dtype.itemsize for a in consts + tuple(lru_params)) + 2 * tm * D_MODEL * 4
           + 2 * tm * LANES * 4 + 5 * 2 * tm * D_MODEL * 2 + 16 * tm * D_MODEL * 4)
    return pl.pallas_call(
        functools.partial(_in_proj_kernel, per_seq=per_seq),
        out_shape=[tok_out] * 3 + [vt_out, tok_out],
        grid=(n // tm,),
        in_specs=[tok] + [whole(a) for a in consts] + [pl.BlockSpec((tm, LANES), row)]
        + [whole(a) for a in lru_params],
        out_specs=[tok] * 3 + [vt_spec, tok],
        scratch_shapes=[pltpu.VMEM((tm + SUBLANES, D_MODEL), F32), pltpu.VMEM((SUBLANES, D_MODEL), F32)],
        compiler_params=_params(("arbitrary",), est),
        name="in_proj_lru",
    )(x2d, *consts, rope, *lru_params)


def _lru_conv_gates(x, cw_ref, cb_ref, wax_ref, xpad_ref):
    tt = x.shape[0]
    xpad_ref[SUBLANES:SUBLANES + tt, :] = x
    cw = cw_ref[...]
    xc = cb_ref[...] + cw[3:4] * x
    for j in range(1, CONV_WIDTH):
        xc = xc + cw[CONV_WIDTH - 1 - j:CONV_WIDTH - j] * xpad_ref[SUBLANES - j:SUBLANES - j + tt, :]
    xpad_ref[0:SUBLANES, :] = x[tt - SUBLANES:tt]

    xcb = xc.astype(BF16)
    ra, ri = [], []
    for n in range(LRU_BLOCKS):
        g = jnp.dot(xcb[:, n * LRU_BLOCK_WIDTH:(n + 1) * LRU_BLOCK_WIDTH], wax_ref[n], preferred_element_type=F32)
        ra.append(g[:, :LRU_BLOCK_WIDTH])
        ri.append(g[:, LRU_BLOCK_WIDTH:])
    return xc, jnp.concatenate(ra, axis=1), jnp.concatenate(ri, axis=1)


def _lru_scan(xc, pre_a, pre_x, bax_ref, lam_ref, hprev_ref):
    nblk = xc.shape[0] // SUBLANES
    bax = bax_ref[...]
    t_r = jnp.tanh(pre_a + bax[0:1])
    i = 0.5 * jnp.tanh(pre_x + bax[1:2]) + 0.5
    u = (0.5 * LRU_C) * jax.nn.softplus(-lam_ref[...]) * (t_r + 1.0)
    a = jnp.exp(-u)
    gain2 = jnp.tanh(u) * (a * a + 1.0)
    b = jnp.where(gain2 > 0.0, gain2 * lax.rsqrt(gain2), 0.0) * i * xc

    a3 = a.reshape(nblk, SUBLANES, D_MODEL)
    b3 = b.reshape(nblk, SUBLANES, D_MODEL)
    sub = lax.broadcasted_iota(I32, (nblk, SUBLANES, D_MODEL), 1)
    shift = 1
    while shift < SUBLANES:
        keep = sub >= shift
        a_sh = pltpu.roll(a3, shift, 1)
        b_sh = pltpu.roll(b3, shift, 1)
        b3 = jnp.where(keep, a3 * b_sh + b3, b3)
        a3 = jnp.where(keep, a3 * a_sh, a3)
        shift *= 2
    h_last = hprev_ref[...]
    groups = []
    for blk in range(nblk):
        hb = a3[blk] * h_last + b3[blk]
        groups.append(hb)
        h_last = jnp.broadcast_to(hb[SUBLANES - 1:SUBLANES], (SUBLANES, D_MODEL))
    hprev_ref[...] = h_last
    return jnp.concatenate(groups, axis=0)


def _lru_out(hr, gate, merge_gate, wo_ref):
    y = (jax.nn.gelu(gate) * hr).astype(BF16)
    yl = jnp.dot(y, wo_ref[...], preferred_element_type=F32)
    return (_sigmoid(merge_gate) * yl).astype(BF16)


def _attn_kernel(lam_ref, q_ref, qnext_ref, k_ref, vt_ref, sub_ref, o_ref, m_ref, l_ref, acc_ref, qz_ref, sa_ref,
                 sb_ref, *, tq, out_scale):
    i = pl.program_id(2)
    last = pl.num_programs(2) - 1
    heads = range(ATTN_HEADS_PER_STEP)
    m_ref[...] = jnp.full_like(m_ref, NEG_INF)
    l_ref[...] = jnp.zeros_like(l_ref)
    acc_ref[...] = jnp.zeros_like(acc_ref)
    cw = ATTN_COL_GROUP

    def load_queries(src_ref):
        row = lax.broadcasted_iota(I32, (V_DIM, tq), 0)
        for h in heads:
            qt = src_ref[:, h * V_DIM:(h + 1) * V_DIM].astype(F32).T
            zero = jnp.zeros_like(qt)
            qz_ref[h] = jnp.concatenate([jnp.where(row < HEAD_DIM, qt, zero), jnp.where(row >= HEAD_DIM, qt, zero)],
                                        axis=1).astype(BF16)

    def block_off(j):
        return pl.multiple_of(j * tq, tq)

    def scores(j, s_ref):
        for h in heads:
            kb = k_ref[pl.ds(block_off(j), tq), h * V_DIM:(h + 1) * V_DIM]
            s_ref[h] = jnp.dot(kb, qz_ref[h], preferred_element_type=F32)

    def softmax_pv(j, s_ref, diagonal):
        for h in heads:
            vtb = vt_ref[h * V_DIM:(h + 1) * V_DIM, pl.ds(block_off(j), tq)]
            for g in range(2 * tq // cw):
                cols = pl.ds(g * cw, cw)
                s = s_ref[h, :, cols]
                if diagonal:
                    key = lax.broadcasted_iota(I32, (tq, cw), 0)
                    qry = (lax.broadcasted_iota(I32, (tq, cw), 1) + g * cw) % tq
                    s = jnp.where((key // CHUNK) <= (qry // CHUNK), s, NEG_INF)
                m_prev = m_ref[h, :, cols]
                m_new = jnp.maximum(m_prev, jnp.max(s, axis=0, keepdims=True))
                alpha = jnp.exp2(m_prev - m_new)
                p = jnp.exp2(s - m_new)
                l_ref[h, :, cols] = alpha * l_ref[h, :, cols] + jnp.sum(p, axis=0, keepdims=True)
                acc_ref[h, :, cols] = alpha * acc_ref[h, :, cols] + jnp.dot(vtb, p.astype(BF16),
                                                                            preferred_element_type=F32)
                m_ref[h, :, cols] = m_new

    @pl.when(i == 0)
    def _():
        load_queries(q_ref)
        scores(0, sa_ref)

    def pair(p, carry):
        j = 2 * p
        scores(j + 1, sb_ref)
        softmax_pv(j, sa_ref, False)
        scores(j + 2, sa_ref)
        softmax_pv(j + 1, sb_ref, False)
        return carry

    lax.fori_loop(0, i // 2, pair, 0)

    @pl.when(i % 2 == 1)
    def _():
        scores(i, sb_ref)
        softmax_pv(i - 1, sa_ref, False)
        softmax_pv(i, sb_ref, True)

    @pl.when(i % 2 == 0)
    def _():
        softmax_pv(i, sa_ref, True)

    def write_output():
        for h in heads:
            o12 = acc_ref[h] * (1.0 / l_ref[h])
            ot = o12[:, :tq] - lam_ref[0, 0] * o12[:, tq:]
            ot = ot * lax.rsqrt(jnp.mean(ot * ot, axis=0, keepdims=True) + EPS) * sub_ref[...] * out_scale
            o_ref[:, h * V_DIM:(h + 1) * V_DIM] = ot.T.astype(BF16)

    @pl.when(i < last)
    def _():
        load_queries(qnext_ref)
        scores(0, sa_ref)
        write_output()

    @pl.when(i == last)
    def _():
        write_output()


def _diff_attn(lam, q, k, vt, sub, lambda_init, tq):
    bsz, s, _ = q.shape
    hps = ATTN_HEADS_PER_STEP
    width = hps * V_DIM
    nq = s // tq
    qspec = pl.BlockSpec((None, tq, width), lambda b, h, i: (b, i, h))
    qnext_spec = pl.BlockSpec((None, tq, width), lambda b, h, i: (b, jnp.minimum(i + 1, nq - 1), h))
    kspec = pl.BlockSpec((None, s, width), lambda b, h, i: (b, 0, h))
    vtspec = pl.BlockSpec((None, width, s), lambda b, h, i: (b, h, 0))
    est = 2 * 2 * s * width * 2 + 4 * tq * width * 2 + hps * (V_DIM * 2 * tq * 6 + 2 * tq * 2 * tq * 4) \
        + 3 * 2 * tq * tq * 4
    return pl.pallas_call(
        functools.partial(_attn_kernel, tq=tq, out_scale=1.0 - lambda_init),
        out_shape=jax.ShapeDtypeStruct((bsz, s, ATTN_HEADS * V_DIM), BF16),
        grid=(bsz, ATTN_HEADS // hps, nq),
        in_specs=[pl.BlockSpec(memory_space=pltpu.SMEM), qspec, qnext_spec, kspec, vtspec,
                  pl.BlockSpec((V_DIM, 1), lambda b, h, i: (0, 0))],
        out_specs=qspec,
        scratch_shapes=[pltpu.VMEM((hps, 1, 2 * tq), F32), pltpu.VMEM((hps, 1, 2 * tq), F32),
                        pltpu.VMEM((hps, V_DIM, 2 * tq), F32), pltpu.VMEM((hps, V_DIM, 2 * tq), BF16),
                        pltpu.VMEM((hps, tq, 2 * tq), F32), pltpu.VMEM((hps, tq, 2 * tq), F32)],
        compiler_params=_params(("parallel", "parallel", "arbitrary"), est),
        name="diff_attn",
    )(lam, q, q, k, vt, sub)


def _mix_out(x, o, ga, ml, wao_ref, wout_ref):
    ya = jnp.dot(o, wao_ref[...], preferred_element_type=F32)
    mixed = ml.astype(F32) + _sigmoid(ga.astype(F32)) * ya
    return x + jnp.dot(mixed.astype(BF16), wout_ref[...], preferred_element_type=F32)


def _mem_kv_kernel(mem_ref, g_ref, w_ref, ckg_ref, k_ref, v_ref):
    h = _rms(mem_ref[...], g_ref[...]).astype(BF16)
    kv = jnp.dot(h, w_ref[...], preferred_element_type=F32)
    ks = [_rms(kv[:, hd * MEM_HEAD_DIM:(hd + 1) * MEM_HEAD_DIM], ckg_ref[...]) for hd in range(MEM_HEADS)]
    k_ref[...] = jnp.concatenate(ks, axis=1).astype(BF16)
    v_ref[...] = kv[:, MEM_WIDTH:].astype(BF16)


def _mem_kv(mem, g, w, ckg):
    bsz, m, _ = mem.shape
    fixed = lambda b: (0, 0)
    out = pl.BlockSpec((None, m, MEM_WIDTH), lambda b: (b, 0, 0))
    est = 2 * m * D_MODEL * 4 + w.size * 2 + 4 * m * MEM_WIDTH * 2 + 4 * m * D_MODEL * 4
    return pl.pallas_call(
        _mem_kv_kernel,
        out_shape=[jax.ShapeDtypeStruct((bsz, m, MEM_WIDTH), BF16)] * 2,
        grid=(bsz,),
        in_specs=[pl.BlockSpec((None, m, D_MODEL), lambda b: (b, 0, 0)), _resident((1, D_MODEL), fixed),
                  _resident(w.shape, fixed), _resident((1, MEM_HEAD_DIM), fixed)],
        out_specs=[out, out],
        compiler_params=_params(("parallel",), est),
        name="mem_kv",
    )(mem, g, w, ckg)


def _split_bf16(x):
    hi = x.astype(BF16)
    return hi, (x - hi.astype(F32)).astype(BF16)


def _cross_router_kernel(x_ref, o_ref, ga_ref, ml_ref, wao_ref, wout_ref, gcx_ref, wcq_ref, cqg_ref, kc_ref, vc_ref,
                         wco_ref, gffn_ref, wrh_ref, wrl_ref, br_ref, x2_ref, hn_ref, eid_ref, ew_ref, hist_ref):
    x1 = _mix_out(x_ref[...], o_ref[...], ga_ref[...], ml_ref[...], wao_ref, wout_ref)
    q = jnp.dot(_rms(x1, gcx_ref[...]).astype(BF16), wcq_ref[...], preferred_element_type=F32)
    outs = []
    for hd in range(MEM_HEADS):
        sl = slice(hd * MEM_HEAD_DIM, (hd + 1) * MEM_HEAD_DIM)
        qh = _rms(q[:, sl], cqg_ref[...]) * MEM_HEAD_DIM ** -0.5
        s = lax.dot_general(qh.astype(BF16), kc_ref[:, sl], (((1,), (1,)), ((), ())), preferred_element_type=F32)
        p = jnp.exp(s - jnp.max(s, axis=1, keepdims=True))
        o = jnp.dot(p.astype(BF16), vc_ref[:, sl], preferred_element_type=F32)
        outs.append(o / jnp.sum(p, axis=1, keepdims=True))
    x2 = x1 + jnp.dot(jnp.concatenate(outs, axis=1).astype(BF16), wco_ref[...], preferred_element_type=F32)
    x2_ref[...] = x2

    hn = _rms(x2, gffn_ref[...])
    hn_ref[...] = hn.astype(BF16)
    h_hi, h_lo = _split_bf16(hn)
    logits = (jnp.dot(h_hi, wrh_ref[...], preferred_element_type=F32)
              + jnp.dot(h_lo, wrh_ref[...], preferred_element_type=F32)
              + jnp.dot(h_hi, wrl_ref[...], preferred_element_type=F32)) + br_ref[...]
    lane = lax.broadcasted_iota(I32, logits.shape, 1)
    is_group = lane < N_GROUPS
    gl = jnp.where(is_group, logits, NEG_INF)
    gmax = jnp.max(gl, axis=1, keepdims=True)
    gval = 1.0 / jnp.sum(jnp.where(is_group, jnp.exp(gl - gmax), 0.0), axis=1, keepdims=True)
    gidx = jnp.min(jnp.where(gl == gmax, lane, LANES), axis=1, keepdims=True)
    lane_group = lax.shift_right_logical(lane + (EXPERTS_PER_GROUP - N_GROUPS), 3) - 1
    chosen = lane_group == gidx
    el = jnp.where(chosen, logits, NEG_INF)
    v1 = jnp.max(el, axis=1, keepdims=True)
    i1 = jnp.min(jnp.where(chosen & (el == v1), lane, LANES), axis=1, keepdims=True)
    rest = chosen & (lane != i1)
    el2 = jnp.where(rest, logits, NEG_INF)
    v2 = jnp.max(el2, axis=1, keepdims=True)
    i2 = jnp.min(jnp.where(rest & (el2 == v2), lane, LANES), axis=1, keepdims=True)
    t = jnp.exp(v2 - v1)
    w1 = gval / (1.0 + t)
    w2 = gval * t / (1.0 + t)
    e1, e2 = i1 - N_GROUPS, i2 - N_GROUPS
    eid_ref[...] = jnp.where(lane == 0, e1, jnp.where(lane == 1, e2, 0))
    ew_ref[...] = jnp.where(lane == 0, w1, jnp.where(lane == 1, w2, 0.0))
    count = jnp.sum((lane == e1).astype(I32) + (lane == e2).astype(I32), axis=0, keepdims=True)
    hist_ref[...] = jnp.broadcast_to(count, hist_ref.shape)


def _cross_router(x, o, ga, ml, wao, wout, gcx, wcq, cqg, kc, vc, wco, gffn, wrh, wrl, br, tm):
    bsz, s, _ = x.shape
    m = kc.shape[1]
    tok = pl.BlockSpec((None, tm, D_MODEL), lambda b, i: (b, i, 0))
    small = pl.BlockSpec((None, tm, LANES), lambda b, i: (b, i, 0))
    memb = pl.BlockSpec((None, m, MEM_WIDTH), lambda b, i: (b, 0, 0))
    fixed = lambda b, i: (0, 0)
    est = (3 * 2 * tm * D_MODEL * 4 + 4 * 2 * tm * D_MODEL * 2 + 2 * 2 * tm * LANES * 4 + 2 * 2 * m * MEM_WIDTH * 2
           + 2 * D_MODEL * D_MODEL * 2 + 2 * D_MODEL * MEM_WIDTH * 2 + 2 * D_MODEL * LANES * 2 + 8 * tm * D_MODEL * 4)
    return pl.pallas_call(
        _cross_router_kernel,
        out_shape=[jax.ShapeDtypeStruct((bsz, s, D_MODEL), F32), jax.ShapeDtypeStruct((bsz, s, D_MODEL), BF16),
                   jax.ShapeDtypeStruct((bsz, s, LANES), I32), jax.ShapeDtypeStruct((bsz, s, LANES), F32),
                   jax.ShapeDtypeStruct((bsz, s // tm, SUBLANES, LANES), I32)],
        grid=(bsz, s // tm),
        in_specs=[tok, tok, tok, tok, _resident(wao.shape, fixed), _resident(wout.shape, fixed),
                  _resident((1, D_MODEL), fixed), _resident(wcq.shape, fixed), _resident((1, MEM_HEAD_DIM), fixed),
                  memb, memb, _resident(wco.shape, fixed), _resident((1, D_MODEL), fixed),
                  _resident(wrh.shape, fixed), _resident(wrl.shape, fixed), _resident((1, LANES), fixed)],
        out_specs=[tok, tok, small, small,
                   pl.BlockSpec((None, None, SUBLANES, LANES), lambda b, i: (b, i, 0, 0))],
        compiler_params=_params(("parallel", "parallel"), est),
        name="mix_cross_router",
    )(x, o, ga, ml, wao, wout, gcx, wcq, cqg, kc, vc, wco, gffn, wrh, wrl, br)


MOE_CHUNK = SUBLANES
PACKED = D_MODEL // 2
U32 = jnp.uint32


def _pack_rows(x):
    hi = lax.bitcast_convert_type(x[:, :PACKED].astype(BF16).astype(F32), U32)
    lo = lax.bitcast_convert_type(x[:, PACKED:].astype(BF16).astype(F32), U32)
    return hi | lax.shift_right_logical(lo, jnp.full_like(lo, 16))


def _unpack_rows(p):
    a = lax.bitcast_convert_type(p & jnp.full_like(p, 0xFFFF0000), F32)
    b = lax.bitcast_convert_type(lax.shift_left(p, jnp.full_like(p, 16)), F32)
    return jnp.concatenate([a, b], axis=1).astype(BF16)


def _local_rows(tm):
    return TOP_K * tm + N_EXPERTS * MOE_CHUNK


def _segment_loop(t, lst_ref, gofs_ref, nch_ref, fn):
    pair = 2 * MOE_CHUNK

    def per_expert(e, carry):
        k = t * N_EXPERTS + e
        lst, gofs, nch = lst_ref[k], gofs_ref[k], nch_ref[k]
        npair = lax.shift_right_logical(nch, 1)

        def per_pair(c, cc):
            fn(pl.multiple_of(lst + c * pair, MOE_CHUNK), pl.multiple_of(gofs + c * pair, MOE_CHUNK), pair)
            return cc
        lax.fori_loop(0, npair, per_pair, 0)

        @pl.when((nch & 1) == 1)
        def _():
            fn(pl.multiple_of(lst + npair * pair, MOE_CHUNK), pl.multiple_of(gofs + npair * pair, MOE_CHUNK),
               MOE_CHUNK)
        return carry
    lax.fori_loop(0, N_EXPERTS, per_expert, 0)


def _repeat(count, fn):
    def body(c, carry):
        fn()
        return carry
    lax.fori_loop(0, count, body, 0)


MOE_WAIT_GROUP = 8


def _wait_chunks(count, wait_rows):
    _repeat(lax.shift_right_logical(count, 3), lambda: wait_rows(MOE_WAIT_GROUP * MOE_CHUNK))
    _repeat(count & (MOE_WAIT_GROUP - 1), lambda: wait_rows(MOE_CHUNK))


def _dispatch_kernel(gofs_ref, lst_ref, nch_ref, ntot_ref, tail_ref, tailn_ref, tailtot_ref,
                     hn_ref, eid_ref, lrow_ref, xs_hbm, before_ref, xloc_ref, zero_ref, sem, *, tm):
    t = pl.program_id(0)
    last = pl.num_programs(0) - 1
    slot = t % 2
    na = TOP_K * tm
    loc = xloc_ref.shape[1]

    @pl.when(t == 0)
    def _():
        r = lax.broadcasted_iota(I32, (na, na), 0)
        c = lax.broadcasted_iota(I32, (na, na), 1)
        before_ref[...] = (r < c).astype(BF16)
        zero_ref[...] = jnp.zeros_like(zero_ref)

    et = eid_ref[...].astype(F32).T
    e_row = jnp.concatenate([et[0:1], et[1:2]], axis=1)
    hit = lax.broadcasted_iota(I32, (LANES, na), 0).astype(F32) == e_row
    hit_b = hit.astype(BF16)
    rank = jnp.dot(hit_b, before_ref[...], preferred_element_type=F32)
    start = jnp.dot(lrow_ref[...].astype(BF16), hit_b, preferred_element_type=F32)[0:1] * MOE_CHUNK
    pos = (start + jnp.sum(jnp.where(hit, rank, 0.0), axis=0, keepdims=True)).astype(I32)
    r = lax.broadcasted_iota(I32, (loc, tm), 0)
    sel = ((r == pos[:, :tm]) | (r == pos[:, tm:])).astype(BF16)
    xloc_ref[slot] = _pack_rows(jnp.dot(sel, hn_ref[...], preferred_element_type=F32))

    def copy_out(local_row, global_row, s, rows=MOE_CHUNK):
        return pltpu.make_async_copy(xloc_ref.at[s, pl.ds(local_row, rows), :],
                                     xs_hbm.at[pl.ds(global_row, rows), :], sem.at[s])

    _segment_loop(t, lst_ref, gofs_ref, nch_ref, lambda lr, gr, rows: copy_out(lr, gr, slot, rows).start())

    @pl.when(t > 0)
    def _():
        _wait_chunks(ntot_ref[jnp.maximum(t - 1, 0)], lambda rows: copy_out(0, 0, 1 - slot, rows).wait())

    @pl.when(t == last)
    def _():
        _wait_chunks(ntot_ref[t], lambda rows: copy_out(0, 0, slot, rows).wait())

        def zero_out(global_row):
            return pltpu.make_async_copy(zero_ref, xs_hbm.at[pl.ds(global_row, MOE_CHUNK), :], sem.at[0])

        def per_expert(e, carry):
            def per_chunk(c, cc):
                zero_out(pl.multiple_of(tail_ref[e] + c * MOE_CHUNK, MOE_CHUNK)).start()
                return cc
            lax.fori_loop(0, tailn_ref[e], per_chunk, 0)
            return carry
        lax.fori_loop(0, N_EXPERTS, per_expert, 0)
        _repeat(tailtot_ref[0], lambda: zero_out(0).wait())


def _dispatch(plan, hn2d, eid2d, lrow, p_rows, tm):
    n = hn2d.shape[0]
    na = TOP_K * tm
    loc = _local_rows(tm)
    pre = (plan["gofs"], plan["lst"], plan["nch"], plan["ntot"], plan["tail"], plan["tailn"], plan["tailtot"])
    est = 2 * tm * D_MODEL * 2 + na * na * 2 + 2 * loc * PACKED * 4 + loc * D_MODEL * 8 + loc * tm * 4 \
        + 4 * LANES * na * 4
    grid_spec = pltpu.PrefetchScalarGridSpec(
        num_scalar_prefetch=len(pre),
        grid=(n // tm,),
        in_specs=[pl.BlockSpec((tm, D_MODEL), lambda t, *_: (t, 0)), pl.BlockSpec((tm, LANES), lambda t, *_: (t, 0)),
                  pl.BlockSpec((None, SUBLANES, LANES), lambda t, *_: (t, 0, 0))],
        out_specs=pl.BlockSpec(memory_space=pl.ANY),
        scratch_shapes=[pltpu.VMEM((na, na), BF16), pltpu.VMEM((2, loc, PACKED), U32),
                        pltpu.VMEM((MOE_CHUNK, PACKED), U32), pltpu.SemaphoreType.DMA((2,))],
    )
    return pl.pallas_call(
        functools.partial(_dispatch_kernel, tm=tm),
        out_shape=jax.ShapeDtypeStruct((p_rows, PACKED), U32),
        grid_spec=grid_spec,
        compiler_params=_params(("arbitrary",), est),
        name="moe_dispatch",
    )(*pre, hn2d, eid2d, lrow)


def _expert_kernel(blk_e_ref, blk_src_ref, blk_n_ref, xs_ref, wgu_ref, wd_ref, ys_ref, wgu_bf, wd_bf):
    i = pl.program_id(0)

    @pl.when(blk_n_ref[i] > 0)
    def _():
        @pl.when((i == 0) | (blk_e_ref[i] != blk_e_ref[jnp.maximum(i - 1, 0)]))
        def _():
            wgu_bf[...] = wgu_ref[0].astype(BF16)
            wd_bf[...] = wd_ref[0].astype(BF16)

        gu = jnp.dot(_unpack_rows(xs_ref[...]), wgu_bf[...], preferred_element_type=F32)
        act = (jax.nn.silu(gu[:, :EXPERT_FF]) * gu[:, EXPERT_FF:]).astype(BF16)
        ys_ref[...] = _pack_rows(jnp.dot(act, wd_bf[...], preferred_element_type=F32))


def _experts(plan, xs, wgu, wd, eb):
    n_blk = xs.shape[0] // eb
    rows = pl.BlockSpec((eb, PACKED), lambda i, be, bs, bn: (bs[i], 0))
    w_elems = wgu.shape[1] * wgu.shape[2] + wd.shape[1] * wd.shape[2]
    est = 2 * 2 * eb * PACKED * 4 + 2 * w_elems * 4 + w_elems * 2 + 6 * eb * D_MODEL * 4
    grid_spec = pltpu.PrefetchScalarGridSpec(
        num_scalar_prefetch=3,
        grid=(n_blk,),
        in_specs=[rows, pl.BlockSpec((1,) + wgu.shape[1:], lambda i, be, bs, bn: (be[i], 0, 0)),
                  pl.BlockSpec((1,) + wd.shape[1:], lambda i, be, bs, bn: (be[i], 0, 0))],
        out_specs=rows,
        scratch_shapes=[pltpu.VMEM(wgu.shape[1:], BF16), pltpu.VMEM(wd.shape[1:], BF16)],
    )
    return pl.pallas_call(
        _expert_kernel,
        out_shape=jax.ShapeDtypeStruct(xs.shape, U32),
        grid_spec=grid_spec,
        compiler_params=_params(("arbitrary",), est),
        name="experts",
    )(plan["blk_e"], plan["blk_src"], plan["blk_n"], xs, wgu, wd)


def _combine_kernel(gofs_ref, lst_ref, nch_ref, ntot_ref, x2_ref, ew_ref, eid_ref, lrow_ref, ys_hbm, o_ref,
                    before_ref, yloc_ref, sem, *, tm):
    t = pl.program_id(0)
    last = pl.num_programs(0) - 1
    slot = t % 2
    na = TOP_K * tm
    loc = yloc_ref.shape[1]

    def copy_in(local_row, global_row, s, rows=MOE_CHUNK):
        return pltpu.make_async_copy(ys_hbm.at[pl.ds(global_row, rows), :],
                                     yloc_ref.at[s, pl.ds(local_row, rows), :], sem.at[s])

    def fetch(tile, s):
        _segment_loop(tile, lst_ref, gofs_ref, nch_ref, lambda lr, gr, rows: copy_in(lr, gr, s, rows).start())

    @pl.when(t == 0)
    def _():
        r = lax.broadcasted_iota(I32, (na, na), 0)
        c = lax.broadcasted_iota(I32, (na, na), 1)
        before_ref[...] = (c < r).astype(BF16)
        yloc_ref[...] = jnp.zeros_like(yloc_ref)
        fetch(0, 0)

    @pl.when(t < last)
    def _():
        fetch(t + 1, 1 - slot)

    _wait_chunks(ntot_ref[t], lambda rows: copy_in(0, 0, slot, rows).wait())

    eid = eid_ref[...]
    lane = lax.broadcasted_iota(I32, (tm, LANES), 1)
    hit = jnp.concatenate([lane == eid[:, 0:1], lane == eid[:, 1:2]], axis=0)
    rank = jnp.dot(before_ref[...], hit.astype(BF16), preferred_element_type=F32)
    start = lrow_ref[0:1, :] * MOE_CHUNK
    pos = jnp.sum(jnp.where(hit, rank + start, 0.0), axis=1, keepdims=True).astype(I32)
    col = lax.broadcasted_iota(I32, (tm, loc), 1)
    ew = ew_ref[...]
    sel = jnp.where(col == pos[:tm], ew[:, 0:1], 0.0) + jnp.where(col == pos[tm:], ew[:, 1:2], 0.0)
    o_ref[...] = x2_ref[...] + jnp.dot(sel.astype(BF16), _unpack_rows(yloc_ref[slot]), preferred_element_type=F32)


def _combine(plan, x2, ew, eid2d, lrow, ys, tm):
    n = x2.shape[0]
    na = TOP_K * tm
    loc = _local_rows(tm)
    pre = (plan["gofs"], plan["lst"], plan["nch"], plan["ntot"])
    tok = pl.BlockSpec((tm, D_MODEL), lambda t, *_: (t, 0))
    small = pl.BlockSpec((tm, LANES), lambda t, *_: (t, 0))
    est = 2 * 2 * tm * D_MODEL * 4 + na * na * 2 + 2 * loc * PACKED * 4 + loc * D_MODEL * 10 + 2 * tm * loc * 4 \
        + 3 * tm * D_MODEL * 4
    grid_spec = pltpu.PrefetchScalarGridSpec(
        num_scalar_prefetch=len(pre),
        grid=(n // tm,),
        in_specs=[tok, small, small, pl.BlockSpec((None, SUBLANES, LANES), lambda t, *_: (t, 0, 0)),
                  pl.BlockSpec(memory_space=pl.ANY)],
        out_specs=tok,
        scratch_shapes=[pltpu.VMEM((na, na), BF16), pltpu.VMEM((2, loc, PACKED), U32),
                        pltpu.SemaphoreType.DMA((2,))],
    )
    return pl.pallas_call(
        functools.partial(_combine_kernel, tm=tm),
        out_shape=jax.ShapeDtypeStruct((n, D_MODEL), F32),
        grid_spec=grid_spec,
        compiler_params=_params(("arbitrary",), est),
        name="moe_combine",
    )(*pre, x2, ew, eid2d, lrow, ys)


def _rope_table(positions):
    inv_freq = jnp.exp(-math.log(ROPE_THETA) * jnp.arange(ROT_HALF, dtype=F32) / ROT_HALF)
    ang = positions.astype(F32).reshape(-1, 1) * inv_freq
    seg = jnp.concatenate([jnp.cos(ang), jnp.sin(ang), jnp.zeros((ang.shape[0], HEAD_DIM - ROT_DIMS), F32)], axis=1)
    return jnp.tile(seg, (1, LANES // HEAD_DIM))


def _plan_rows(n_tok, tm, eb):
    worst = n_tok * TOP_K + (n_tok // tm) * N_EXPERTS * (MOE_CHUNK - 1)
    return (worst + eb - 1) // eb * eb + N_EXPERTS * eb


def _moe_plan(hist, p_rows, eb):
    n_tiles = hist.shape[0]
    seg = (hist + MOE_CHUNK - 1) // MOE_CHUNK * MOE_CHUNK
    tot = jnp.sum(seg, axis=0)
    region = (tot + eb - 1) // eb * eb
    pend = jnp.cumsum(region)
    pstart = pend - region
    gofs = pstart[None, :] + jnp.cumsum(seg, axis=0) - seg
    lst = jnp.cumsum(seg, axis=1) - seg
    nch = seg // MOE_CHUNK
    n_blk = p_rows // eb
    blk_first = jnp.arange(n_blk, dtype=I32) * eb
    blk_e = jnp.minimum(jnp.sum(blk_first[:, None] >= pend[None, :], axis=1), N_EXPERTS - 1).astype(I32)
    blk_n = jnp.clip(pstart[blk_e] + tot[blk_e] - blk_first, 0, eb)
    blk_src = jnp.minimum(jnp.arange(n_blk, dtype=I32), jnp.maximum(pend[-1] // eb - 1, 0))
    tailn = (region - tot) // MOE_CHUNK
    lrow = jnp.pad((lst // MOE_CHUNK).astype(F32), ((0, 0), (0, LANES - N_EXPERTS)))
    flat = lambda a: a.reshape(-1).astype(I32)
    plan = dict(gofs=flat(gofs), lst=flat(lst), nch=flat(nch), ntot=flat(jnp.sum(nch, axis=1)),
                tail=flat(pstart + tot), tailn=flat(tailn), tailtot=flat(jnp.sum(tailn)),
                blk_e=blk_e, blk_src=flat(blk_src), blk_n=flat(blk_n))
    return plan, jnp.broadcast_to(lrow[:, None, :], (n_tiles, SUBLANES, LANES))


def _tile(n, pref):
    t = min(n, pref)
    assert n % t == 0, (n, pref)
    return t


def kernel(x, mem, positions, norm_mix, w_in, conv_w, conv_b, lru_wa, lru_ba, lru_wx, lru_bx, lru_lambda, w_lru_o, q_norm, k_norm, lambda_q1, lambda_k1, lambda_q2, lambda_k2, subln, w_attn_o, w_out, norm_cx, norm_mem, w_cq, w_ckv, cq_norm, ck_norm, w_co, norm_ffn, w_group, b_group, w_router, b_router, w_gate_up, w_down):
    bsz, s, d = x.shape
    assert d == D_MODEL and w_in.shape[-1] == N_PROJ * D_MODEL
    n = bsz * s
    depth = w_in.shape[0]
    tm = _tile(n, 512)
    ts = _tile(s, 512)
    tq = _tile(s, 512)
    assert tq % CHUNK == 0 and ts % SUBLANES == 0
    eb = _tile(n, 512)
    rope = _rope_table(positions)
    row = lambda v: v.reshape(1, -1).astype(F32)
    rep = LANES // HEAD_DIM

    for layer in range(depth):
        lambda_init = 0.8 - 0.6 * math.exp(-0.3 * layer)
        wax = (0.5 * jnp.concatenate([lru_wa[layer], lru_wx[layer]], axis=-1)).astype(BF16)
        bax = 0.5 * jnp.stack([lru_ba[layer], lru_bx[layer]]).astype(F32)
        lru_params = (conv_w[layer].astype(F32), row(conv_b[layer]), wax, bax, row(lru_lambda[layer]),
                      w_lru_o[layer].astype(BF16))
        ml, q, k, vt, ga = _in_proj(
            x.reshape(n, d), row(norm_mix[layer]), w_in[layer].astype(BF16),
            jnp.tile(row(q_norm[layer]), (1, rep)), jnp.tile(row(k_norm[layer]), (1, rep)), rope, lru_params, ts, s)
        seq = lambda a: a.reshape(bsz, s, d)

        lam = (jnp.exp(jnp.sum(lambda_q1[layer].astype(F32) * lambda_k1[layer].astype(F32)))
               - jnp.exp(jnp.sum(lambda_q2[layer].astype(F32) * lambda_k2[layer].astype(F32))) + lambda_init)
        o = _diff_attn(lam.reshape(1, 1), seq(q), seq(k), vt, subln[layer].astype(F32).reshape(V_DIM, 1),
                       lambda_init, tq)

        kc, vc = _mem_kv(mem, row(norm_mem[layer]), w_ckv[layer].astype(BF16), row(ck_norm[layer]))
        w_r = jnp.concatenate([w_group[layer], w_router[layer],
                               jnp.zeros((d, LANES - N_GROUPS - N_EXPERTS), F32)], axis=1).astype(F32)
        b_r = jnp.concatenate([b_group[layer], b_router[layer],
                               jnp.zeros((LANES - N_GROUPS - N_EXPERTS,), F32)]).reshape(1, LANES).astype(F32)
        wrh, wrl = _split_bf16(w_r)
        x2, hn, eid, ew, hist = _cross_router(
            x, o, seq(ga), seq(ml), w_attn_o[layer].astype(BF16), w_out[layer].astype(BF16),
            row(norm_cx[layer]), w_cq[layer].astype(BF16), row(cq_norm[layer]), kc, vc,
            w_co[layer].astype(BF16), row(norm_ffn[layer]), wrh, wrl, b_r, ts)

        p_rows = _plan_rows(n, ts, eb)
        plan, lrow = _moe_plan(hist[:, :, 0, :N_EXPERTS].reshape(n // ts, N_EXPERTS), p_rows, eb)
        eid2d = eid.reshape(n, LANES)
        xs = _dispatch(plan, hn.reshape(n, d), eid2d, lrow, p_rows, ts)
        ys = _experts(plan, xs, w_gate_up[layer], w_down[layer], eb)
        x = _combine(plan, x2.reshape(n, d), ew.reshape(n, LANES), eid2d, lrow, ys, ts).reshape(bsz, s, d)
    return x
```

```python
import functools
import math

import jax
import jax.numpy as jnp
from jax import lax
from jax.experimental import pallas as pl
from jax.experimental.pallas import tpu as pltpu

F32 = jnp.float32
BF16 = jnp.bfloat16
I32 = jnp.int32

D_MODEL = 1024
CHUNK = 64
LRU_BLOCKS = 8
LRU_BLOCK_WIDTH = D_MODEL // LRU_BLOCKS
CONV_WIDTH = 4
LRU_C = 8.0
ATTN_HEADS = 8
HEAD_DIM = 64
V_DIM = 2 * HEAD_DIM
ROPE_THETA = 500000.0
ROT_DIMS = HEAD_DIM // 4
ROT_HALF = ROT_DIMS // 2
MEM_HEADS = 4
MEM_HEAD_DIM = 128
MEM_WIDTH = MEM_HEADS * MEM_HEAD_DIM
N_GROUPS = 4
EXPERTS_PER_GROUP = 8
N_EXPERTS = N_GROUPS * EXPERTS_PER_GROUP
TOP_K = 2
EXPERT_FF = 512
N_PROJ = 7
EPS = 1e-6
NEG_INF = -1e30
LOG2_E = math.log2(math.e)
ATTN_COL_GROUP = 512
ATTN_HEADS_PER_STEP = 4

LANES = 128
SUBLANES = 8
V7X_VMEM_BYTES = 64 * 1024 * 1024
MIB = 1024 * 1024


def _vmem_limit(estimate_bytes):
    return int(min(max(estimate_bytes * 3 // 2, 16 * MIB), V7X_VMEM_BYTES - 8 * MIB))


def _params(semantics, vmem_estimate):
    return pltpu.CompilerParams(dimension_semantics=semantics, vmem_limit_bytes=_vmem_limit(vmem_estimate))


def _resident(shape, index_map):
    return pl.BlockSpec(shape, index_map, pipeline_mode=pl.Buffered(1))


def _rms(x, g):
    return x * lax.rsqrt(jnp.mean(x * x, axis=-1, keepdims=True) + EPS) * g


def _sigmoid(x):
    return 0.5 * jnp.tanh(0.5 * x) + 0.5


def _segment_ones():
    r = lax.broadcasted_iota(I32, (LANES, LANES), 0) // HEAD_DIM
    c = lax.broadcasted_iota(I32, (LANES, LANES), 1) // HEAD_DIM
    return (r == c).astype(BF16)


def _qk_post(p, gain, cos_t, sin_lo, sin_hi, seg, scale):
    cols = []
    for c in range(D_MODEL // LANES):
        pc = p[:, c * LANES:(c + 1) * LANES]
        ss = jnp.dot((pc * pc).astype(BF16), seg, preferred_element_type=F32)
        y = pc * lax.rsqrt(ss * (1.0 / HEAD_DIM) + EPS) * gain
        y = y * cos_t + pltpu.roll(y, LANES - ROT_HALF, 1) * sin_lo + pltpu.roll(y, ROT_HALF, 1) * sin_hi
        cols.append((y * scale).astype(BF16))
    return jnp.concatenate(cols, axis=1)


def _in_proj_kernel(x_ref, g_ref, w_ref, qg_ref, kg_ref, rope_ref, cw_ref, cb_ref, wax_ref, bax_ref, lam_ref, wo_ref,
                    ml_ref, q_ref, k_ref, vt_ref, ga_ref, xpad_ref, hprev_ref, *, per_seq):
    @pl.when(pl.program_id(0) % per_seq == 0)
    def _():
        xpad_ref[0:SUBLANES, :] = jnp.zeros((SUBLANES, D_MODEL), F32)
        hprev_ref[...] = jnp.zeros_like(hprev_ref)

    h = _rms(x_ref[...], g_ref[...]).astype(BF16)

    def proj(j):
        return jnp.dot(h, w_ref[:, j * D_MODEL:(j + 1) * D_MODEL], preferred_element_type=F32)

    xc, pre_a, pre_x = _lru_conv_gates(proj(0), cw_ref, cb_ref, wax_ref, xpad_ref)
    seg = _segment_ones()
    tab = rope_ref[...]
    seg_lane = lax.broadcasted_iota(I32, tab.shape, 1) % HEAD_DIM
    first, second = seg_lane < ROT_HALF, (seg_lane >= ROT_HALF) & (seg_lane < ROT_DIMS)
    cos_t = jnp.where(first, tab, jnp.where(second, pltpu.roll(tab, ROT_HALF, 1), 1.0))
    sin_lo = jnp.where(first, -pltpu.roll(tab, LANES - ROT_HALF, 1), 0.0)
    sin_hi = jnp.where(second, tab, 0.0)
    q_ref[...] = _qk_post(proj(2), qg_ref[...], cos_t, sin_lo, sin_hi, seg, HEAD_DIM ** -0.5 * LOG2_E)
    k_ref[...] = _qk_post(proj(3), kg_ref[...], cos_t, sin_lo, sin_hi, seg, 1.0)
    hr = _lru_scan(xc, pre_a, pre_x, bax_ref, lam_ref, hprev_ref)
    gate, merge_gate = proj(1), proj(5)
    v = proj(4)
    for hd in range(ATTN_HEADS):
        vt_ref[hd * V_DIM:(hd + 1) * V_DIM, :] = v[:, hd * V_DIM:(hd + 1) * V_DIM].T.astype(BF16)
    ml_ref[...] = _lru_out(hr, gate, merge_gate, wo_ref)
    ga_ref[...] = proj(6).astype(BF16)


def _in_proj(x2d, g, w_in, qg, kg, rope, lru_params, tm, seq_len):
    n = x2d.shape[0]
    per_seq = seq_len // tm
    row = lambda i: (i, 0)
    tok = pl.BlockSpec((tm, D_MODEL), row)
    tok_out = jax.ShapeDtypeStruct((n, D_MODEL), BF16)
    vt_out = jax.ShapeDtypeStruct((n // seq_len, D_MODEL, seq_len), BF16)
    vt_spec = pl.BlockSpec((None, D_MODEL, tm), lambda i: (i // per_seq, 0, i % per_seq))
    whole = lambda a: _resident(a.shape, lambda i: (0,) * a.ndim)
    consts = (g, w_in, qg, kg)
    est = (sum(a.size * a.dtype.itemsize for a in consts + tuple(lru_params)) + 2 * tm * D_MODEL * 4
           + 2 * tm * LANES * 4 + 5 * 2 * tm * D_MODEL * 2 + 16 * tm * D_MODEL * 4)
    return pl.pallas_call(
        functools.partial(_in_proj_kernel, per_seq=per_seq),
        out_shape=[tok_out] * 3 + [vt_out, tok_out],
        grid=(n // tm,),
        in_specs=[tok] + [whole(a) for a in consts] + [pl.BlockSpec((tm, LANES), row)]
        + [whole(a) for a in lru_params],
        out_specs=[tok] * 3 + [vt_spec, tok],
        scratch_shapes=[pltpu.VMEM((tm + SUBLANES, D_MODEL), F32), pltpu.VMEM((SUBLANES, D_MODEL), F32)],
        compiler_params=_params(("arbitrary",), est),
        name="in_proj_lru",
    )(x2d, *consts, rope, *lru_params)


def _lru_conv_gates(x, cw_ref, cb_ref, wax_ref, xpad_ref):
    tt = x.shape[0]
    xpad_ref[SUBLANES:SUBLANES + tt, :] = x
    cw = cw_ref[...]
    xc = cb_ref[...] + cw[3:4] * x
    for j in range(1, CONV_WIDTH):
        xc = xc + cw[CONV_WIDTH - 1 - j:CONV_WIDTH - j] * xpad_ref[SUBLANES - j:SUBLANES - j + tt, :]
    xpad_ref[0:SUBLANES, :] = x[tt - SUBLANES:tt]

    xcb = xc.astype(BF16)
    ra, ri = [], []
    for n in range(LRU_BLOCKS):
        g = jnp.dot(xcb[:, n * LRU_BLOCK_WIDTH:(n + 1) * LRU_BLOCK_WIDTH], wax_ref[n], preferred_element_type=F32)
        ra.append(g[:, :LRU_BLOCK_WIDTH])
        ri.append(g[:, LRU_BLOCK_WIDTH:])
    return xc, jnp.concatenate(ra, axis=1), jnp.concatenate(ri, axis=1)


def _lru_scan(xc, pre_a, pre_x, bax_ref, lam_ref, hprev_ref):
    nblk = xc.shape[0] // SUBLANES
    bax = bax_ref[...]
    t_r = jnp.tanh(pre_a + bax[0:1])
    i = 0.5 * jnp.tanh(pre_x + bax[1:2]) + 0.5
    u = (0.5 * LRU_C) * jax.nn.softplus(-lam_ref[...]) * (t_r + 1.0)
    a = jnp.exp(-u)
    gain2 = jnp.tanh(u) * (a * a + 1.0)
    b = jnp.where(gain2 > 0.0, gain2 * lax.rsqrt(gain2), 0.0) * i * xc

    a3 = a.reshape(nblk, SUBLANES, D_MODEL)
    b3 = b.reshape(nblk, SUBLANES, D_MODEL)
    sub = lax.broadcasted_iota(I32, (nblk, SUBLANES, D_MODEL), 1)
    shift = 1
    while shift < SUBLANES:
        keep = sub >= shift
        a_sh = pltpu.roll(a3, shift, 1)
        b_sh = pltpu.roll(b3, shift, 1)
        b3 = jnp.where(keep, a3 * b_sh + b3, b3)
        a3 = jnp.where(keep, a3 * a_sh, a3)
        shift *= 2
    h_last = hprev_ref[...]
    groups = []
    for blk in range(nblk):
        hb = a3[blk] * h_last + b3[blk]
        groups.append(hb)
        h_last = jnp.broadcast_to(hb[SUBLANES - 1:SUBLANES], (SUBLANES, D_MODEL))
    hprev_ref[...] = h_last
    return jnp.concatenate(groups, axis=0)


def _lru_out(hr, gate, merge_gate, wo_ref):
    y = (jax.nn.gelu(gate) * hr).astype(BF16)
    yl = jnp.dot(y, wo_ref[...], preferred_element_type=F32)
    return (_sigmoid(merge_gate) * yl).astype(BF16)


def _attn_kernel(lam_ref, q_ref, qnext_ref, k_ref, vt_ref, sub_ref, o_ref, m_ref, l_ref, acc_ref, qz_ref, sa_ref,
                 sb_ref, mxa_ref, mxb_ref, *, tq, out_scale):
    i = pl.program_id(2)
    last = pl.num_programs(2) - 1
    heads = range(ATTN_HEADS_PER_STEP)
    m_ref[...] = jnp.full_like(m_ref, NEG_INF)
    l_ref[...] = jnp.zeros_like(l_ref)
    acc_ref[...] = jnp.zeros_like(acc_ref)
    cw = ATTN_COL_GROUP

    def load_queries(src_ref):
        row = lax.broadcasted_iota(I32, (V_DIM, tq), 0)
        for h in heads:
            qt = src_ref[:, h * V_DIM:(h + 1) * V_DIM].astype(F32).T
            zero = jnp.zeros_like(qt)
            qz_ref[h] = jnp.concatenate([jnp.where(row < HEAD_DIM, qt, zero), jnp.where(row >= HEAD_DIM, qt, zero)],
                                        axis=1).astype(BF16)

    def block_off(j):
        return pl.multiple_of(j * tq, tq)

    buf_a, buf_b = (sa_ref, mxa_ref), (sb_ref, mxb_ref)

    def scores(j, buf):
        s_ref, mx_ref = buf
        for h in heads:
            kb = k_ref[pl.ds(block_off(j), tq), h * V_DIM:(h + 1) * V_DIM]
            s = jnp.dot(kb, qz_ref[h], preferred_element_type=F32)
            s_ref[h] = s
            mx_ref[h] = jnp.max(s, axis=0, keepdims=True)

    def softmax_pv(j, buf, diagonal):
        s_ref, mx_ref = buf
        for h in heads:
            vtb = vt_ref[h * V_DIM:(h + 1) * V_DIM, pl.ds(block_off(j), tq)]
            for g in range(2 * tq // cw):
                cols = pl.ds(g * cw, cw)
                s = s_ref[h, :, cols]
                if diagonal:
                    key = lax.broadcasted_iota(I32, (tq, cw), 0)
                    qry = (lax.broadcasted_iota(I32, (tq, cw), 1) + g * cw) % tq
                    s = jnp.where((key // CHUNK) <= (qry // CHUNK), s, NEG_INF)
                    blk_max = jnp.max(s, axis=0, keepdims=True)
                else:
                    blk_max = mx_ref[h, :, cols]
                m_prev = m_ref[h, :, cols]
                m_new = jnp.maximum(m_prev, blk_max)
                alpha = jnp.exp2(m_prev - m_new)
                p = jnp.exp2(s - m_new)
                l_ref[h, :, cols] = alpha * l_ref[h, :, cols] + jnp.sum(p, axis=0, keepdims=True)
                acc_ref[h, :, cols] = alpha * acc_ref[h, :, cols] + jnp.dot(vtb, p.astype(BF16),
                                                                            preferred_element_type=F32)
                m_ref[h, :, cols] = m_new

    @pl.when(i == 0)
    def _():
        load_queries(q_ref)
        scores(0, buf_a)

    def pair(p, carry):
        j = 2 * p
        scores(j + 1, buf_b)
        softmax_pv(j, buf_a, False)
        scores(j + 2, buf_a)
        softmax_pv(j + 1, buf_b, False)
        return carry

    lax.fori_loop(0, i // 2, pair, 0)

    @pl.when(i % 2 == 1)
    def _():
        scores(i, buf_b)
        softmax_pv(i - 1, buf_a, False)
        softmax_pv(i, buf_b, True)

    @pl.when(i % 2 == 0)
    def _():
        softmax_pv(i, buf_a, True)

    def write_output():
        for h in heads:
            o12 = acc_ref[h] * (1.0 / l_ref[h])
            ot = o12[:, :tq] - lam_ref[0, 0] * o12[:, tq:]
            ot = ot * lax.rsqrt(jnp.mean(ot * ot, axis=0, keepdims=True) + EPS) * sub_ref[...] * out_scale
            o_ref[:, h * V_DIM:(h + 1) * V_DIM] = ot.T.astype(BF16)

    @pl.when(i < last)
    def _():
        load_queries(qnext_ref)
        scores(0, buf_a)
        write_output()

    @pl.when(i == last)
    def _():
        write_output()


def _diff_attn(lam, q, k, vt, sub, lambda_init, tq):
    bsz, s, _ = q.shape
    hps = ATTN_HEADS_PER_STEP
    width = hps * V_DIM
    nq = s // tq
    qspec = pl.BlockSpec((None, tq, width), lambda b, h, i: (b, i, h))
    qnext_spec = pl.BlockSpec((None, tq, width), lambda b, h, i: (b, jnp.minimum(i + 1, nq - 1), h))
    kspec = pl.BlockSpec((None, s, width), lambda b, h, i: (b, 0, h))
    vtspec = pl.BlockSpec((None, width, s), lambda b, h, i: (b, h, 0))
    est = 2 * 2 * s * width * 2 + 4 * tq * width * 2 + hps * (V_DIM * 2 * tq * 6 + 2 * tq * 2 * tq * 4) \
        + 3 * 2 * tq * tq * 4
    return pl.pallas_call(
        functools.partial(_attn_kernel, tq=tq, out_scale=1.0 - lambda_init),
        out_shape=jax.ShapeDtypeStruct((bsz, s, ATTN_HEADS * V_DIM), BF16),
        grid=(bsz, ATTN_HEADS // hps, nq),
        in_specs=[pl.BlockSpec(memory_space=pltpu.SMEM), qspec, qnext_spec, kspec, vtspec,
                  pl.BlockSpec((V_DIM, 1), lambda b, h, i: (0, 0))],
        out_specs=qspec,
        scratch_shapes=[pltpu.VMEM((hps, 1, 2 * tq), F32), pltpu.VMEM((hps, 1, 2 * tq), F32),
                        pltpu.VMEM((hps, V_DIM, 2 * tq), F32), pltpu.VMEM((hps, V_DIM, 2 * tq), BF16),
                        pltpu.VMEM((hps, tq, 2 * tq), F32), pltpu.VMEM((hps, tq, 2 * tq), F32),
                        pltpu.VMEM((hps, 1, 2 * tq), F32), pltpu.VMEM((hps, 1, 2 * tq), F32)],
        compiler_params=_params(("parallel", "parallel", "arbitrary"), est),
        name="diff_attn",
    )(lam, q, q, k, vt, sub)


def _mix_out(x, o, ga, ml, wao_ref, wout_ref):
    ya = jnp.dot(o, wao_ref[...], preferred_element_type=F32)
    mixed = ml.astype(F32) + _sigmoid(ga.astype(F32)) * ya
    return x + jnp.dot(mixed.astype(BF16), wout_ref[...], preferred_element_type=F32)


def _mem_kv_kernel(mem_ref, g_ref, w_ref, ckg_ref, k_ref, v_ref):
    h = _rms(mem_ref[...], g_ref[...]).astype(BF16)
    kv = jnp.dot(h, w_ref[...], preferred_element_type=F32)
    ks = [_rms(kv[:, hd * MEM_HEAD_DIM:(hd + 1) * MEM_HEAD_DIM], ckg_ref[...]) for hd in range(MEM_HEADS)]
    k_ref[...] = jnp.concatenate(ks, axis=1).astype(BF16)
    v_ref[...] = kv[:, MEM_WIDTH:].astype(BF16)


def _mem_kv(mem, g, w, ckg):
    bsz, m, _ = mem.shape
    fixed = lambda b: (0, 0)
    out = pl.BlockSpec((None, m, MEM_WIDTH), lambda b: (b, 0, 0))
    est = 2 * m * D_MODEL * 4 + w.size * 2 + 4 * m * MEM_WIDTH * 2 + 4 * m * D_MODEL * 4
    return pl.pallas_call(
        _mem_kv_kernel,
        out_shape=[jax.ShapeDtypeStruct((bsz, m, MEM_WIDTH), BF16)] * 2,
        grid=(bsz,),
        in_specs=[pl.BlockSpec((None, m, D_MODEL), lambda b: (b, 0, 0)), _resident((1, D_MODEL), fixed),
                  _resident(w.shape, fixed), _resident((1, MEM_HEAD_DIM), fixed)],
        out_specs=[out, out],
        compiler_params=_params(("parallel",), est),
        name="mem_kv",
    )(mem, g, w, ckg)


def _split_bf16(x):
    hi = x.astype(BF16)
    return hi, (x - hi.astype(F32)).astype(BF16)


def _cross_router_kernel(x_ref, o_ref, ga_ref, ml_ref, wao_ref, wout_ref, gcx_ref, wcq_ref, cqg_ref, kc_ref, vc_ref,
                         wco_ref, gffn_ref, wrh_ref, wrl_ref, br_ref, x2_ref, hn_ref, eid_ref, ew_ref, hist_ref):
    x1 = _mix_out(x_ref[...], o_ref[...], ga_ref[...], ml_ref[...], wao_ref, wout_ref)
    q = jnp.dot(_rms(x1, gcx_ref[...]).astype(BF16), wcq_ref[...], preferred_element_type=F32)
    outs = []
    for hd in range(MEM_HEADS):
        sl = slice(hd * MEM_HEAD_DIM, (hd + 1) * MEM_HEAD_DIM)
        qh = _rms(q[:, sl], cqg_ref[...]) * MEM_HEAD_DIM ** -0.5
        s = lax.dot_general(qh.astype(BF16), kc_ref[:, sl], (((1,), (1,)), ((), ())), preferred_element_type=F32)
        p = jnp.exp(s - jnp.max(s, axis=1, keepdims=True))
        o = jnp.dot(p.astype(BF16), vc_ref[:, sl], preferred_element_type=F32)
        outs.append(o / jnp.sum(p, axis=1, keepdims=True))
    x2 = x1 + jnp.dot(jnp.concatenate(outs, axis=1).astype(BF16), wco_ref[...], preferred_element_type=F32)
    x2_ref[...] = x2

    hn = _rms(x2, gffn_ref[...])
    hn_ref[...] = hn.astype(BF16)
    h_hi, h_lo = _split_bf16(hn)
    logits = (jnp.dot(h_hi, wrh_ref[...], preferred_element_type=F32)
              + jnp.dot(h_lo, wrh_ref[...], preferred_element_type=F32)
              + jnp.dot(h_hi, wrl_ref[...], preferred_element_type=F32)) + br_ref[...]
    lane = lax.broadcasted_iota(I32, logits.shape, 1)
    is_group = lane < N_GROUPS
    gl = jnp.where(is_group, logits, NEG_INF)
    gmax = jnp.max(gl, axis=1, keepdims=True)
    gval = 1.0 / jnp.sum(jnp.where(is_group, jnp.exp(gl - gmax), 0.0), axis=1, keepdims=True)
    gidx = jnp.min(jnp.where(gl == gmax, lane, LANES), axis=1, keepdims=True)
    lane_group = lax.shift_right_logical(lane + (EXPERTS_PER_GROUP - N_GROUPS), 3) - 1
    chosen = lane_group == gidx
    el = jnp.where(chosen, logits, NEG_INF)
    v1 = jnp.max(el, axis=1, keepdims=True)
    i1 = jnp.min(jnp.where(chosen & (el == v1), lane, LANES), axis=1, keepdims=True)
    rest = chosen & (lane != i1)
    el2 = jnp.where(rest, logits, NEG_INF)
    v2 = jnp.max(el2, axis=1, keepdims=True)
    i2 = jnp.min(jnp.where(rest & (el2 == v2), lane, LANES), axis=1, keepdims=True)
    t = jnp.exp(v2 - v1)
    w1 = gval / (1.0 + t)
    w2 = gval * t / (1.0 + t)
    e1, e2 = i1 - N_GROUPS, i2 - N_GROUPS
    eid_ref[...] = jnp.where(lane == 0, e1, jnp.where(lane == 1, e2, 0))
    ew_ref[...] = jnp.where(lane == 0, w1, jnp.where(lane == 1, w2, 0.0))
    count = jnp.sum((lane == e1).astype(I32) + (lane == e2).astype(I32), axis=0, keepdims=True)
    hist_ref[...] = jnp.broadcast_to(count, hist_ref.shape)


def _cross_router(x, o, ga, ml, wao, wout, gcx, wcq, cqg, kc, vc, wco, gffn, wrh, wrl, br, tm):
    bsz, s, _ = x.shape
    m = kc.shape[1]
    tok = pl.BlockSpec((None, tm, D_MODEL), lambda b, i: (b, i, 0))
    small = pl.BlockSpec((None, tm, LANES), lambda b, i: (b, i, 0))
    memb = pl.BlockSpec((None, m, MEM_WIDTH), lambda b, i: (b, 0, 0))
    fixed = lambda b, i: (0, 0)
    est = (3 * 2 * tm * D_MODEL * 4 + 4 * 2 * tm * D_MODEL * 2 + 2 * 2 * tm * LANES * 4 + 2 * 2 * m * MEM_WIDTH * 2
           + 2 * D_MODEL * D_MODEL * 2 + 2 * D_MODEL * MEM_WIDTH * 2 + 2 * D_MODEL * LANES * 2 + 8 * tm * D_MODEL * 4)
    return pl.pallas_call(
        _cross_router_kernel,
        out_shape=[jax.ShapeDtypeStruct((bsz, s, D_MODEL), F32), jax.ShapeDtypeStruct((bsz, s, D_MODEL), BF16),
                   jax.ShapeDtypeStruct((bsz, s, LANES), I32), jax.ShapeDtypeStruct((bsz, s, LANES), F32),
                   jax.ShapeDtypeStruct((bsz, s // tm, SUBLANES, LANES), I32)],
        grid=(bsz, s // tm),
        in_specs=[tok, tok, tok, tok, _resident(wao.shape, fixed), _resident(wout.shape, fixed),
                  _resident((1, D_MODEL), fixed), _resident(wcq.shape, fixed), _resident((1, MEM_HEAD_DIM), fixed),
                  memb, memb, _resident(wco.shape, fixed), _resident((1, D_MODEL), fixed),
                  _resident(wrh.shape, fixed), _resident(wrl.shape, fixed), _resident((1, LANES), fixed)],
        out_specs=[tok, tok, small, small,
                   pl.BlockSpec((None, None, SUBLANES, LANES), lambda b, i: (b, i, 0, 0))],
        compiler_params=_params(("parallel", "parallel"), est),
        name="mix_cross_router",
    )(x, o, ga, ml, wao, wout, gcx, wcq, cqg, kc, vc, wco, gffn, wrh, wrl, br)


MOE_CHUNK = SUBLANES
PACKED = D_MODEL // 2
U32 = jnp.uint32


def _pack_rows(x):
    hi = lax.bitcast_convert_type(x[:, :PACKED].astype(BF16).astype(F32), U32)
    lo = lax.bitcast_convert_type(x[:, PACKED:].astype(BF16).astype(F32), U32)
    return hi | lax.shift_right_logical(lo, jnp.full_like(lo, 16))


def _unpack_rows(p):
    a = lax.bitcast_convert_type(p & jnp.full_like(p, 0xFFFF0000), F32)
    b = lax.bitcast_convert_type(lax.shift_left(p, jnp.full_like(p, 16)), F32)
    return jnp.concatenate([a, b], axis=1).astype(BF16)


def _local_rows(tm):
    return TOP_K * tm + N_EXPERTS * MOE_CHUNK


def _segment_loop(t, lst_ref, gofs_ref, nch_ref, fn):
    quad = 4 * MOE_CHUNK

    def per_expert(e, carry):
        k = t * N_EXPERTS + e
        lst, gofs, nch = lst_ref[k], gofs_ref[k], nch_ref[k]
        nquad = lax.shift_right_logical(nch, 2)

        def per_quad(c, cc):
            fn(pl.multiple_of(lst + c * quad, MOE_CHUNK), pl.multiple_of(gofs + c * quad, MOE_CHUNK), quad)
            return cc
        lax.fori_loop(0, nquad, per_quad, 0)
        done = nquad * quad

        @pl.when((nch & 2) == 2)
        def _():
            fn(pl.multiple_of(lst + done, MOE_CHUNK), pl.multiple_of(gofs + done, MOE_CHUNK), 2 * MOE_CHUNK)

        @pl.when((nch & 1) == 1)
        def _():
            rest = done + (nch & 2) * MOE_CHUNK
            fn(pl.multiple_of(lst + rest, MOE_CHUNK), pl.multiple_of(gofs + rest, MOE_CHUNK), MOE_CHUNK)
        return carry
    lax.fori_loop(0, N_EXPERTS, per_expert, 0)


def _repeat(count, fn):
    def body(c, carry):
        fn()
        return carry
    lax.fori_loop(0, count, body, 0)


MOE_WAIT_GROUP = 8


def _wait_chunks(count, wait_rows):
    _repeat(lax.shift_right_logical(count, 3), lambda: wait_rows(MOE_WAIT_GROUP * MOE_CHUNK))
    _repeat(count & (MOE_WAIT_GROUP - 1), lambda: wait_rows(MOE_CHUNK))


def _dispatch_kernel(gofs_ref, lst_ref, nch_ref, ntot_ref, tail_ref, tailn_ref, tailtot_ref,
                     hn_ref, eid_ref, lrow_ref, xs_hbm, before_ref, xloc_ref, zero_ref, sem, *, tm):
    t = pl.program_id(0)
    last = pl.num_programs(0) - 1
    slot = t % 2
    na = TOP_K * tm
    loc = xloc_ref.shape[1]

    @pl.when(t == 0)
    def _():
        r = lax.broadcasted_iota(I32, (na, na), 0)
        c = lax.broadcasted_iota(I32, (na, na), 1)
        before_ref[...] = (r < c).astype(BF16)
        zero_ref[...] = jnp.zeros_like(zero_ref)

    et = eid_ref[...].astype(F32).T
    e_row = jnp.concatenate([et[0:1], et[1:2]], axis=1)
    hit = lax.broadcasted_iota(I32, (LANES, na), 0).astype(F32) == e_row
    hit_b = hit.astype(BF16)
    rank = jnp.dot(hit_b, before_ref[...], preferred_element_type=F32)
    start = jnp.dot(lrow_ref[...].astype(BF16), hit_b, preferred_element_type=F32)[0:1] * MOE_CHUNK
    pos = (start + jnp.sum(jnp.where(hit, rank, 0.0), axis=0, keepdims=True)).astype(I32)
    r = lax.broadcasted_iota(I32, (loc, tm), 0)
    sel = ((r == pos[:, :tm]) | (r == pos[:, tm:])).astype(BF16)
    xloc_ref[slot] = _pack_rows(jnp.dot(sel, hn_ref[...], preferred_element_type=F32))

    def copy_out(local_row, global_row, s, rows=MOE_CHUNK):
        return pltpu.make_async_copy(xloc_ref.at[s, pl.ds(local_row, rows), :],
                                     xs_hbm.at[pl.ds(global_row, rows), :], sem.at[s])

    _segment_loop(t, lst_ref, gofs_ref, nch_ref, lambda lr, gr, rows: copy_out(lr, gr, slot, rows).start())

    @pl.when(t > 0)
    def _():
        _wait_chunks(ntot_ref[jnp.maximum(t - 1, 0)], lambda rows: copy_out(0, 0, 1 - slot, rows).wait())

    @pl.when(t == last)
    def _():
        _wait_chunks(ntot_ref[t], lambda rows: copy_out(0, 0, slot, rows).wait())

        def zero_out(global_row):
            return pltpu.make_async_copy(zero_ref, xs_hbm.at[pl.ds(global_row, MOE_CHUNK), :], sem.at[0])

        def per_expert(e, carry):
            def per_chunk(c, cc):
                zero_out(pl.multiple_of(tail_ref[e] + c * MOE_CHUNK, MOE_CHUNK)).start()
                return cc
            lax.fori_loop(0, tailn_ref[e], per_chunk, 0)
            return carry
        lax.fori_loop(0, N_EXPERTS, per_expert, 0)
        _repeat(tailtot_ref[0], lambda: zero_out(0).wait())


def _dispatch(plan, hn2d, eid2d, lrow, p_rows, tm):
    n = hn2d.shape[0]
    na = TOP_K * tm
    loc = _local_rows(tm)
    pre = (plan["gofs"], plan["lst"], plan["nch"], plan["ntot"], plan["tail"], plan["tailn"], plan["tailtot"])
    est = 2 * tm * D_MODEL * 2 + na * na * 2 + 2 * loc * PACKED * 4 + loc * D_MODEL * 8 + loc * tm * 4 \
        + 4 * LANES * na * 4
    grid_spec = pltpu.PrefetchScalarGridSpec(
        num_scalar_prefetch=len(pre),
        grid=(n // tm,),
        in_specs=[pl.BlockSpec((tm, D_MODEL), lambda t, *_: (t, 0)), pl.BlockSpec((tm, LANES), lambda t, *_: (t, 0)),
                  pl.BlockSpec((None, SUBLANES, LANES), lambda t, *_: (t, 0, 0))],
        out_specs=pl.BlockSpec(memory_space=pl.ANY),
        scratch_shapes=[pltpu.VMEM((na, na), BF16), pltpu.VMEM((2, loc, PACKED), U32),
                        pltpu.VMEM((MOE_CHUNK, PACKED), U32), pltpu.SemaphoreType.DMA((2,))],
    )
    return pl.pallas_call(
        functools.partial(_dispatch_kernel, tm=tm),
        out_shape=jax.ShapeDtypeStruct((p_rows, PACKED), U32),
        grid_spec=grid_spec,
        compiler_params=_params(("arbitrary",), est),
        name="moe_dispatch",
    )(*pre, hn2d, eid2d, lrow)


def _expert_kernel(blk_e_ref, blk_src_ref, blk_n_ref, xs_ref, wgu_ref, wd_ref, ys_ref, wgu_bf, wd_bf):
    i = pl.program_id(0)

    @pl.when(blk_n_ref[i] > 0)
    def _():
        @pl.when((i == 0) | (blk_e_ref[i] != blk_e_ref[jnp.maximum(i - 1, 0)]))
        def _():
            wgu_bf[...] = wgu_ref[0].astype(BF16)
            wd_bf[...] = wd_ref[0].astype(BF16)

        gu = jnp.dot(_unpack_rows(xs_ref[...]), wgu_bf[...], preferred_element_type=F32)
        act = (jax.nn.silu(gu[:, :EXPERT_FF]) * gu[:, EXPERT_FF:]).astype(BF16)
        ys_ref[...] = _pack_rows(jnp.dot(act, wd_bf[...], preferred_element_type=F32))


def _experts(plan, xs, wgu, wd, eb):
    n_blk = xs.shape[0] // eb
    rows = pl.BlockSpec((eb, PACKED), lambda i, be, bs, bn: (bs[i], 0))
    w_elems = wgu.shape[1] * wgu.shape[2] + wd.shape[1] * wd.shape[2]
    est = 2 * 2 * eb * PACKED * 4 + 2 * w_elems * 4 + w_elems * 2 + 6 * eb * D_MODEL * 4
    grid_spec = pltpu.PrefetchScalarGridSpec(
        num_scalar_prefetch=3,
        grid=(n_blk,),
        in_specs=[rows, pl.BlockSpec((1,) + wgu.shape[1:], lambda i, be, bs, bn: (be[i], 0, 0)),
                  pl.BlockSpec((1,) + wd.shape[1:], lambda i, be, bs, bn: (be[i], 0, 0))],
        out_specs=rows,
        scratch_shapes=[pltpu.VMEM(wgu.shape[1:], BF16), pltpu.VMEM(wd.shape[1:], BF16)],
    )
    return pl.pallas_call(
        _expert_kernel,
        out_shape=jax.ShapeDtypeStruct(xs.shape, U32),
        grid_spec=grid_spec,
        compiler_params=_params(("arbitrary",), est),
        name="experts",
    )(plan["blk_e"], plan["blk_src"], plan["blk_n"], xs, wgu, wd)


def _combine_kernel(gofs_ref, lst_ref, nch_ref, ntot_ref, x2_ref, ew_ref, eid_ref, lrow_ref, ys_hbm, o_ref,
                    before_ref, yloc_ref, sem, *, tm):
    t = pl.program_id(0)
    last = pl.num_programs(0) - 1
    slot = t % 2
    na = TOP_K * tm
    loc = yloc_ref.shape[1]

    def copy_in(local_row, global_row, s, rows=MOE_CHUNK):
        return pltpu.make_async_copy(ys_hbm.at[pl.ds(global_row, rows), :],
                                     yloc_ref.at[s, pl.ds(local_row, rows), :], sem.at[s])

    def fetch(tile, s):
        _segment_loop(tile, lst_ref, gofs_ref, nch_ref, lambda lr, gr, rows: copy_in(lr, gr, s, rows).start())

    @pl.when(t == 0)
    def _():
        r = lax.broadcasted_iota(I32, (na, na), 0)
        c = lax.broadcasted_iota(I32, (na, na), 1)
        before_ref[...] = (c < r).astype(BF16)
        yloc_ref[...] = jnp.zeros_like(yloc_ref)
        fetch(0, 0)

    @pl.when(t < last)
    def _():
        fetch(t + 1, 1 - slot)

    _wait_chunks(ntot_ref[t], lambda rows: copy_in(0, 0, slot, rows).wait())

    eid = eid_ref[...]
    lane = lax.broadcasted_iota(I32, (tm, LANES), 1)
    hit = jnp.concatenate([lane == eid[:, 0:1], lane == eid[:, 1:2]], axis=0)
    rank = jnp.dot(before_ref[...], hit.astype(BF16), preferred_element_type=F32)
    start = lrow_ref[0:1, :] * MOE_CHUNK
    pos = jnp.sum(jnp.where(hit, rank + start, 0.0), axis=1, keepdims=True).astype(I32)
    col = lax.broadcasted_iota(I32, (tm, loc), 1)
    ew = ew_ref[...]
    sel = jnp.where(col == pos[:tm], ew[:, 0:1], 0.0) + jnp.where(col == pos[tm:], ew[:, 1:2], 0.0)
    o_ref[...] = x2_ref[...] + jnp.dot(sel.astype(BF16), _unpack_rows(yloc_ref[slot]), preferred_element_type=F32)


def _combine(plan, x2, ew, eid2d, lrow, ys, tm):
    n = x2.shape[0]
    na = TOP_K * tm
    loc = _local_rows(tm)
    pre = (plan["gofs"], plan["lst"], plan["nch"], plan["ntot"])
    tok = pl.BlockSpec((tm, D_MODEL), lambda t, *_: (t, 0))
    small = pl.BlockSpec((tm, LANES), lambda t, *_: (t, 0))
    est = 2 * 2 * tm * D_MODEL * 4 + na * na * 2 + 2 * loc * PACKED * 4 + loc * D_MODEL * 10 + 2 * tm * loc * 4 \
        + 3 * tm * D_MODEL * 4
    grid_spec = pltpu.PrefetchScalarGridSpec(
        num_scalar_prefetch=len(pre),
        grid=(n // tm,),
        in_specs=[tok, small, small, pl.BlockSpec((None, SUBLANES, LANES), lambda t, *_: (t, 0, 0)),
                  pl.BlockSpec(memory_space=pl.ANY)],
        out_specs=tok,
        scratch_shapes=[pltpu.VMEM((na, na), BF16), pltpu.VMEM((2, loc, PACKED), U32),
                        pltpu.SemaphoreType.DMA((2,))],
    )
    return pl.pallas_call(
        functools.partial(_combine_kernel, tm=tm),
        out_shape=jax.ShapeDtypeStruct((n, D_MODEL), F32),
        grid_spec=grid_spec,
        compiler_params=_params(("arbitrary",), est),
        name="moe_combine",
    )(*pre, x2, ew, eid2d, lrow, ys)


def _rope_table(positions):
    inv_freq = jnp.exp(-math.log(ROPE_THETA) * jnp.arange(ROT_HALF, dtype=F32) / ROT_HALF)
    ang = positions.astype(F32).reshape(-1, 1) * inv_freq
    seg = jnp.concatenate([jnp.cos(ang), jnp.sin(ang), jnp.zeros((ang.shape[0], HEAD_DIM - ROT_DIMS), F32)], axis=1)
    return jnp.tile(seg, (1, LANES // HEAD_DIM))


def _plan_rows(n_tok, tm, eb):
    worst = n_tok * TOP_K + (n_tok // tm) * N_EXPERTS * (MOE_CHUNK - 1)
    return (worst + eb - 1) // eb * eb + N_EXPERTS * eb


def _moe_plan(hist, p_rows, eb):
    n_tiles = hist.shape[0]
    seg = (hist + MOE_CHUNK - 1) // MOE_CHUNK * MOE_CHUNK
    tot = jnp.sum(seg, axis=0)
    region = (tot + eb - 1) // eb * eb
    pend = jnp.cumsum(region)
    pstart = pend - region
    gofs = pstart[None, :] + jnp.cumsum(seg, axis=0) - seg
    lst = jnp.cumsum(seg, axis=1) - seg
    nch = seg // MOE_CHUNK
    n_blk = p_rows // eb
    blk_first = jnp.arange(n_blk, dtype=I32) * eb
    blk_e = jnp.minimum(jnp.sum(blk_first[:, None] >= pend[None, :], axis=1), N_EXPERTS - 1).astype(I32)
    blk_n = jnp.clip(pstart[blk_e] + tot[blk_e] - blk_first, 0, eb)
    blk_src = jnp.minimum(jnp.arange(n_blk, dtype=I32), jnp.maximum(pend[-1] // eb - 1, 0))
    tailn = (region - tot) // MOE_CHUNK
    lrow = jnp.pad((lst // MOE_CHUNK).astype(F32), ((0, 0), (0, LANES - N_EXPERTS)))
    flat = lambda a: a.reshape(-1).astype(I32)
    plan = dict(gofs=flat(gofs), lst=flat(lst), nch=flat(nch), ntot=flat(jnp.sum(nch, axis=1)),
                tail=flat(pstart + tot), tailn=flat(tailn), tailtot=flat(jnp.sum(tailn)),
                blk_e=blk_e, blk_src=flat(blk_src), blk_n=flat(blk_n))
    return plan, jnp.broadcast_to(lrow[:, None, :], (n_tiles, SUBLANES, LANES))


def _tile(n, pref):
    t = min(n, pref)
    assert n % t == 0, (n, pref)
    return t


def kernel(x, mem, positions, norm_mix, w_in, conv_w, conv_b, lru_wa, lru_ba, lru_wx, lru_bx, lru_lambda, w_lru_o, q_norm, k_norm, lambda_q1, lambda_k1, lambda_q2, lambda_k2, subln, w_attn_o, w_out, norm_cx, norm_mem, w_cq, w_ckv, cq_norm, ck_norm, w_co, norm_ffn, w_group, b_group, w_router, b_router, w_gate_up, w_down):
    bsz, s, d = x.shape
    assert d == D_MODEL and w_in.shape[-1] == N_PROJ * D_MODEL
    n = bsz * s
    depth = w_in.shape[0]
    tm = _tile(n, 512)
    ts = _tile(s, 512)
    tq = _tile(s, 512)
    assert tq % CHUNK == 0 and ts % SUBLANES == 0
    eb = _tile(n, 512)
    rope = _rope_table(positions)
    row = lambda v: v.reshape(1, -1).astype(F32)
    rep = LANES // HEAD_DIM

    for layer in range(depth):
        lambda_init = 0.8 - 0.6 * math.exp(-0.3 * layer)
        wax = (0.5 * jnp.concatenate([lru_wa[layer], lru_wx[layer]], axis=-1)).astype(BF16)
        bax = 0.5 * jnp.stack([lru_ba[layer], lru_bx[layer]]).astype(F32)
        lru_params = (conv_w[layer].astype(F32), row(conv_b[layer]), wax, bax, row(lru_lambda[layer]),
                      w_lru_o[layer].astype(BF16))
        ml, q, k, vt, ga = _in_proj(
            x.reshape(n, d), row(norm_mix[layer]), w_in[layer].astype(BF16),
            jnp.tile(row(q_norm[layer]), (1, rep)), jnp.tile(row(k_norm[layer]), (1, rep)), rope, lru_params, ts, s)
        seq = lambda a: a.reshape(bsz, s, d)

        lam = (jnp.exp(jnp.sum(lambda_q1[layer].astype(F32) * lambda_k1[layer].astype(F32)))
               - jnp.exp(jnp.sum(lambda_q2[layer].astype(F32) * lambda_k2[layer].astype(F32))) + lambda_init)
        o = _diff_attn(lam.reshape(1, 1), seq(q), seq(k), vt, subln[layer].astype(F32).reshape(V_DIM, 1),
                       lambda_init, tq)

        kc, vc = _mem_kv(mem, row(norm_mem[layer]), w_ckv[layer].astype(BF16), row(ck_norm[layer]))
        w_r = jnp.concatenate([w_group[layer], w_router[layer],
                               jnp.zeros((d, LANES - N_GROUPS - N_EXPERTS), F32)], axis=1).astype(F32)
        b_r = jnp.concatenate([b_group[layer], b_router[layer],
                               jnp.zeros((LANES - N_GROUPS - N_EXPERTS,), F32)]).reshape(1, LANES).astype(F32)
        wrh, wrl = _split_bf16(w_r)
        x2, hn, eid, ew, hist = _cross_router(
            x, o, seq(ga), seq(ml), w_attn_o[layer].astype(BF16), w_out[layer].astype(BF16),
            row(norm_cx[layer]), w_cq[layer].astype(BF16), row(cq_norm[layer]), kc, vc,
            w_co[layer].astype(BF16), row(norm_ffn[layer]), wrh, wrl, b_r, ts)

        p_rows = _plan_rows(n, ts, eb)
        plan, lrow = _moe_plan(hist[:, :, 0, :N_EXPERTS].reshape(n // ts, N_EXPERTS), p_rows, eb)
        eid2d = eid.reshape(n, LANES)
        xs = _dispatch(plan, hn.reshape(n, d), eid2d, lrow, p_rows, ts)
        ys = _experts(plan, xs, w_gate_up[layer], w_down[layer], eb)
        x = _combine(plan, x2.reshape(n, d), ew.reshape(n, LANES), eid2d, lrow, ys, ts).reshape(bsz, s, d)
    return x
```

```python
import functools
import math

import jax
import jax.numpy as jnp
from jax import lax
from jax.experimental import pallas as pl
from jax.experimental.pallas import tpu as pltpu

F32 = jnp.float32
BF16 = jnp.bfloat16
I32 = jnp.int32

D_MODEL = 1024
CHUNK = 64
LRU_BLOCKS = 8
LRU_BLOCK_WIDTH = D_MODEL // LRU_BLOCKS
CONV_WIDTH = 4
LRU_C = 8.0
ATTN_HEADS = 8
HEAD_DIM = 64
V_DIM = 2 * HEAD_DIM
ROPE_THETA = 500000.0
ROT_DIMS = HEAD_DIM // 4
ROT_HALF = ROT_DIMS // 2
MEM_HEADS = 4
MEM_HEAD_DIM = 128
MEM_WIDTH = MEM_HEADS * MEM_HEAD_DIM
N_GROUPS = 4
EXPERTS_PER_GROUP = 8
N_EXPERTS = N_GROUPS * EXPERTS_PER_GROUP
TOP_K = 2
EXPERT_FF = 512
N_PROJ = 7
EPS = 1e-6
NEG_INF = -1e30
LOG2_E = math.log2(math.e)
ATTN_COL_GROUP = 512
ATTN_HEADS_PER_STEP = 4
TOKEN_TILE = 512
EXPERT_BLOCK = 512

LANES = 128
SUBLANES = 8
V7X_VMEM_BYTES = 64 * 1024 * 1024
MIB = 1024 * 1024


def _vmem_limit(estimate_bytes):
    return int(min(max(estimate_bytes * 3 // 2, 16 * MIB), V7X_VMEM_BYTES - 8 * MIB))


def _params(semantics, vmem_estimate):
    return pltpu.CompilerParams(dimension_semantics=semantics, vmem_limit_bytes=_vmem_limit(vmem_estimate))


def _resident(shape, index_map):
    return pl.BlockSpec(shape, index_map, pipeline_mode=pl.Buffered(1))


def _rms(x, g):
    return x * lax.rsqrt(jnp.mean(x * x, axis=-1, keepdims=True) + EPS) * g


def _sigmoid(x):
    return 0.5 * jnp.tanh(0.5 * x) + 0.5


def _segment_ones():
    r = lax.broadcasted_iota(I32, (LANES, LANES), 0) // HEAD_DIM
    c = lax.broadcasted_iota(I32, (LANES, LANES), 1) // HEAD_DIM
    return (r == c).astype(BF16)


def _qk_post(p, gain, cos_t, sin_lo, sin_hi, seg, scale):
    cols = []
    for c in range(D_MODEL // LANES):
        pc = p[:, c * LANES:(c + 1) * LANES]
        ss = jnp.dot((pc * pc).astype(BF16), seg, preferred_element_type=F32)
        y = pc * lax.rsqrt(ss * (1.0 / HEAD_DIM) + EPS) * gain
        y = y * cos_t + pltpu.roll(y, LANES - ROT_HALF, 1) * sin_lo + pltpu.roll(y, ROT_HALF, 1) * sin_hi
        cols.append((y * scale).astype(BF16))
    return jnp.concatenate(cols, axis=1)


def _in_proj_kernel(x_ref, g_ref, w_ref, qg_ref, kg_ref, rope_ref, cw_ref, cb_ref, wax_ref, bax_ref, lam_ref, wo_ref,
                    ml_ref, q_ref, k_ref, vt_ref, ga_ref, xpad_ref, hprev_ref, *, per_seq):
    @pl.when(pl.program_id(0) % per_seq == 0)
    def _():
        xpad_ref[0:SUBLANES, :] = jnp.zeros((SUBLANES, D_MODEL), F32)
        hprev_ref[...] = jnp.zeros_like(hprev_ref)

    h = _rms(x_ref[...], g_ref[...]).astype(BF16)

    def proj(j):
        return jnp.dot(h, w_ref[:, j * D_MODEL:(j + 1) * D_MODEL], preferred_element_type=F32)

    xc, pre_a, pre_x = _lru_conv_gates(proj(0), cw_ref, cb_ref, wax_ref, xpad_ref)
    seg = _segment_ones()
    tab = rope_ref[...]
    seg_lane = lax.broadcasted_iota(I32, tab.shape, 1) % HEAD_DIM
    first, second = seg_lane < ROT_HALF, (seg_lane >= ROT_HALF) & (seg_lane < ROT_DIMS)
    cos_t = jnp.where(first, tab, jnp.where(second, pltpu.roll(tab, ROT_HALF, 1), 1.0))
    sin_lo = jnp.where(first, -pltpu.roll(tab, LANES - ROT_HALF, 1), 0.0)
    sin_hi = jnp.where(second, tab, 0.0)
    q_ref[...] = _qk_post(proj(2), qg_ref[...], cos_t, sin_lo, sin_hi, seg, HEAD_DIM ** -0.5 * LOG2_E)
    k_ref[...] = _qk_post(proj(3), kg_ref[...], cos_t, sin_lo, sin_hi, seg, 1.0)
    hr = _lru_scan(xc, pre_a, pre_x, bax_ref, lam_ref, hprev_ref)
    gate, merge_gate = proj(1), proj(5)
    v = proj(4)
    for hd in range(ATTN_HEADS):
        vt_ref[hd * V_DIM:(hd + 1) * V_DIM, :] = v[:, hd * V_DIM:(hd + 1) * V_DIM].T.astype(BF16)
    ml_ref[...] = _lru_out(hr, gate, merge_gate, wo_ref)
    ga_ref[...] = proj(6).astype(BF16)


def _in_proj(x2d, g, w_in, qg, kg, rope, lru_params, tm, seq_len):
    n = x2d.shape[0]
    per_seq = seq_len // tm
    row = lambda i: (i, 0)
    tok = pl.BlockSpec((tm, D_MODEL), row)
    tok_out = jax.ShapeDtypeStruct((n, D_MODEL), BF16)
    vt_out = jax.ShapeDtypeStruct((n // seq_len, D_MODEL, seq_len), BF16)
    vt_spec = pl.BlockSpec((None, D_MODEL, tm), lambda i: (i // per_seq, 0, i % per_seq))
    whole = lambda a: _resident(a.shape, lambda i: (0,) * a.ndim)
    consts = (g, w_in, qg, kg)
    est = (sum(a.size * a.dtype.itemsize for a in consts + tuple(lru_params)) + 2 * tm * D_MODEL * 4
           + 2 * tm * LANES * 4 + 5 * 2 * tm * D_MODEL * 2 + 16 * tm * D_MODEL * 4)
    return pl.pallas_call(
        functools.partial(_in_proj_kernel, per_seq=per_seq),
        out_shape=[tok_out] * 3 + [vt_out, tok_out],
        grid=(n // tm,),
        in_specs=[tok] + [whole(a) for a in consts] + [pl.BlockSpec((tm, LANES), row)]
        + [whole(a) for a in lru_params],
        out_specs=[tok] * 3 + [vt_spec, tok],
        scratch_shapes=[pltpu.VMEM((tm + SUBLANES, D_MODEL), F32), pltpu.VMEM((SUBLANES, D_MODEL), F32)],
        compiler_params=_params(("arbitrary",), est),
        name="in_proj_lru",
    )(x2d, *consts, rope, *lru_params)


def _lru_conv_gates(x, cw_ref, cb_ref, wax_ref, xpad_ref):
    tt = x.shape[0]
    xpad_ref[SUBLANES:SUBLANES + tt, :] = x
    cw = cw_ref[...]
    xc = cb_ref[...] + cw[3:4] * x
    for j in range(1, CONV_WIDTH):
        xc = xc + cw[CONV_WIDTH - 1 - j:CONV_WIDTH - j] * xpad_ref[SUBLANES - j:SUBLANES - j + tt, :]
    xpad_ref[0:SUBLANES, :] = x[tt - SUBLANES:tt]

    xcb = xc.astype(BF16)
    ra, ri = [], []
    for n in range(LRU_BLOCKS):
        g = jnp.dot(xcb[:, n * LRU_BLOCK_WIDTH:(n + 1) * LRU_BLOCK_WIDTH], wax_ref[n], preferred_element_type=F32)
        ra.append(g[:, :LRU_BLOCK_WIDTH])
        ri.append(g[:, LRU_BLOCK_WIDTH:])
    return xc, jnp.concatenate(ra, axis=1), jnp.concatenate(ri, axis=1)


def _lru_scan(xc, pre_a, pre_x, bax_ref, lam_ref, hprev_ref):
    nblk = xc.shape[0] // SUBLANES
    bax = bax_ref[...]
    t_r = jnp.tanh(pre_a + bax[0:1])
    i = 0.5 * jnp.tanh(pre_x + bax[1:2]) + 0.5
    u = (0.5 * LRU_C) * jax.nn.softplus(-lam_ref[...]) * (t_r + 1.0)
    a = jnp.exp(-u)
    gain2 = jnp.tanh(u) * (a * a + 1.0)
    b = jnp.where(gain2 > 0.0, gain2 * lax.rsqrt(gain2), 0.0) * i * xc

    a3 = a.reshape(nblk, SUBLANES, D_MODEL)
    b3 = b.reshape(nblk, SUBLANES, D_MODEL)
    sub = lax.broadcasted_iota(I32, (nblk, SUBLANES, D_MODEL), 1)
    shift = 1
    while shift < SUBLANES:
        keep = sub >= shift
        a_sh = pltpu.roll(a3, shift, 1)
        b_sh = pltpu.roll(b3, shift, 1)
        b3 = jnp.where(keep, a3 * b_sh + b3, b3)
        a3 = jnp.where(keep, a3 * a_sh, a3)
        shift *= 2
    h_last = hprev_ref[...]
    groups = []
    for blk in range(nblk):
        hb = a3[blk] * h_last + b3[blk]
        groups.append(hb)
        h_last = jnp.broadcast_to(hb[SUBLANES - 1:SUBLANES], (SUBLANES, D_MODEL))
    hprev_ref[...] = h_last
    return jnp.concatenate(groups, axis=0)


def _lru_out(hr, gate, merge_gate, wo_ref):
    y = (jax.nn.gelu(gate) * hr).astype(BF16)
    yl = jnp.dot(y, wo_ref[...], preferred_element_type=F32)
    return (_sigmoid(merge_gate) * yl).astype(BF16)


def _attn_kernel(lam_ref, q_ref, qnext_ref, k_ref, vt_ref, sub_ref, o_ref, m_ref, l_ref, acc_ref, qz_ref, sa_ref,
                 sb_ref, mxa_ref, mxb_ref, *, tq, out_scale):
    i = pl.program_id(2)
    last = pl.num_programs(2) - 1
    heads = range(ATTN_HEADS_PER_STEP)
    m_ref[...] = jnp.full_like(m_ref, NEG_INF)
    l_ref[...] = jnp.zeros_like(l_ref)
    acc_ref[...] = jnp.zeros_like(acc_ref)
    cw = ATTN_COL_GROUP

    def load_queries(src_ref):
        row = lax.broadcasted_iota(I32, (V_DIM, tq), 0)
        for h in heads:
            qt = src_ref[:, h * V_DIM:(h + 1) * V_DIM].astype(F32).T
            zero = jnp.zeros_like(qt)
            qz_ref[h] = jnp.concatenate([jnp.where(row < HEAD_DIM, qt, zero), jnp.where(row >= HEAD_DIM, qt, zero)],
                                        axis=1).astype(BF16)

    def block_off(j):
        return pl.multiple_of(j * tq, tq)

    buf_a, buf_b = (sa_ref, mxa_ref), (sb_ref, mxb_ref)

    def scores(j, buf):
        s_ref, mx_ref = buf
        for h in heads:
            kb = k_ref[pl.ds(block_off(j), tq), h * V_DIM:(h + 1) * V_DIM]
            s = jnp.dot(kb, qz_ref[h], preferred_element_type=F32)
            s_ref[h] = s
            mx_ref[h] = jnp.max(s, axis=0, keepdims=True)

    def softmax_pv(j, buf, diagonal):
        s_ref, mx_ref = buf
        for h in heads:
            vtb = vt_ref[h * V_DIM:(h + 1) * V_DIM, pl.ds(block_off(j), tq)]
            for g in range(2 * tq // cw):
                cols = pl.ds(g * cw, cw)
                s = s_ref[h, :, cols]
                if diagonal:
                    key = lax.broadcasted_iota(I32, (tq, cw), 0)
                    qry = (lax.broadcasted_iota(I32, (tq, cw), 1) + g * cw) % tq
                    s = jnp.where((key // CHUNK) <= (qry // CHUNK), s, NEG_INF)
                    blk_max = jnp.max(s, axis=0, keepdims=True)
                else:
                    blk_max = mx_ref[h, :, cols]
                m_prev = m_ref[h, :, cols]
                m_new = jnp.maximum(m_prev, blk_max)
                alpha = jnp.exp2(m_prev - m_new)
                p = jnp.exp2(s - m_new)
                l_ref[h, :, cols] = alpha * l_ref[h, :, cols] + jnp.sum(p, axis=0, keepdims=True)
                acc_ref[h, :, cols] = alpha * acc_ref[h, :, cols] + jnp.dot(vtb, p.astype(BF16),
                                                                            preferred_element_type=F32)
                m_ref[h, :, cols] = m_new

    @pl.when(i == 0)
    def _():
        load_queries(q_ref)
        scores(0, buf_a)

    def pair(p, carry):
        j = 2 * p
        scores(j + 1, buf_b)
        softmax_pv(j, buf_a, False)
        scores(j + 2, buf_a)
        softmax_pv(j + 1, buf_b, False)
        return carry

    lax.fori_loop(0, i // 2, pair, 0)

    @pl.when(i % 2 == 1)
    def _():
        scores(i, buf_b)
        softmax_pv(i - 1, buf_a, False)
        softmax_pv(i, buf_b, True)

    @pl.when(i % 2 == 0)
    def _():
        softmax_pv(i, buf_a, True)

    def write_output():
        for h in heads:
            o12 = acc_ref[h] * (1.0 / l_ref[h])
            ot = o12[:, :tq] - lam_ref[0, 0] * o12[:, tq:]
            ot = ot * lax.rsqrt(jnp.mean(ot * ot, axis=0, keepdims=True) + EPS) * sub_ref[...] * out_scale
            o_ref[:, h * V_DIM:(h + 1) * V_DIM] = ot.T.astype(BF16)

    @pl.when(i < last)
    def _():
        load_queries(qnext_ref)
        scores(0, buf_a)
        write_output()

    @pl.when(i == last)
    def _():
        write_output()


def _diff_attn(lam, q, k, vt, sub, lambda_init, tq):
    bsz, s, _ = q.shape
    hps = ATTN_HEADS_PER_STEP
    width = hps * V_DIM
    nq = s // tq
    qspec = pl.BlockSpec((None, tq, width), lambda b, h, i: (b, i, h))
    qnext_spec = pl.BlockSpec((None, tq, width), lambda b, h, i: (b, jnp.minimum(i + 1, nq - 1), h))
    kspec = pl.BlockSpec((None, s, width), lambda b, h, i: (b, 0, h))
    vtspec = pl.BlockSpec((None, width, s), lambda b, h, i: (b, h, 0))
    est = 2 * 2 * s * width * 2 + 4 * tq * width * 2 + hps * (V_DIM * 2 * tq * 6 + 2 * tq * 2 * tq * 4) \
        + 3 * 2 * tq * tq * 4
    return pl.pallas_call(
        functools.partial(_attn_kernel, tq=tq, out_scale=1.0 - lambda_init),
        out_shape=jax.ShapeDtypeStruct((bsz, s, ATTN_HEADS * V_DIM), BF16),
        grid=(bsz, ATTN_HEADS // hps, nq),
        in_specs=[pl.BlockSpec(memory_space=pltpu.SMEM), qspec, qnext_spec, kspec, vtspec,
                  pl.BlockSpec((V_DIM, 1), lambda b, h, i: (0, 0))],
        out_specs=qspec,
        scratch_shapes=[pltpu.VMEM((hps, 1, 2 * tq), F32), pltpu.VMEM((hps, 1, 2 * tq), F32),
                        pltpu.VMEM((hps, V_DIM, 2 * tq), F32), pltpu.VMEM((hps, V_DIM, 2 * tq), BF16),
                        pltpu.VMEM((hps, tq, 2 * tq), F32), pltpu.VMEM((hps, tq, 2 * tq), F32),
                        pltpu.VMEM((hps, 1, 2 * tq), F32), pltpu.VMEM((hps, 1, 2 * tq), F32)],
        compiler_params=_params(("parallel", "parallel", "arbitrary"), est),
        name="diff_attn",
    )(lam, q, q, k, vt, sub)


def _mix_out(x, o, ga, ml, wao_ref, wout_ref):
    ya = jnp.dot(o, wao_ref[...], preferred_element_type=F32)
    mixed = ml.astype(F32) + _sigmoid(ga.astype(F32)) * ya
    return x + jnp.dot(mixed.astype(BF16), wout_ref[...], preferred_element_type=F32)


def _mem_kv_kernel(mem_ref, g_ref, w_ref, ckg_ref, k_ref, v_ref):
    h = _rms(mem_ref[...], g_ref[...]).astype(BF16)
    kv = jnp.dot(h, w_ref[...], preferred_element_type=F32)
    ks = [_rms(kv[:, hd * MEM_HEAD_DIM:(hd + 1) * MEM_HEAD_DIM], ckg_ref[...]) for hd in range(MEM_HEADS)]
    k_ref[...] = jnp.concatenate(ks, axis=1).astype(BF16)
    v_ref[...] = kv[:, MEM_WIDTH:].astype(BF16)


def _mem_kv(mem, g, w, ckg):
    bsz, m, _ = mem.shape
    fixed = lambda b: (0, 0)
    out = pl.BlockSpec((None, m, MEM_WIDTH), lambda b: (b, 0, 0))
    est = 2 * m * D_MODEL * 4 + w.size * 2 + 4 * m * MEM_WIDTH * 2 + 4 * m * D_MODEL * 4
    return pl.pallas_call(
        _mem_kv_kernel,
        out_shape=[jax.ShapeDtypeStruct((bsz, m, MEM_WIDTH), BF16)] * 2,
        grid=(bsz,),
        in_specs=[pl.BlockSpec((None, m, D_MODEL), lambda b: (b, 0, 0)), _resident((1, D_MODEL), fixed),
                  _resident(w.shape, fixed), _resident((1, MEM_HEAD_DIM), fixed)],
        out_specs=[out, out],
        compiler_params=_params(("parallel",), est),
        name="mem_kv",
    )(mem, g, w, ckg)


def _split_bf16(x):
    hi = x.astype(BF16)
    return hi, (x - hi.astype(F32)).astype(BF16)


def _cross_router_kernel(x_ref, o_ref, ga_ref, ml_ref, wao_ref, wout_ref, gcx_ref, wcq_ref, cqg_ref, kc_ref, vc_ref,
                         wco_ref, gffn_ref, wrh_ref, wrl_ref, br_ref, x2_ref, hn_ref, eid_ref, ew_ref, hist_ref):
    x1 = _mix_out(x_ref[...], o_ref[...], ga_ref[...], ml_ref[...], wao_ref, wout_ref)
    q = jnp.dot(_rms(x1, gcx_ref[...]).astype(BF16), wcq_ref[...], preferred_element_type=F32)
    outs = []
    for hd in range(MEM_HEADS):
        sl = slice(hd * MEM_HEAD_DIM, (hd + 1) * MEM_HEAD_DIM)
        qh = _rms(q[:, sl], cqg_ref[...]) * MEM_HEAD_DIM ** -0.5
        s = lax.dot_general(qh.astype(BF16), kc_ref[:, sl], (((1,), (1,)), ((), ())), preferred_element_type=F32)
        p = jnp.exp(s - jnp.max(s, axis=1, keepdims=True))
        o = jnp.dot(p.astype(BF16), vc_ref[:, sl], preferred_element_type=F32)
        outs.append(o / jnp.sum(p, axis=1, keepdims=True))
    x2 = x1 + jnp.dot(jnp.concatenate(outs, axis=1).astype(BF16), wco_ref[...], preferred_element_type=F32)
    x2_ref[...] = x2

    hn = _rms(x2, gffn_ref[...])
    hn_ref[...] = hn.astype(BF16)
    h_hi, h_lo = _split_bf16(hn)
    logits = (jnp.dot(h_hi, wrh_ref[...], preferred_element_type=F32)
              + jnp.dot(h_lo, wrh_ref[...], preferred_element_type=F32)
              + jnp.dot(h_hi, wrl_ref[...], preferred_element_type=F32)) + br_ref[...]
    lane = lax.broadcasted_iota(I32, logits.shape, 1)
    is_group = lane < N_GROUPS
    gl = jnp.where(is_group, logits, NEG_INF)
    gmax = jnp.max(gl, axis=1, keepdims=True)
    gval = 1.0 / jnp.sum(jnp.where(is_group, jnp.exp(gl - gmax), 0.0), axis=1, keepdims=True)
    gidx = jnp.min(jnp.where(gl == gmax, lane, LANES), axis=1, keepdims=True)
    lane_group = lax.shift_right_logical(lane + (EXPERTS_PER_GROUP - N_GROUPS), 3) - 1
    chosen = lane_group == gidx
    el = jnp.where(chosen, logits, NEG_INF)
    v1 = jnp.max(el, axis=1, keepdims=True)
    i1 = jnp.min(jnp.where(chosen & (el == v1), lane, LANES), axis=1, keepdims=True)
    rest = chosen & (lane != i1)
    el2 = jnp.where(rest, logits, NEG_INF)
    v2 = jnp.max(el2, axis=1, keepdims=True)
    i2 = jnp.min(jnp.where(rest & (el2 == v2), lane, LANES), axis=1, keepdims=True)
    t = jnp.exp(v2 - v1)
    w1 = gval / (1.0 + t)
    w2 = gval * t / (1.0 + t)
    e1, e2 = i1 - N_GROUPS, i2 - N_GROUPS
    eid_ref[...] = jnp.where(lane == 0, e1, jnp.where(lane == 1, e2, 0))
    ew_ref[...] = jnp.where(lane == 0, w1, jnp.where(lane == 1, w2, 0.0))
    count = jnp.sum((lane == e1).astype(I32) + (lane == e2).astype(I32), axis=0, keepdims=True)
    hist_ref[...] = jnp.broadcast_to(count, hist_ref.shape)


def _cross_router(x, o, ga, ml, wao, wout, gcx, wcq, cqg, kc, vc, wco, gffn, wrh, wrl, br, tm):
    bsz, s, _ = x.shape
    m = kc.shape[1]
    tok = pl.BlockSpec((None, tm, D_MODEL), lambda b, i: (b, i, 0))
    small = pl.BlockSpec((None, tm, LANES), lambda b, i: (b, i, 0))
    memb = pl.BlockSpec((None, m, MEM_WIDTH), lambda b, i: (b, 0, 0))
    fixed = lambda b, i: (0, 0)
    est = (3 * 2 * tm * D_MODEL * 4 + 4 * 2 * tm * D_MODEL * 2 + 2 * 2 * tm * LANES * 4 + 2 * 2 * m * MEM_WIDTH * 2
           + 2 * D_MODEL * D_MODEL * 2 + 2 * D_MODEL * MEM_WIDTH * 2 + 2 * D_MODEL * LANES * 2 + 8 * tm * D_MODEL * 4)
    return pl.pallas_call(
        _cross_router_kernel,
        out_shape=[jax.ShapeDtypeStruct((bsz, s, D_MODEL), F32), jax.ShapeDtypeStruct((bsz, s, D_MODEL), BF16),
                   jax.ShapeDtypeStruct((bsz, s, LANES), I32), jax.ShapeDtypeStruct((bsz, s, LANES), F32),
                   jax.ShapeDtypeStruct((bsz, s // tm, SUBLANES, LANES), I32)],
        grid=(bsz, s // tm),
        in_specs=[tok, tok, tok, tok, _resident(wao.shape, fixed), _resident(wout.shape, fixed),
                  _resident((1, D_MODEL), fixed), _resident(wcq.shape, fixed), _resident((1, MEM_HEAD_DIM), fixed),
                  memb, memb, _resident(wco.shape, fixed), _resident((1, D_MODEL), fixed),
                  _resident(wrh.shape, fixed), _resident(wrl.shape, fixed), _resident((1, LANES), fixed)],
        out_specs=[tok, tok, small, small,
                   pl.BlockSpec((None, None, SUBLANES, LANES), lambda b, i: (b, i, 0, 0))],
        compiler_params=_params(("parallel", "parallel"), est),
        name="mix_cross_router",
    )(x, o, ga, ml, wao, wout, gcx, wcq, cqg, kc, vc, wco, gffn, wrh, wrl, br)


MOE_CHUNK = SUBLANES
PACKED = D_MODEL // 2
U32 = jnp.uint32


def _pack_rows(x):
    hi = lax.bitcast_convert_type(x[:, :PACKED].astype(BF16).astype(F32), U32)
    lo = lax.bitcast_convert_type(x[:, PACKED:].astype(BF16).astype(F32), U32)
    return hi | lax.shift_right_logical(lo, jnp.full_like(lo, 16))


def _unpack_rows(p):
    a = lax.bitcast_convert_type(p & jnp.full_like(p, 0xFFFF0000), F32)
    b = lax.bitcast_convert_type(lax.shift_left(p, jnp.full_like(p, 16)), F32)
    return jnp.concatenate([a, b], axis=1).astype(BF16)


def _local_rows(tm):
    return TOP_K * tm + N_EXPERTS * MOE_CHUNK


def _segment_loop(t, lst_ref, gofs_ref, nch_ref, fn):
    quad = 4 * MOE_CHUNK

    def per_expert(e, carry):
        k = t * N_EXPERTS + e
        lst, gofs, nch = lst_ref[k], gofs_ref[k], nch_ref[k]
        nquad = lax.shift_right_logical(nch, 2)

        def per_quad(c, cc):
            fn(pl.multiple_of(lst + c * quad, MOE_CHUNK), pl.multiple_of(gofs + c * quad, MOE_CHUNK), quad)
            return cc
        lax.fori_loop(0, nquad, per_quad, 0)
        done = nquad * quad

        @pl.when((nch & 2) == 2)
        def _():
            fn(pl.multiple_of(lst + done, MOE_CHUNK), pl.multiple_of(gofs + done, MOE_CHUNK), 2 * MOE_CHUNK)

        @pl.when((nch & 1) == 1)
        def _():
            rest = done + (nch & 2) * MOE_CHUNK
            fn(pl.multiple_of(lst + rest, MOE_CHUNK), pl.multiple_of(gofs + rest, MOE_CHUNK), MOE_CHUNK)
        return carry
    lax.fori_loop(0, N_EXPERTS, per_expert, 0)


def _repeat(count, fn):
    def body(c, carry):
        fn()
        return carry
    lax.fori_loop(0, count, body, 0)


MOE_WAIT_GROUP = 8


def _wait_chunks(count, wait_rows):
    _repeat(lax.shift_right_logical(count, MOE_WAIT_GROUP.bit_length() - 1),
            lambda: wait_rows(MOE_WAIT_GROUP * MOE_CHUNK))
    _repeat(count & (MOE_WAIT_GROUP - 1), lambda: wait_rows(MOE_CHUNK))


def _dispatch_kernel(gofs_ref, lst_ref, nch_ref, ntot_ref, tail_ref, tailn_ref, tailtot_ref,
                     hn_ref, eid_ref, lrow_ref, xs_hbm, before_ref, xloc_ref, zero_ref, sem, *, tm):
    t = pl.program_id(0)
    last = pl.num_programs(0) - 1
    slot = t % 2
    na = TOP_K * tm
    loc = xloc_ref.shape[1]

    @pl.when(t == 0)
    def _():
        r = lax.broadcasted_iota(I32, (na, na), 0)
        c = lax.broadcasted_iota(I32, (na, na), 1)
        before_ref[...] = (r < c).astype(BF16)
        zero_ref[...] = jnp.zeros_like(zero_ref)

    et = eid_ref[...].astype(F32).T
    e_row = jnp.concatenate([et[0:1], et[1:2]], axis=1)
    hit = lax.broadcasted_iota(I32, (LANES, na), 0).astype(F32) == e_row
    hit_b = hit.astype(BF16)
    rank = jnp.dot(hit_b, before_ref[...], preferred_element_type=F32)
    start = jnp.dot(lrow_ref[...].astype(BF16), hit_b, preferred_element_type=F32)[0:1] * MOE_CHUNK
    pos = (start + jnp.sum(jnp.where(hit, rank, 0.0), axis=0, keepdims=True)).astype(I32)
    r = lax.broadcasted_iota(I32, (loc, tm), 0)
    sel = ((r == pos[:, :tm]) | (r == pos[:, tm:])).astype(BF16)
    xloc_ref[slot] = _pack_rows(jnp.dot(sel, hn_ref[...], preferred_element_type=F32))

    def copy_out(local_row, global_row, s, rows=MOE_CHUNK):
        return pltpu.make_async_copy(xloc_ref.at[s, pl.ds(local_row, rows), :],
                                     xs_hbm.at[pl.ds(global_row, rows), :], sem.at[s])

    _segment_loop(t, lst_ref, gofs_ref, nch_ref, lambda lr, gr, rows: copy_out(lr, gr, slot, rows).start())

    @pl.when(t > 0)
    def _():
        _wait_chunks(ntot_ref[jnp.maximum(t - 1, 0)], lambda rows: copy_out(0, 0, 1 - slot, rows).wait())

    @pl.when(t == last)
    def _():
        _wait_chunks(ntot_ref[t], lambda rows: copy_out(0, 0, slot, rows).wait())

        def zero_out(global_row):
            return pltpu.make_async_copy(zero_ref, xs_hbm.at[pl.ds(global_row, MOE_CHUNK), :], sem.at[0])

        def per_expert(e, carry):
            def per_chunk(c, cc):
                zero_out(pl.multiple_of(tail_ref[e] + c * MOE_CHUNK, MOE_CHUNK)).start()
                return cc
            lax.fori_loop(0, tailn_ref[e], per_chunk, 0)
            return carry
        lax.fori_loop(0, N_EXPERTS, per_expert, 0)
        _repeat(tailtot_ref[0], lambda: zero_out(0).wait())


def _dispatch(plan, hn2d, eid2d, lrow, p_rows, tm):
    n = hn2d.shape[0]
    na = TOP_K * tm
    loc = _local_rows(tm)
    pre = (plan["gofs"], plan["lst"], plan["nch"], plan["ntot"], plan["tail"], plan["tailn"], plan["tailtot"])
    est = 2 * tm * D_MODEL * 2 + na * na * 2 + 2 * loc * PACKED * 4 + loc * D_MODEL * 8 + loc * tm * 4 \
        + 4 * LANES * na * 4
    grid_spec = pltpu.PrefetchScalarGridSpec(
        num_scalar_prefetch=len(pre),
        grid=(n // tm,),
        in_specs=[pl.BlockSpec((tm, D_MODEL), lambda t, *_: (t, 0)), pl.BlockSpec((tm, LANES), lambda t, *_: (t, 0)),
                  pl.BlockSpec((None, SUBLANES, LANES), lambda t, *_: (t, 0, 0))],
        out_specs=pl.BlockSpec(memory_space=pl.ANY),
        scratch_shapes=[pltpu.VMEM((na, na), BF16), pltpu.VMEM((2, loc, PACKED), U32),
                        pltpu.VMEM((MOE_CHUNK, PACKED), U32), pltpu.SemaphoreType.DMA((2,))],
    )
    return pl.pallas_call(
        functools.partial(_dispatch_kernel, tm=tm),
        out_shape=jax.ShapeDtypeStruct((p_rows, PACKED), U32),
        grid_spec=grid_spec,
        compiler_params=_params(("arbitrary",), est),
        name="moe_dispatch",
    )(*pre, hn2d, eid2d, lrow)


def _expert_kernel(blk_e_ref, blk_src_ref, blk_n_ref, xs_ref, wgu_ref, wd_ref, ys_ref, wgu_bf, wd_bf):
    i = pl.program_id(0)

    @pl.when(blk_n_ref[i] > 0)
    def _():
        @pl.when((i == 0) | (blk_e_ref[i] != blk_e_ref[jnp.maximum(i - 1, 0)]))
        def _():
            wgu_bf[...] = wgu_ref[0].astype(BF16)
            wd_bf[...] = wd_ref[0].astype(BF16)

        gu = jnp.dot(_unpack_rows(xs_ref[...]), wgu_bf[...], preferred_element_type=F32)
        act = (jax.nn.silu(gu[:, :EXPERT_FF]) * gu[:, EXPERT_FF:]).astype(BF16)
        ys_ref[...] = _pack_rows(jnp.dot(act, wd_bf[...], preferred_element_type=F32))


def _experts(plan, xs, wgu, wd, eb):
    n_blk = xs.shape[0] // eb
    rows = pl.BlockSpec((eb, PACKED), lambda i, be, bs, bn: (bs[i], 0))
    w_elems = wgu.shape[1] * wgu.shape[2] + wd.shape[1] * wd.shape[2]
    est = 2 * 2 * eb * PACKED * 4 + 2 * w_elems * 4 + w_elems * 2 + 6 * eb * D_MODEL * 4
    grid_spec = pltpu.PrefetchScalarGridSpec(
        num_scalar_prefetch=3,
        grid=(n_blk,),
        in_specs=[rows, pl.BlockSpec((1,) + wgu.shape[1:], lambda i, be, bs, bn: (be[i], 0, 0)),
                  pl.BlockSpec((1,) + wd.shape[1:], lambda i, be, bs, bn: (be[i], 0, 0))],
        out_specs=rows,
        scratch_shapes=[pltpu.VMEM(wgu.shape[1:], BF16), pltpu.VMEM(wd.shape[1:], BF16)],
    )
    return pl.pallas_call(
        _expert_kernel,
        out_shape=jax.ShapeDtypeStruct(xs.shape, U32),
        grid_spec=grid_spec,
        compiler_params=_params(("arbitrary",), est),
        name="experts",
    )(plan["blk_e"], plan["blk_src"], plan["blk_n"], xs, wgu, wd)


def _combine_kernel(gofs_ref, lst_ref, nch_ref, ntot_ref, x2_ref, ew_ref, eid_ref, lrow_ref, ys_hbm, o_ref,
                    before_ref, yloc_ref, sem, *, tm):
    t = pl.program_id(0)
    last = pl.num_programs(0) - 1
    slot = t % 2
    na = TOP_K * tm
    loc = yloc_ref.shape[1]

    def copy_in(local_row, global_row, s, rows=MOE_CHUNK):
        return pltpu.make_async_copy(ys_hbm.at[pl.ds(global_row, rows), :],
                                     yloc_ref.at[s, pl.ds(local_row, rows), :], sem.at[s])

    def fetch(tile, s):
        _segment_loop(tile, lst_ref, gofs_ref, nch_ref, lambda lr, gr, rows: copy_in(lr, gr, s, rows).start())

    @pl.when(t == 0)
    def _():
        r = lax.broadcasted_iota(I32, (na, na), 0)
        c = lax.broadcasted_iota(I32, (na, na), 1)
        before_ref[...] = (c < r).astype(BF16)
        yloc_ref[...] = jnp.zeros_like(yloc_ref)
        fetch(0, 0)

    @pl.when(t < last)
    def _():
        fetch(t + 1, 1 - slot)

    _wait_chunks(ntot_ref[t], lambda rows: copy_in(0, 0, slot, rows).wait())

    eid = eid_ref[...]
    lane = lax.broadcasted_iota(I32, (tm, LANES), 1)
    hit = jnp.concatenate([lane == eid[:, 0:1], lane == eid[:, 1:2]], axis=0)
    rank = jnp.dot(before_ref[...], hit.astype(BF16), preferred_element_type=F32)
    start = lrow_ref[0:1, :] * MOE_CHUNK
    pos = jnp.sum(jnp.where(hit, rank + start, 0.0), axis=1, keepdims=True).astype(I32)
    col = lax.broadcasted_iota(I32, (tm, loc), 1)
    ew = ew_ref[...]
    sel = jnp.where(col == pos[:tm], ew[:, 0:1], 0.0) + jnp.where(col == pos[tm:], ew[:, 1:2], 0.0)
    o_ref[...] = x2_ref[...] + jnp.dot(sel.astype(BF16), _unpack_rows(yloc_ref[slot]), preferred_element_type=F32)


def _combine(plan, x2, ew, eid2d, lrow, ys, tm):
    n = x2.shape[0]
    na = TOP_K * tm
    loc = _local_rows(tm)
    pre = (plan["gofs"], plan["lst"], plan["nch"], plan["ntot"])
    tok = pl.BlockSpec((tm, D_MODEL), lambda t, *_: (t, 0))
    small = pl.BlockSpec((tm, LANES), lambda t, *_: (t, 0))
    est = 2 * 2 * tm * D_MODEL * 4 + na * na * 2 + 2 * loc * PACKED * 4 + loc * D_MODEL * 10 + 2 * tm * loc * 4 \
        + 3 * tm * D_MODEL * 4
    grid_spec = pltpu.PrefetchScalarGridSpec(
        num_scalar_prefetch=len(pre),
        grid=(n // tm,),
        in_specs=[tok, small, small, pl.BlockSpec((None, SUBLANES, LANES), lambda t, *_: (t, 0, 0)),
                  pl.BlockSpec(memory_space=pl.ANY)],
        out_specs=tok,
        scratch_shapes=[pltpu.VMEM((na, na), BF16), pltpu.VMEM((2, loc, PACKED), U32),
                        pltpu.SemaphoreType.DMA((2,))],
    )
    return pl.pallas_call(
        functools.partial(_combine_kernel, tm=tm),
        out_shape=jax.ShapeDtypeStruct((n, D_MODEL), F32),
        grid_spec=grid_spec,
        compiler_params=_params(("arbitrary",), est),
        name="moe_combine",
    )(*pre, x2, ew, eid2d, lrow, ys)


def _rope_table(positions):
    inv_freq = jnp.exp(-math.log(ROPE_THETA) * jnp.arange(ROT_HALF, dtype=F32) / ROT_HALF)
    ang = positions.astype(F32).reshape(-1, 1) * inv_freq
    seg = jnp.concatenate([jnp.cos(ang), jnp.sin(ang), jnp.zeros((ang.shape[0], HEAD_DIM - ROT_DIMS), F32)], axis=1)
    return jnp.tile(seg, (1, LANES // HEAD_DIM))


def _plan_rows(n_tok, tm, eb):
    worst = n_tok * TOP_K + (n_tok // tm) * N_EXPERTS * (MOE_CHUNK - 1)
    return (worst + eb - 1) // eb * eb + N_EXPERTS * eb


def _moe_plan(hist, p_rows, eb):
    n_tiles = hist.shape[0]
    seg = (hist + MOE_CHUNK - 1) // MOE_CHUNK * MOE_CHUNK
    tot = jnp.sum(seg, axis=0)
    region = (tot + eb - 1) // eb * eb
    pend = jnp.cumsum(region)
    pstart = pend - region
    gofs = pstart[None, :] + jnp.cumsum(seg, axis=0) - seg
    lst = jnp.cumsum(seg, axis=1) - seg
    nch = seg // MOE_CHUNK
    n_blk = p_rows // eb
    blk_first = jnp.arange(n_blk, dtype=I32) * eb
    blk_e = jnp.minimum(jnp.sum(blk_first[:, None] >= pend[None, :], axis=1), N_EXPERTS - 1).astype(I32)
    blk_n = jnp.clip(pstart[blk_e] + tot[blk_e] - blk_first, 0, eb)
    blk_src = jnp.minimum(jnp.arange(n_blk, dtype=I32), jnp.maximum(pend[-1] // eb - 1, 0))
    tailn = (region - tot) // MOE_CHUNK
    lrow = jnp.pad((lst // MOE_CHUNK).astype(F32), ((0, 0), (0, LANES - N_EXPERTS)))
    flat = lambda a: a.reshape(-1).astype(I32)
    plan = dict(gofs=flat(gofs), lst=flat(lst), nch=flat(nch), ntot=flat(jnp.sum(nch, axis=1)),
                tail=flat(pstart + tot), tailn=flat(tailn), tailtot=flat(jnp.sum(tailn)),
                blk_e=blk_e, blk_src=flat(blk_src), blk_n=flat(blk_n))
    return plan, jnp.broadcast_to(lrow[:, None, :], (n_tiles, SUBLANES, LANES))


def _tile(n, pref):
    t = min(n, pref)
    assert n % t == 0, (n, pref)
    return t


def kernel(x, mem, positions, norm_mix, w_in, conv_w, conv_b, lru_wa, lru_ba, lru_wx, lru_bx, lru_lambda, w_lru_o, q_norm, k_norm, lambda_q1, lambda_k1, lambda_q2, lambda_k2, subln, w_attn_o, w_out, norm_cx, norm_mem, w_cq, w_ckv, cq_norm, ck_norm, w_co, norm_ffn, w_group, b_group, w_router, b_router, w_gate_up, w_down):
    bsz, s, d = x.shape
    assert d == D_MODEL and w_in.shape[-1] == N_PROJ * D_MODEL
    n = bsz * s
    depth = w_in.shape[0]
    ts = _tile(s, TOKEN_TILE)
    tq = _tile(s, TOKEN_TILE)
    assert tq % CHUNK == 0 and ts % SUBLANES == 0
    eb = _tile(n, EXPERT_BLOCK)
    rope = _rope_table(positions)
    row = lambda v: v.reshape(1, -1).astype(F32)
    rep = LANES // HEAD_DIM

    for layer in range(depth):
        lambda_init = 0.8 - 0.6 * math.exp(-0.3 * layer)
        wax = (0.5 * jnp.concatenate([lru_wa[layer], lru_wx[layer]], axis=-1)).astype(BF16)
        bax = 0.5 * jnp.stack([lru_ba[layer], lru_bx[layer]]).astype(F32)
        lru_params = (conv_w[layer].astype(F32), row(conv_b[layer]), wax, bax, row(lru_lambda[layer]),
                      w_lru_o[layer].astype(BF16))
        ml, q, k, vt, ga = _in_proj(
            x.reshape(n, d), row(norm_mix[layer]), w_in[layer].astype(BF16),
            jnp.tile(row(q_norm[layer]), (1, rep)), jnp.tile(row(k_norm[layer]), (1, rep)), rope, lru_params, ts, s)
        seq = lambda a: a.reshape(bsz, s, d)

        lam = (jnp.exp(jnp.sum(lambda_q1[layer].astype(F32) * lambda_k1[layer].astype(F32)))
               - jnp.exp(jnp.sum(lambda_q2[layer].astype(F32) * lambda_k2[layer].astype(F32))) + lambda_init)
        o = _diff_attn(lam.reshape(1, 1), seq(q), seq(k), vt, subln[layer].astype(F32).reshape(V_DIM, 1),
                       lambda_init, tq)

        kc, vc = _mem_kv(mem, row(norm_mem[layer]), w_ckv[layer].astype(BF16), row(ck_norm[layer]))
        w_r = jnp.concatenate([w_group[layer], w_router[layer],
                               jnp.zeros((d, LANES - N_GROUPS - N_EXPERTS), F32)], axis=1).astype(F32)
        b_r = jnp.concatenate([b_group[layer], b_router[layer],
                               jnp.zeros((LANES - N_GROUPS - N_EXPERTS,), F32)]).reshape(1, LANES).astype(F32)
        wrh, wrl = _split_bf16(w_r)
        x2, hn, eid, ew, hist = _cross_router(
            x, o, seq(ga), seq(ml), w_attn_o[layer].astype(BF16), w_out[layer].astype(BF16),
            row(norm_cx[layer]), w_cq[layer].astype(BF16), row(cq_norm[layer]), kc, vc,
            w_co[layer].astype(BF16), row(norm_ffn[layer]), wrh, wrl, b_r, ts)

        p_rows = _plan_rows(n, ts, eb)
        plan, lrow = _moe_plan(hist[:, :, 0, :N_EXPERTS].reshape(n // ts, N_EXPERTS), p_rows, eb)
        eid2d = eid.reshape(n, LANES)
        xs = _dispatch(plan, hn.reshape(n, d), eid2d, lrow, p_rows, ts)
        ys = _experts(plan, xs, w_gate_up[layer], w_down[layer], eb)
        x = _combine(plan, x2.reshape(n, d), ew.reshape(n, LANES), eid2d, lrow, ys, ts).reshape(bsz, s, d)
    return x
```

```python
import functools
import math

import jax
import jax.numpy as jnp
from jax import lax
from jax.experimental import pallas as pl
from jax.experimental.pallas import tpu as pltpu

F32 = jnp.float32
BF16 = jnp.bfloat16
I32 = jnp.int32

D_MODEL = 1024
CHUNK = 64
LRU_BLOCKS = 8
LRU_BLOCK_WIDTH = D_MODEL // LRU_BLOCKS
CONV_WIDTH = 4
LRU_C = 8.0
ATTN_HEADS = 8
HEAD_DIM = 64
V_DIM = 2 * HEAD_DIM
ROPE_THETA = 500000.0
ROT_DIMS = HEAD_DIM // 4
ROT_HALF = ROT_DIMS // 2
MEM_HEADS = 4
MEM_HEAD_DIM = 128
MEM_WIDTH = MEM_HEADS * MEM_HEAD_DIM
N_GROUPS = 4
EXPERTS_PER_GROUP = 8
N_EXPERTS = N_GROUPS * EXPERTS_PER_GROUP
TOP_K = 2
EXPERT_FF = 512
N_PROJ = 7
EPS = 1e-6
NEG_INF = -1e30
LOG2_E = math.log2(math.e)
ATTN_COL_GROUP = 512
ATTN_HEADS_PER_STEP = 2
TOKEN_TILE = 512
EXPERT_BLOCK = 512

LANES = 128
SUBLANES = 8
V7X_VMEM_BYTES = 64 * 1024 * 1024
MIB = 1024 * 1024


def _vmem_limit(estimate_bytes):
    return int(min(max(estimate_bytes * 3 // 2, 16 * MIB), V7X_VMEM_BYTES - 8 * MIB))


def _params(semantics, vmem_estimate):
    return pltpu.CompilerParams(dimension_semantics=semantics, vmem_limit_bytes=_vmem_limit(vmem_estimate))


def _resident(shape, index_map):
    return pl.BlockSpec(shape, index_map, pipeline_mode=pl.Buffered(1))


def _rms(x, g):
    return x * lax.rsqrt(jnp.mean(x * x, axis=-1, keepdims=True) + EPS) * g


def _sigmoid(x):
    return 0.5 * jnp.tanh(0.5 * x) + 0.5


def _segment_ones():
    r = lax.broadcasted_iota(I32, (LANES, LANES), 0) // HEAD_DIM
    c = lax.broadcasted_iota(I32, (LANES, LANES), 1) // HEAD_DIM
    return (r == c).astype(BF16)


def _qk_post(p, gain, cos_t, sin_lo, sin_hi, seg, scale):
    cols = []
    for c in range(D_MODEL // LANES):
        pc = p[:, c * LANES:(c + 1) * LANES]
        ss = jnp.dot((pc * pc).astype(BF16), seg, preferred_element_type=F32)
        y = pc * lax.rsqrt(ss * (1.0 / HEAD_DIM) + EPS) * gain
        y = y * cos_t + pltpu.roll(y, LANES - ROT_HALF, 1) * sin_lo + pltpu.roll(y, ROT_HALF, 1) * sin_hi
        cols.append((y * scale).astype(BF16))
    return jnp.concatenate(cols, axis=1)


def _in_proj_kernel(x_ref, g_ref, w_ref, qg_ref, kg_ref, rope_ref, cw_ref, cb_ref, wax_ref, bax_ref, lam_ref, wo_ref,
                    ml_ref, q_ref, k_ref, vt_ref, ga_ref, xpad_ref, hprev_ref, *, per_seq):
    @pl.when(pl.program_id(0) % per_seq == 0)
    def _():
        xpad_ref[0:SUBLANES, :] = jnp.zeros((SUBLANES, D_MODEL), F32)
        hprev_ref[...] = jnp.zeros_like(hprev_ref)

    h = _rms(x_ref[...], g_ref[...]).astype(BF16)

    def proj(j):
        return jnp.dot(h, w_ref[:, j * D_MODEL:(j + 1) * D_MODEL], preferred_element_type=F32)

    xc, pre_a, pre_x = _lru_conv_gates(proj(0), cw_ref, cb_ref, wax_ref, xpad_ref)
    seg = _segment_ones()
    tab = rope_ref[...]
    seg_lane = lax.broadcasted_iota(I32, tab.shape, 1) % HEAD_DIM
    first, second = seg_lane < ROT_HALF, (seg_lane >= ROT_HALF) & (seg_lane < ROT_DIMS)
    cos_t = jnp.where(first, tab, jnp.where(second, pltpu.roll(tab, ROT_HALF, 1), 1.0))
    sin_lo = jnp.where(first, -pltpu.roll(tab, LANES - ROT_HALF, 1), 0.0)
    sin_hi = jnp.where(second, tab, 0.0)
    q_ref[...] = _qk_post(proj(2), qg_ref[...], cos_t, sin_lo, sin_hi, seg, HEAD_DIM ** -0.5 * LOG2_E)
    k_ref[...] = _qk_post(proj(3), kg_ref[...], cos_t, sin_lo, sin_hi, seg, 1.0)
    hr = _lru_scan(xc, pre_a, pre_x, bax_ref, lam_ref, hprev_ref)
    gate, merge_gate = proj(1), proj(5)
    v = proj(4)
    for hd in range(ATTN_HEADS):
        vt_ref[hd * V_DIM:(hd + 1) * V_DIM, :] = v[:, hd * V_DIM:(hd + 1) * V_DIM].T.astype(BF16)
    ml_ref[...] = _lru_out(hr, gate, merge_gate, wo_ref)
    ga_ref[...] = proj(6).astype(BF16)


def _in_proj(x2d, g, w_in, qg, kg, rope, lru_params, tm, seq_len):
    n = x2d.shape[0]
    per_seq = seq_len // tm
    row = lambda i: (i, 0)
    tok = pl.BlockSpec((tm, D_MODEL), row)
    tok_out = jax.ShapeDtypeStruct((n, D_MODEL), BF16)
    vt_out = jax.ShapeDtypeStruct((n // seq_len, D_MODEL, seq_len), BF16)
    vt_spec = pl.BlockSpec((None, D_MODEL, tm), lambda i: (i // per_seq, 0, i % per_seq))
    whole = lambda a: _resident(a.shape, lambda i: (0,) * a.ndim)
    consts = (g, w_in, qg, kg)
    est = (sum(a.size * a.dtype.itemsize for a in consts + tuple(lru_params)) + 2 * tm * D_MODEL * 4
           + 2 * tm * LANES * 4 + 5 * 2 * tm * D_MODEL * 2 + 16 * tm * D_MODEL * 4)
    return pl.pallas_call(
        functools.partial(_in_proj_kernel, per_seq=per_seq),
        out_shape=[tok_out] * 3 + [vt_out, tok_out],
        grid=(n // tm,),
        in_specs=[tok] + [whole(a) for a in consts] + [pl.BlockSpec((tm, LANES), row)]
        + [whole(a) for a in lru_params],
        out_specs=[tok] * 3 + [vt_spec, tok],
        scratch_shapes=[pltpu.VMEM((tm + SUBLANES, D_MODEL), F32), pltpu.VMEM((SUBLANES, D_MODEL), F32)],
        compiler_params=_params(("arbitrary",), est),
        name="in_proj_lru",
    )(x2d, *consts, rope, *lru_params)


def _lru_conv_gates(x, cw_ref, cb_ref, wax_ref, xpad_ref):
    tt = x.shape[0]
    xpad_ref[SUBLANES:SUBLANES + tt, :] = x
    cw = cw_ref[...]
    xc = cb_ref[...] + cw[3:4] * x
    for j in range(1, CONV_WIDTH):
        xc = xc + cw[CONV_WIDTH - 1 - j:CONV_WIDTH - j] * xpad_ref[SUBLANES - j:SUBLANES - j + tt, :]
    xpad_ref[0:SUBLANES, :] = x[tt - SUBLANES:tt]

    xcb = xc.astype(BF16)
    ra, ri = [], []
    for n in range(LRU_BLOCKS):
        g = jnp.dot(xcb[:, n * LRU_BLOCK_WIDTH:(n + 1) * LRU_BLOCK_WIDTH], wax_ref[n], preferred_element_type=F32)
        ra.append(g[:, :LRU_BLOCK_WIDTH])
        ri.append(g[:, LRU_BLOCK_WIDTH:])
    return xc, jnp.concatenate(ra, axis=1), jnp.concatenate(ri, axis=1)


def _lru_scan(xc, pre_a, pre_x, bax_ref, lam_ref, hprev_ref):
    nblk = xc.shape[0] // SUBLANES
    bax = bax_ref[...]
    t_r = jnp.tanh(pre_a + bax[0:1])
    i = 0.5 * jnp.tanh(pre_x + bax[1:2]) + 0.5
    u = (0.5 * LRU_C) * jax.nn.softplus(-lam_ref[...]) * (t_r + 1.0)
    a = jnp.exp(-u)
    gain2 = jnp.tanh(u) * (a * a + 1.0)
    b = jnp.where(gain2 > 0.0, gain2 * lax.rsqrt(gain2), 0.0) * i * xc

    a3 = a.reshape(nblk, SUBLANES, D_MODEL)
    b3 = b.reshape(nblk, SUBLANES, D_MODEL)
    sub = lax.broadcasted_iota(I32, (nblk, SUBLANES, D_MODEL), 1)
    shift = 1
    while shift < SUBLANES:
        keep = sub >= shift
        a_sh = pltpu.roll(a3, shift, 1)
        b_sh = pltpu.roll(b3, shift, 1)
        b3 = jnp.where(keep, a3 * b_sh + b3, b3)
        a3 = jnp.where(keep, a3 * a_sh, a3)
        shift *= 2
    h_last = hprev_ref[...]
    groups = []
    for blk in range(nblk):
        hb = a3[blk] * h_last + b3[blk]
        groups.append(hb)
        h_last = jnp.broadcast_to(hb[SUBLANES - 1:SUBLANES], (SUBLANES, D_MODEL))
    hprev_ref[...] = h_last
    return jnp.concatenate(groups, axis=0)


def _lru_out(hr, gate, merge_gate, wo_ref):
    y = (jax.nn.gelu(gate) * hr).astype(BF16)
    yl = jnp.dot(y, wo_ref[...], preferred_element_type=F32)
    return (_sigmoid(merge_gate) * yl).astype(BF16)


def _attn_kernel(lam_ref, q_ref, qnext_ref, k_ref, vt_ref, sub_ref, o_ref, m_ref, l_ref, acc_ref, qz_ref, sa_ref,
                 sb_ref, mxa_ref, mxb_ref, *, tq, out_scale):
    i = pl.program_id(2)
    last = pl.num_programs(2) - 1
    heads = range(ATTN_HEADS_PER_STEP)
    m_ref[...] = jnp.full_like(m_ref, NEG_INF)
    l_ref[...] = jnp.zeros_like(l_ref)
    acc_ref[...] = jnp.zeros_like(acc_ref)
    cw = ATTN_COL_GROUP

    def load_queries(src_ref):
        row = lax.broadcasted_iota(I32, (V_DIM, tq), 0)
        for h in heads:
            qt = src_ref[:, h * V_DIM:(h + 1) * V_DIM].astype(F32).T
            zero = jnp.zeros_like(qt)
            qz_ref[h] = jnp.concatenate([jnp.where(row < HEAD_DIM, qt, zero), jnp.where(row >= HEAD_DIM, qt, zero)],
                                        axis=1).astype(BF16)

    def block_off(j):
        return pl.multiple_of(j * tq, tq)

    buf_a, buf_b = (sa_ref, mxa_ref), (sb_ref, mxb_ref)

    def scores(j, buf):
        s_ref, mx_ref = buf
        for h in heads:
            kb = k_ref[pl.ds(block_off(j), tq), h * V_DIM:(h + 1) * V_DIM]
            s = jnp.dot(kb, qz_ref[h], preferred_element_type=F32)
            s_ref[h] = s
            mx_ref[h] = jnp.max(s, axis=0, keepdims=True)

    def softmax_pv(j, buf, diagonal):
        s_ref, mx_ref = buf
        for h in heads:
            vtb = vt_ref[h * V_DIM:(h + 1) * V_DIM, pl.ds(block_off(j), tq)]
            for g in range(2 * tq // cw):
                cols = pl.ds(g * cw, cw)
                s = s_ref[h, :, cols]
                if diagonal:
                    key = lax.broadcasted_iota(I32, (tq, cw), 0)
                    qry = (lax.broadcasted_iota(I32, (tq, cw), 1) + g * cw) % tq
                    s = jnp.where((key // CHUNK) <= (qry // CHUNK), s, NEG_INF)
                    blk_max = jnp.max(s, axis=0, keepdims=True)
                else:
                    blk_max = mx_ref[h, :, cols]
                m_prev = m_ref[h, :, cols]
                m_new = jnp.maximum(m_prev, blk_max)
                alpha = jnp.exp2(m_prev - m_new)
                p = jnp.exp2(s - m_new)
                l_ref[h, :, cols] = alpha * l_ref[h, :, cols] + jnp.sum(p, axis=0, keepdims=True)
                acc_ref[h, :, cols] = alpha * acc_ref[h, :, cols] + jnp.dot(vtb, p.astype(BF16),
                                                                            preferred_element_type=F32)
                m_ref[h, :, cols] = m_new

    @pl.when(i == 0)
    def _():
        load_queries(q_ref)
        scores(0, buf_a)

    def pair(p, carry):
        j = 2 * p
        scores(j + 1, buf_b)
        softmax_pv(j, buf_a, False)
        scores(j + 2, buf_a)
        softmax_pv(j + 1, buf_b, False)
        return carry

    lax.fori_loop(0, i // 2, pair, 0)

    @pl.when(i % 2 == 1)
    def _():
        scores(i, buf_b)
        softmax_pv(i - 1, buf_a, False)
        softmax_pv(i, buf_b, True)

    @pl.when(i % 2 == 0)
    def _():
        softmax_pv(i, buf_a, True)

    def write_output():
        for h in heads:
            o12 = acc_ref[h] * (1.0 / l_ref[h])
            ot = o12[:, :tq] - lam_ref[0, 0] * o12[:, tq:]
            ot = ot * lax.rsqrt(jnp.mean(ot * ot, axis=0, keepdims=True) + EPS) * sub_ref[...] * out_scale
            o_ref[:, h * V_DIM:(h + 1) * V_DIM] = ot.T.astype(BF16)

    @pl.when(i < last)
    def _():
        load_queries(qnext_ref)
        scores(0, buf_a)
        write_output()

    @pl.when(i == last)
    def _():
        write_output()


def _diff_attn(lam, q, k, vt, sub, lambda_init, tq):
    bsz, s, _ = q.shape
    hps = ATTN_HEADS_PER_STEP
    width = hps * V_DIM
    nq = s // tq
    qspec = pl.BlockSpec((None, tq, width), lambda b, h, i: (b, i, h))
    qnext_spec = pl.BlockSpec((None, tq, width), lambda b, h, i: (b, jnp.minimum(i + 1, nq - 1), h))
    kspec = pl.BlockSpec((None, s, width), lambda b, h, i: (b, 0, h))
    vtspec = pl.BlockSpec((None, width, s), lambda b, h, i: (b, h, 0))
    est = 2 * 2 * s * width * 2 + 4 * tq * width * 2 + hps * (V_DIM * 2 * tq * 6 + 2 * tq * 2 * tq * 4) \
        + 3 * 2 * tq * tq * 4
    return pl.pallas_call(
        functools.partial(_attn_kernel, tq=tq, out_scale=1.0 - lambda_init),
        out_shape=jax.ShapeDtypeStruct((bsz, s, ATTN_HEADS * V_DIM), BF16),
        grid=(bsz, ATTN_HEADS // hps, nq),
        in_specs=[pl.BlockSpec(memory_space=pltpu.SMEM), qspec, qnext_spec, kspec, vtspec,
                  pl.BlockSpec((V_DIM, 1), lambda b, h, i: (0, 0))],
        out_specs=qspec,
        scratch_shapes=[pltpu.VMEM((hps, 1, 2 * tq), F32), pltpu.VMEM((hps, 1, 2 * tq), F32),
                        pltpu.VMEM((hps, V_DIM, 2 * tq), F32), pltpu.VMEM((hps, V_DIM, 2 * tq), BF16),
                        pltpu.VMEM((hps, tq, 2 * tq), F32), pltpu.VMEM((hps, tq, 2 * tq), F32),
                        pltpu.VMEM((hps, 1, 2 * tq), F32), pltpu.VMEM((hps, 1, 2 * tq), F32)],
        compiler_params=_params(("parallel", "parallel", "arbitrary"), est),
        name="diff_attn",
    )(lam, q, q, k, vt, sub)


def _mix_out(x, o, ga, ml, wao_ref, wout_ref):
    ya = jnp.dot(o, wao_ref[...], preferred_element_type=F32)
    mixed = ml.astype(F32) + _sigmoid(ga.astype(F32)) * ya
    return x + jnp.dot(mixed.astype(BF16), wout_ref[...], preferred_element_type=F32)


def _mem_kv_kernel(mem_ref, g_ref, w_ref, ckg_ref, k_ref, v_ref):
    h = _rms(mem_ref[...], g_ref[...]).astype(BF16)
    kv = jnp.dot(h, w_ref[...], preferred_element_type=F32)
    ks = [_rms(kv[:, hd * MEM_HEAD_DIM:(hd + 1) * MEM_HEAD_DIM], ckg_ref[...]) for hd in range(MEM_HEADS)]
    k_ref[...] = jnp.concatenate(ks, axis=1).astype(BF16)
    v_ref[...] = kv[:, MEM_WIDTH:].astype(BF16)


def _mem_kv(mem, g, w, ckg):
    bsz, m, _ = mem.shape
    fixed = lambda b: (0, 0)
    out = pl.BlockSpec((None, m, MEM_WIDTH), lambda b: (b, 0, 0))
    est = 2 * m * D_MODEL * 4 + w.size * 2 + 4 * m * MEM_WIDTH * 2 + 4 * m * D_MODEL * 4
    return pl.pallas_call(
        _mem_kv_kernel,
        out_shape=[jax.ShapeDtypeStruct((bsz, m, MEM_WIDTH), BF16)] * 2,
        grid=(bsz,),
        in_specs=[pl.BlockSpec((None, m, D_MODEL), lambda b: (b, 0, 0)), _resident((1, D_MODEL), fixed),
                  _resident(w.shape, fixed), _resident((1, MEM_HEAD_DIM), fixed)],
        out_specs=[out, out],
        compiler_params=_params(("parallel",), est),
        name="mem_kv",
    )(mem, g, w, ckg)


def _split_bf16(x):
    hi = x.astype(BF16)
    return hi, (x - hi.astype(F32)).astype(BF16)


def _cross_router_kernel(x_ref, o_ref, ga_ref, ml_ref, wao_ref, wout_ref, gcx_ref, wcq_ref, cqg_ref, kc_ref, vc_ref,
                         wco_ref, gffn_ref, wrh_ref, wrl_ref, br_ref, x2_ref, hn_ref, eid_ref, ew_ref, hist_ref):
    x1 = _mix_out(x_ref[...], o_ref[...], ga_ref[...], ml_ref[...], wao_ref, wout_ref)
    q = jnp.dot(_rms(x1, gcx_ref[...]).astype(BF16), wcq_ref[...], preferred_element_type=F32)
    outs = []
    for hd in range(MEM_HEADS):
        sl = slice(hd * MEM_HEAD_DIM, (hd + 1) * MEM_HEAD_DIM)
        qh = _rms(q[:, sl], cqg_ref[...]) * MEM_HEAD_DIM ** -0.5
        s = lax.dot_general(qh.astype(BF16), kc_ref[:, sl], (((1,), (1,)), ((), ())), preferred_element_type=F32)
        p = jnp.exp(s - jnp.max(s, axis=1, keepdims=True))
        o = jnp.dot(p.astype(BF16), vc_ref[:, sl], preferred_element_type=F32)
        outs.append(o / jnp.sum(p, axis=1, keepdims=True))
    x2 = x1 + jnp.dot(jnp.concatenate(outs, axis=1).astype(BF16), wco_ref[...], preferred_element_type=F32)
    x2_ref[...] = x2

    hn = _rms(x2, gffn_ref[...])
    hn_ref[...] = hn.astype(BF16)
    h_hi, h_lo = _split_bf16(hn)
    logits = (jnp.dot(h_hi, wrh_ref[...], preferred_element_type=F32)
              + jnp.dot(h_lo, wrh_ref[...], preferred_element_type=F32)
              + jnp.dot(h_hi, wrl_ref[...], preferred_element_type=F32)) + br_ref[...]
    lane = lax.broadcasted_iota(I32, logits.shape, 1)
    is_group = lane < N_GROUPS
    gl = jnp.where(is_group, logits, NEG_INF)
    gmax = jnp.max(gl, axis=1, keepdims=True)
    gval = 1.0 / jnp.sum(jnp.where(is_group, jnp.exp(gl - gmax), 0.0), axis=1, keepdims=True)
    gidx = jnp.min(jnp.where(gl == gmax, lane, LANES), axis=1, keepdims=True)
    lane_group = lax.shift_right_logical(lane + (EXPERTS_PER_GROUP - N_GROUPS), 3) - 1
    chosen = lane_group == gidx
    el = jnp.where(chosen, logits, NEG_INF)
    v1 = jnp.max(el, axis=1, keepdims=True)
    i1 = jnp.min(jnp.where(chosen & (el == v1), lane, LANES), axis=1, keepdims=True)
    rest = chosen & (lane != i1)
    el2 = jnp.where(rest, logits, NEG_INF)
    v2 = jnp.max(el2, axis=1, keepdims=True)
    i2 = jnp.min(jnp.where(rest & (el2 == v2), lane, LANES), axis=1, keepdims=True)
    t = jnp.exp(v2 - v1)
    w1 = gval / (1.0 + t)
    w2 = gval * t / (1.0 + t)
    e1, e2 = i1 - N_GROUPS, i2 - N_GROUPS
    eid_ref[...] = jnp.where(lane == 0, e1, jnp.where(lane == 1, e2, 0))
    ew_ref[...] = jnp.where(lane == 0, w1, jnp.where(lane == 1, w2, 0.0))
    count = jnp.sum((lane == e1).astype(I32) + (lane == e2).astype(I32), axis=0, keepdims=True)
    hist_ref[...] = jnp.broadcast_to(count, hist_ref.shape)


def _cross_router(x, o, ga, ml, wao, wout, gcx, wcq, cqg, kc, vc, wco, gffn, wrh, wrl, br, tm):
    bsz, s, _ = x.shape
    m = kc.shape[1]
    tok = pl.BlockSpec((None, tm, D_MODEL), lambda b, i: (b, i, 0))
    small = pl.BlockSpec((None, tm, LANES), lambda b, i: (b, i, 0))
    memb = pl.BlockSpec((None, m, MEM_WIDTH), lambda b, i: (b, 0, 0))
    fixed = lambda b, i: (0, 0)
    est = (3 * 2 * tm * D_MODEL * 4 + 4 * 2 * tm * D_MODEL * 2 + 2 * 2 * tm * LANES * 4 + 2 * 2 * m * MEM_WIDTH * 2
           + 2 * D_MODEL * D_MODEL * 2 + 2 * D_MODEL * MEM_WIDTH * 2 + 2 * D_MODEL * LANES * 2 + 8 * tm * D_MODEL * 4)
    return pl.pallas_call(
        _cross_router_kernel,
        out_shape=[jax.ShapeDtypeStruct((bsz, s, D_MODEL), F32), jax.ShapeDtypeStruct((bsz, s, D_MODEL), BF16),
                   jax.ShapeDtypeStruct((bsz, s, LANES), I32), jax.ShapeDtypeStruct((bsz, s, LANES), F32),
                   jax.ShapeDtypeStruct((bsz, s // tm, SUBLANES, LANES), I32)],
        grid=(bsz, s // tm),
        in_specs=[tok, tok, tok, tok, _resident(wao.shape, fixed), _resident(wout.shape, fixed),
                  _resident((1, D_MODEL), fixed), _resident(wcq.shape, fixed), _resident((1, MEM_HEAD_DIM), fixed),
                  memb, memb, _resident(wco.shape, fixed), _resident((1, D_MODEL), fixed),
                  _resident(wrh.shape, fixed), _resident(wrl.shape, fixed), _resident((1, LANES), fixed)],
        out_specs=[tok, tok, small, small,
                   pl.BlockSpec((None, None, SUBLANES, LANES), lambda b, i: (b, i, 0, 0))],
        compiler_params=_params(("parallel", "parallel"), est),
        name="mix_cross_router",
    )(x, o, ga, ml, wao, wout, gcx, wcq, cqg, kc, vc, wco, gffn, wrh, wrl, br)


MOE_CHUNK = SUBLANES
PACKED = D_MODEL // 2
U32 = jnp.uint32


def _pack_rows(x):
    hi = lax.bitcast_convert_type(x[:, :PACKED].astype(BF16).astype(F32), U32)
    lo = lax.bitcast_convert_type(x[:, PACKED:].astype(BF16).astype(F32), U32)
    return hi | lax.shift_right_logical(lo, jnp.full_like(lo, 16))


def _unpack_rows(p):
    a = lax.bitcast_convert_type(p & jnp.full_like(p, 0xFFFF0000), F32)
    b = lax.bitcast_convert_type(lax.shift_left(p, jnp.full_like(p, 16)), F32)
    return jnp.concatenate([a, b], axis=1).astype(BF16)


def _local_rows(tm):
    return TOP_K * tm + N_EXPERTS * MOE_CHUNK


def _segment_loop(t, lst_ref, gofs_ref, nch_ref, fn):
    quad = 4 * MOE_CHUNK

    def per_expert(e, carry):
        k = t * N_EXPERTS + e
        lst, gofs, nch = lst_ref[k], gofs_ref[k], nch_ref[k]
        nquad = lax.shift_right_logical(nch, 2)

        def per_quad(c, cc):
            fn(pl.multiple_of(lst + c * quad, MOE_CHUNK), pl.multiple_of(gofs + c * quad, MOE_CHUNK), quad)
            return cc
        lax.fori_loop(0, nquad, per_quad, 0)
        done = nquad * quad

        @pl.when((nch & 2) == 2)
        def _():
            fn(pl.multiple_of(lst + done, MOE_CHUNK), pl.multiple_of(gofs + done, MOE_CHUNK), 2 * MOE_CHUNK)

        @pl.when((nch & 1) == 1)
        def _():
            rest = done + (nch & 2) * MOE_CHUNK
            fn(pl.multiple_of(lst + rest, MOE_CHUNK), pl.multiple_of(gofs + rest, MOE_CHUNK), MOE_CHUNK)
        return carry
    lax.fori_loop(0, N_EXPERTS, per_expert, 0)


def _repeat(count, fn):
    def body(c, carry):
        fn()
        return carry
    lax.fori_loop(0, count, body, 0)


MOE_WAIT_GROUP = 8


def _wait_chunks(count, wait_rows):
    _repeat(lax.shift_right_logical(count, MOE_WAIT_GROUP.bit_length() - 1),
            lambda: wait_rows(MOE_WAIT_GROUP * MOE_CHUNK))
    _repeat(count & (MOE_WAIT_GROUP - 1), lambda: wait_rows(MOE_CHUNK))


def _dispatch_kernel(gofs_ref, lst_ref, nch_ref, ntot_ref, tail_ref, tailn_ref, tailtot_ref,
                     hn_ref, eid_ref, lrow_ref, xs_hbm, before_ref, xloc_ref, zero_ref, sem, *, tm):
    t = pl.program_id(0)
    last = pl.num_programs(0) - 1
    slot = t % 2
    na = TOP_K * tm
    loc = xloc_ref.shape[1]

    @pl.when(t == 0)
    def _():
        r = lax.broadcasted_iota(I32, (na, na), 0)
        c = lax.broadcasted_iota(I32, (na, na), 1)
        before_ref[...] = (r < c).astype(BF16)
        zero_ref[...] = jnp.zeros_like(zero_ref)

    et = eid_ref[...].astype(F32).T
    e_row = jnp.concatenate([et[0:1], et[1:2]], axis=1)
    hit = lax.broadcasted_iota(I32, (LANES, na), 0).astype(F32) == e_row
    hit_b = hit.astype(BF16)
    rank = jnp.dot(hit_b, before_ref[...], preferred_element_type=F32)
    start = jnp.dot(lrow_ref[...].astype(BF16), hit_b, preferred_element_type=F32)[0:1] * MOE_CHUNK
    pos = (start + jnp.sum(jnp.where(hit, rank, 0.0), axis=0, keepdims=True)).astype(I32)
    r = lax.broadcasted_iota(I32, (loc, tm), 0)
    sel = ((r == pos[:, :tm]) | (r == pos[:, tm:])).astype(BF16)
    xloc_ref[slot] = _pack_rows(jnp.dot(sel, hn_ref[...], preferred_element_type=F32))

    def copy_out(local_row, global_row, s, rows=MOE_CHUNK):
        return pltpu.make_async_copy(xloc_ref.at[s, pl.ds(local_row, rows), :],
                                     xs_hbm.at[pl.ds(global_row, rows), :], sem.at[s])

    _segment_loop(t, lst_ref, gofs_ref, nch_ref, lambda lr, gr, rows: copy_out(lr, gr, slot, rows).start())

    @pl.when(t > 0)
    def _():
        _wait_chunks(ntot_ref[jnp.maximum(t - 1, 0)], lambda rows: copy_out(0, 0, 1 - slot, rows).wait())

    @pl.when(t == last)
    def _():
        _wait_chunks(ntot_ref[t], lambda rows: copy_out(0, 0, slot, rows).wait())

        def zero_out(global_row):
            return pltpu.make_async_copy(zero_ref, xs_hbm.at[pl.ds(global_row, MOE_CHUNK), :], sem.at[0])

        def per_expert(e, carry):
            def per_chunk(c, cc):
                zero_out(pl.multiple_of(tail_ref[e] + c * MOE_CHUNK, MOE_CHUNK)).start()
                return cc
            lax.fori_loop(0, tailn_ref[e], per_chunk, 0)
            return carry
        lax.fori_loop(0, N_EXPERTS, per_expert, 0)
        _repeat(tailtot_ref[0], lambda: zero_out(0).wait())


def _dispatch(plan, hn2d, eid2d, lrow, p_rows, tm):
    n = hn2d.shape[0]
    na = TOP_K * tm
    loc = _local_rows(tm)
    pre = (plan["gofs"], plan["lst"], plan["nch"], plan["ntot"], plan["tail"], plan["tailn"], plan["tailtot"])
    est = 2 * tm * D_MODEL * 2 + na * na * 2 + 2 * loc * PACKED * 4 + loc * D_MODEL * 8 + loc * tm * 4 \
        + 4 * LANES * na * 4
    grid_spec = pltpu.PrefetchScalarGridSpec(
        num_scalar_prefetch=len(pre),
        grid=(n // tm,),
        in_specs=[pl.BlockSpec((tm, D_MODEL), lambda t, *_: (t, 0)), pl.BlockSpec((tm, LANES), lambda t, *_: (t, 0)),
                  pl.BlockSpec((None, SUBLANES, LANES), lambda t, *_: (t, 0, 0))],
        out_specs=pl.BlockSpec(memory_space=pl.ANY),
        scratch_shapes=[pltpu.VMEM((na, na), BF16), pltpu.VMEM((2, loc, PACKED), U32),
                        pltpu.VMEM((MOE_CHUNK, PACKED), U32), pltpu.SemaphoreType.DMA((2,))],
    )
    return pl.pallas_call(
        functools.partial(_dispatch_kernel, tm=tm),
        out_shape=jax.ShapeDtypeStruct((p_rows, PACKED), U32),
        grid_spec=grid_spec,
        compiler_params=_params(("arbitrary",), est),
        name="moe_dispatch",
    )(*pre, hn2d, eid2d, lrow)


def _expert_kernel(blk_e_ref, blk_src_ref, blk_n_ref, xs_ref, wgu_ref, wd_ref, ys_ref, wgu_bf, wd_bf):
    i = pl.program_id(0)

    @pl.when(blk_n_ref[i] > 0)
    def _():
        @pl.when((i == 0) | (blk_e_ref[i] != blk_e_ref[jnp.maximum(i - 1, 0)]))
        def _():
            wgu_bf[...] = wgu_ref[0].astype(BF16)
            wd_bf[...] = wd_ref[0].astype(BF16)

        gu = jnp.dot(_unpack_rows(xs_ref[...]), wgu_bf[...], preferred_element_type=F32)
        act = (jax.nn.silu(gu[:, :EXPERT_FF]) * gu[:, EXPERT_FF:]).astype(BF16)
        ys_ref[...] = _pack_rows(jnp.dot(act, wd_bf[...], preferred_element_type=F32))


def _experts(plan, xs, wgu, wd, eb):
    n_blk = xs.shape[0] // eb
    rows = pl.BlockSpec((eb, PACKED), lambda i, be, bs, bn: (bs[i], 0))
    w_elems = wgu.shape[1] * wgu.shape[2] + wd.shape[1] * wd.shape[2]
    est = 2 * 2 * eb * PACKED * 4 + 2 * w_elems * 4 + w_elems * 2 + 6 * eb * D_MODEL * 4
    grid_spec = pltpu.PrefetchScalarGridSpec(
        num_scalar_prefetch=3,
        grid=(n_blk,),
        in_specs=[rows, pl.BlockSpec((1,) + wgu.shape[1:], lambda i, be, bs, bn: (be[i], 0, 0)),
                  pl.BlockSpec((1,) + wd.shape[1:], lambda i, be, bs, bn: (be[i], 0, 0))],
        out_specs=rows,
        scratch_shapes=[pltpu.VMEM(wgu.shape[1:], BF16), pltpu.VMEM(wd.shape[1:], BF16)],
    )
    return pl.pallas_call(
        _expert_kernel,
        out_shape=jax.ShapeDtypeStruct(xs.shape, U32),
        grid_spec=grid_spec,
        compiler_params=_params(("arbitrary",), est),
        name="experts",
    )(plan["blk_e"], plan["blk_src"], plan["blk_n"], xs, wgu, wd)


def _combine_kernel(gofs_ref, lst_ref, nch_ref, ntot_ref, x2_ref, ew_ref, eid_ref, lrow_ref, ys_hbm, o_ref,
                    before_ref, yloc_ref, sem, *, tm):
    t = pl.program_id(0)
    last = pl.num_programs(0) - 1
    slot = t % 2
    na = TOP_K * tm
    loc = yloc_ref.shape[1]

    def copy_in(local_row, global_row, s, rows=MOE_CHUNK):
        return pltpu.make_async_copy(ys_hbm.at[pl.ds(global_row, rows), :],
                                     yloc_ref.at[s, pl.ds(local_row, rows), :], sem.at[s])

    def fetch(tile, s):
        _segment_loop(tile, lst_ref, gofs_ref, nch_ref, lambda lr, gr, rows: copy_in(lr, gr, s, rows).start())

    @pl.when(t == 0)
    def _():
        r = lax.broadcasted_iota(I32, (na, na), 0)
        c = lax.broadcasted_iota(I32, (na, na), 1)
        before_ref[...] = (c < r).astype(BF16)
        yloc_ref[...] = jnp.zeros_like(yloc_ref)
        fetch(0, 0)

    @pl.when(t < last)
    def _():
        fetch(t + 1, 1 - slot)

    _wait_chunks(ntot_ref[t], lambda rows: copy_in(0, 0, slot, rows).wait())

    eid = eid_ref[...]
    lane = lax.broadcasted_iota(I32, (tm, LANES), 1)
    hit = jnp.concatenate([lane == eid[:, 0:1], lane == eid[:, 1:2]], axis=0)
    rank = jnp.dot(before_ref[...], hit.astype(BF16), preferred_element_type=F32)
    start = lrow_ref[0:1, :] * MOE_CHUNK
    pos = jnp.sum(jnp.where(hit, rank + start, 0.0), axis=1, keepdims=True).astype(I32)
    col = lax.broadcasted_iota(I32, (tm, loc), 1)
    ew = ew_ref[...]
    sel = jnp.where(col == pos[:tm], ew[:, 0:1], 0.0) + jnp.where(col == pos[tm:], ew[:, 1:2], 0.0)
    o_ref[...] = x2_ref[...] + jnp.dot(sel.astype(BF16), _unpack_rows(yloc_ref[slot]), preferred_element_type=F32)


def _combine(plan, x2, ew, eid2d, lrow, ys, tm):
    n = x2.shape[0]
    na = TOP_K * tm
    loc = _local_rows(tm)
    pre = (plan["gofs"], plan["lst"], plan["nch"], plan["ntot"])
    tok = pl.BlockSpec((tm, D_MODEL), lambda t, *_: (t, 0))
    small = pl.BlockSpec((tm, LANES), lambda t, *_: (t, 0))
    est = 2 * 2 * tm * D_MODEL * 4 + na * na * 2 + 2 * loc * PACKED * 4 + loc * D_MODEL * 10 + 2 * tm * loc * 4 \
        + 3 * tm * D_MODEL * 4
    grid_spec = pltpu.PrefetchScalarGridSpec(
        num_scalar_prefetch=len(pre),
        grid=(n // tm,),
        in_specs=[tok, small, small, pl.BlockSpec((None, SUBLANES, LANES), lambda t, *_: (t, 0, 0)),
                  pl.BlockSpec(memory_space=pl.ANY)],
        out_specs=tok,
        scratch_shapes=[pltpu.VMEM((na, na), BF16), pltpu.VMEM((2, loc, PACKED), U32),
                        pltpu.SemaphoreType.DMA((2,))],
    )
    return pl.pallas_call(
        functools.partial(_combine_kernel, tm=tm),
        out_shape=jax.ShapeDtypeStruct((n, D_MODEL), F32),
        grid_spec=grid_spec,
        compiler_params=_params(("arbitrary",), est),
        name="moe_combine",
    )(*pre, x2, ew, eid2d, lrow, ys)


def _rope_table(positions):
    inv_freq = jnp.exp(-math.log(ROPE_THETA) * jnp.arange(ROT_HALF, dtype=F32) / ROT_HALF)
    ang = positions.astype(F32).reshape(-1, 1) * inv_freq
    seg = jnp.concatenate([jnp.cos(ang), jnp.sin(ang), jnp.zeros((ang.shape[0], HEAD_DIM - ROT_DIMS), F32)], axis=1)
    return jnp.tile(seg, (1, LANES // HEAD_DIM))


def _plan_rows(n_tok, tm, eb):
    worst = n_tok * TOP_K + (n_tok // tm) * N_EXPERTS * (MOE_CHUNK - 1)
    return (worst + eb - 1) // eb * eb + N_EXPERTS * eb


def _moe_plan(hist, p_rows, eb):
    n_tiles = hist.shape[0]
    seg = (hist + MOE_CHUNK - 1) // MOE_CHUNK * MOE_CHUNK
    tot = jnp.sum(seg, axis=0)
    region = (tot + eb - 1) // eb * eb
    pend = jnp.cumsum(region)
    pstart = pend - region
    gofs = pstart[None, :] + jnp.cumsum(seg, axis=0) - seg
    lst = jnp.cumsum(seg, axis=1) - seg
    nch = seg // MOE_CHUNK
    n_blk = p_rows // eb
    blk_first = jnp.arange(n_blk, dtype=I32) * eb
    blk_e = jnp.minimum(jnp.sum(blk_first[:, None] >= pend[None, :], axis=1), N_EXPERTS - 1).astype(I32)
    blk_n = jnp.clip(pstart[blk_e] + tot[blk_e] - blk_first, 0, eb)
    blk_src = jnp.minimum(jnp.arange(n_blk, dtype=I32), jnp.maximum(pend[-1] // eb - 1, 0))
    tailn = (region - tot) // MOE_CHUNK
    lrow = jnp.pad((lst // MOE_CHUNK).astype(F32), ((0, 0), (0, LANES - N_EXPERTS)))
    flat = lambda a: a.reshape(-1).astype(I32)
    plan = dict(gofs=flat(gofs), lst=flat(lst), nch=flat(nch), ntot=flat(jnp.sum(nch, axis=1)),
                tail=flat(pstart + tot), tailn=flat(tailn), tailtot=flat(jnp.sum(tailn)),
                blk_e=blk_e, blk_src=flat(blk_src), blk_n=flat(blk_n))
    return plan, jnp.broadcast_to(lrow[:, None, :], (n_tiles, SUBLANES, LANES))


def _tile(n, pref):
    t = min(n, pref)
    assert n % t == 0, (n, pref)
    return t


def kernel(x, mem, positions, norm_mix, w_in, conv_w, conv_b, lru_wa, lru_ba, lru_wx, lru_bx, lru_lambda, w_lru_o, q_norm, k_norm, lambda_q1, lambda_k1, lambda_q2, lambda_k2, subln, w_attn_o, w_out, norm_cx, norm_mem, w_cq, w_ckv, cq_norm, ck_norm, w_co, norm_ffn, w_group, b_group, w_router, b_router, w_gate_up, w_down):
    bsz, s, d = x.shape
    assert d == D_MODEL and w_in.shape[-1] == N_PROJ * D_MODEL
    n = bsz * s
    depth = w_in.shape[0]
    ts = _tile(s, TOKEN_TILE)
    tq = _tile(s, TOKEN_TILE)
    assert tq % CHUNK == 0 and ts % SUBLANES == 0
    eb = _tile(n, EXPERT_BLOCK)
    rope = _rope_table(positions)
    row = lambda v: v.reshape(1, -1).astype(F32)
    rep = LANES // HEAD_DIM

    for layer in range(depth):
        lambda_init = 0.8 - 0.6 * math.exp(-0.3 * layer)
        wax = (0.5 * jnp.concatenate([lru_wa[layer], lru_wx[layer]], axis=-1)).astype(BF16)
        bax = 0.5 * jnp.stack([lru_ba[layer], lru_bx[layer]]).astype(F32)
        lru_params = (conv_w[layer].astype(F32), row(conv_b[layer]), wax, bax, row(lru_lambda[layer]),
                      w_lru_o[layer].astype(BF16))
        ml, q, k, vt, ga = _in_proj(
            x.reshape(n, d), row(norm_mix[layer]), w_in[layer].astype(BF16),
            jnp.tile(row(q_norm[layer]), (1, rep)), jnp.tile(row(k_norm[layer]), (1, rep)), rope, lru_params, ts, s)
        seq = lambda a: a.reshape(bsz, s, d)

        lam = (jnp.exp(jnp.sum(lambda_q1[layer].astype(F32) * lambda_k1[layer].astype(F32)))
               - jnp.exp(jnp.sum(lambda_q2[layer].astype(F32) * lambda_k2[layer].astype(F32))) + lambda_init)
        o = _diff_attn(lam.reshape(1, 1), seq(q), seq(k), vt, subln[layer].astype(F32).reshape(V_DIM, 1),
                       lambda_init, tq)

        kc, vc = _mem_kv(mem, row(norm_mem[layer]), w_ckv[layer].astype(BF16), row(ck_norm[layer]))
        w_r = jnp.concatenate([w_group[layer], w_router[layer],
                               jnp.zeros((d, LANES - N_GROUPS - N_EXPERTS), F32)], axis=1).astype(F32)
        b_r = jnp.concatenate([b_group[layer], b_router[layer],
                               jnp.zeros((LANES - N_GROUPS - N_EXPERTS,), F32)]).reshape(1, LANES).astype(F32)
        wrh, wrl = _split_bf16(w_r)
        x2, hn, eid, ew, hist = _cross_router(
            x, o, seq(ga), seq(ml), w_attn_o[layer].astype(BF16), w_out[layer].astype(BF16),
            row(norm_cx[layer]), w_cq[layer].astype(BF16), row(cq_norm[layer]), kc, vc,
            w_co[layer].astype(BF16), row(norm_ffn[layer]), wrh, wrl, b_r, ts)

        p_rows = _plan_rows(n, ts, eb)
        plan, lrow = _moe_plan(hist[:, :, 0, :N_EXPERTS].reshape(n // ts, N_EXPERTS), p_rows, eb)
        eid2d = eid.reshape(n, LANES)
        xs = _dispatch(plan, hn.reshape(n, d), eid2d, lrow, p_rows, ts)
        ys = _experts(plan, xs, w_gate_up[layer], w_down[layer], eb)
        x = _combine(plan, x2.reshape(n, d), ew.reshape(n, LANES), eid2d, lrow, ys, ts).reshape(bsz, s, d)
    return x
```

```python
import functools
import math

import jax
import jax.numpy as jnp
from jax import lax
from jax.experimental import pallas as pl
from jax.experimental.pallas import tpu as pltpu

F32 = jnp.float32
BF16 = jnp.bfloat16
I32 = jnp.int32

D_MODEL = 1024
CHUNK = 64
LRU_BLOCKS = 8
LRU_BLOCK_WIDTH = D_MODEL // LRU_BLOCKS
CONV_WIDTH = 4
LRU_C = 8.0
ATTN_HEADS = 8
HEAD_DIM = 64
V_DIM = 2 * HEAD_DIM
ROPE_THETA = 500000.0
ROT_DIMS = HEAD_DIM // 4
ROT_HALF = ROT_DIMS // 2
MEM_HEADS = 4
MEM_HEAD_DIM = 128
MEM_WIDTH = MEM_HEADS * MEM_HEAD_DIM
N_GROUPS = 4
EXPERTS_PER_GROUP = 8
N_EXPERTS = N_GROUPS * EXPERTS_PER_GROUP
TOP_K = 2
EXPERT_FF = 512
N_PROJ = 7
EPS = 1e-6
NEG_INF = -1e30
LOG2_E = math.log2(math.e)
ATTN_COL_GROUP = 512
ATTN_HEADS_PER_STEP = 4
TOKEN_TILE = 512
EXPERT_BLOCK = 512

LANES = 128
SUBLANES = 8
V7X_VMEM_BYTES = 64 * 1024 * 1024
MIB = 1024 * 1024


def _vmem_limit(estimate_bytes):
    return int(min(max(estimate_bytes * 3 // 2, 16 * MIB), V7X_VMEM_BYTES - 8 * MIB))


def _params(semantics, vmem_estimate):
    return pltpu.CompilerParams(dimension_semantics=semantics, vmem_limit_bytes=_vmem_limit(vmem_estimate))


def _resident(shape, index_map):
    return pl.BlockSpec(shape, index_map, pipeline_mode=pl.Buffered(1))


def _rms(x, g):
    return x * lax.rsqrt(jnp.mean(x * x, axis=-1, keepdims=True) + EPS) * g


def _sigmoid(x):
    return 0.5 * jnp.tanh(0.5 * x) + 0.5


def _segment_ones():
    r = lax.broadcasted_iota(I32, (LANES, LANES), 0) // HEAD_DIM
    c = lax.broadcasted_iota(I32, (LANES, LANES), 1) // HEAD_DIM
    return (r == c).astype(BF16)


def _qk_post(p, gain, cos_t, sin_lo, sin_hi, seg, scale):
    cols = []
    for c in range(D_MODEL // LANES):
        pc = p[:, c * LANES:(c + 1) * LANES]
        ss = jnp.dot((pc * pc).astype(BF16), seg, preferred_element_type=F32)
        y = pc * lax.rsqrt(ss * (1.0 / HEAD_DIM) + EPS) * gain
        y = y * cos_t + pltpu.roll(y, LANES - ROT_HALF, 1) * sin_lo + pltpu.roll(y, ROT_HALF, 1) * sin_hi
        cols.append((y * scale).astype(BF16))
    return jnp.concatenate(cols, axis=1)


def _in_proj_kernel(x_ref, g_ref, w_ref, qg_ref, kg_ref, rope_ref, cw_ref, cb_ref, wax_ref, bax_ref, lam_ref, wo_ref,
                    ml_ref, q_ref, k_ref, vt_ref, ga_ref, xpad_ref, hprev_ref, *, per_seq):
    @pl.when(pl.program_id(0) % per_seq == 0)
    def _():
        xpad_ref[0:SUBLANES, :] = jnp.zeros((SUBLANES, D_MODEL), F32)
        hprev_ref[...] = jnp.zeros_like(hprev_ref)

    h = _rms(x_ref[...], g_ref[...]).astype(BF16)

    def proj(j):
        return jnp.dot(h, w_ref[:, j * D_MODEL:(j + 1) * D_MODEL], preferred_element_type=F32)

    xc, pre_a, pre_x = _lru_conv_gates(proj(0), cw_ref, cb_ref, wax_ref, xpad_ref)
    seg = _segment_ones()
    tab = rope_ref[...]
    seg_lane = lax.broadcasted_iota(I32, tab.shape, 1) % HEAD_DIM
    first, second = seg_lane < ROT_HALF, (seg_lane >= ROT_HALF) & (seg_lane < ROT_DIMS)
    cos_t = jnp.where(first, tab, jnp.where(second, pltpu.roll(tab, ROT_HALF, 1), 1.0))
    sin_lo = jnp.where(first, -pltpu.roll(tab, LANES - ROT_HALF, 1), 0.0)
    sin_hi = jnp.where(second, tab, 0.0)
    q_ref[...] = _qk_post(proj(2), qg_ref[...], cos_t, sin_lo, sin_hi, seg, HEAD_DIM ** -0.5 * LOG2_E)
    k_ref[...] = _qk_post(proj(3), kg_ref[...], cos_t, sin_lo, sin_hi, seg, 1.0)
    hr = _lru_scan(xc, pre_a, pre_x, bax_ref, lam_ref, hprev_ref)
    gate, merge_gate = proj(1), proj(5)
    v = proj(4)
    for hd in range(ATTN_HEADS):
        vt_ref[hd * V_DIM:(hd + 1) * V_DIM, :] = v[:, hd * V_DIM:(hd + 1) * V_DIM].T.astype(BF16)
    ml_ref[...] = _lru_out(hr, gate, merge_gate, wo_ref)
    ga_ref[...] = proj(6).astype(BF16)


def _in_proj(x2d, g, w_in, qg, kg, rope, lru_params, tm, seq_len):
    n = x2d.shape[0]
    per_seq = seq_len // tm
    row = lambda i: (i, 0)
    tok = pl.BlockSpec((tm, D_MODEL), row)
    tok_out = jax.ShapeDtypeStruct((n, D_MODEL), BF16)
    vt_out = jax.ShapeDtypeStruct((n // seq_len, D_MODEL, seq_len), BF16)
    vt_spec = pl.BlockSpec((None, D_MODEL, tm), lambda i: (i // per_seq, 0, i % per_seq))
    whole = lambda a: _resident(a.shape, lambda i: (0,) * a.ndim)
    consts = (g, w_in, qg, kg)
    est = (sum(a.size * a.dtype.itemsize for a in consts + tuple(lru_params)) + 2 * tm * D_MODEL * 4
           + 2 * tm * LANES * 4 + 5 * 2 * tm * D_MODEL * 2 + 16 * tm * D_MODEL * 4)
    return pl.pallas_call(
        functools.partial(_in_proj_kernel, per_seq=per_seq),
        out_shape=[tok_out] * 3 + [vt_out, tok_out],
        grid=(n // tm,),
        in_specs=[tok] + [whole(a) for a in consts] + [pl.BlockSpec((tm, LANES), row)]
        + [whole(a) for a in lru_params],
        out_specs=[tok] * 3 + [vt_spec, tok],
        scratch_shapes=[pltpu.VMEM((tm + SUBLANES, D_MODEL), F32), pltpu.VMEM((SUBLANES, D_MODEL), F32)],
        compiler_params=_params(("arbitrary",), est),
        name="in_proj_lru",
    )(x2d, *consts, rope, *lru_params)


def _lru_conv_gates(x, cw_ref, cb_ref, wax_ref, xpad_ref):
    tt = x.shape[0]
    xpad_ref[SUBLANES:SUBLANES + tt, :] = x
    cw = cw_ref[...]
    xc = cb_ref[...] + cw[3:4] * x
    for j in range(1, CONV_WIDTH):
        xc = xc + cw[CONV_WIDTH - 1 - j:CONV_WIDTH - j] * xpad_ref[SUBLANES - j:SUBLANES - j + tt, :]
    xpad_ref[0:SUBLANES, :] = x[tt - SUBLANES:tt]

    xcb = xc.astype(BF16)
    ra, ri = [], []
    for n in range(LRU_BLOCKS):
        g = jnp.dot(xcb[:, n * LRU_BLOCK_WIDTH:(n + 1) * LRU_BLOCK_WIDTH], wax_ref[n], preferred_element_type=F32)
        ra.append(g[:, :LRU_BLOCK_WIDTH])
        ri.append(g[:, LRU_BLOCK_WIDTH:])
    return xc, jnp.concatenate(ra, axis=1), jnp.concatenate(ri, axis=1)


def _lru_scan(xc, pre_a, pre_x, bax_ref, lam_ref, hprev_ref):
    nblk = xc.shape[0] // SUBLANES
    bax = bax_ref[...]
    t_r = jnp.tanh(pre_a + bax[0:1])
    i = 0.5 * jnp.tanh(pre_x + bax[1:2]) + 0.5
    u = (0.5 * LRU_C) * jax.nn.softplus(-lam_ref[...]) * (t_r + 1.0)
    a = jnp.exp(-u)
    gain2 = jnp.tanh(u) * (a * a + 1.0)
    b = jnp.where(gain2 > 0.0, gain2 * lax.rsqrt(gain2), 0.0) * i * xc

    a3 = a.reshape(nblk, SUBLANES, D_MODEL)
    b3 = b.reshape(nblk, SUBLANES, D_MODEL)
    sub = lax.broadcasted_iota(I32, (nblk, SUBLANES, D_MODEL), 1)
    shift = 1
    while shift < SUBLANES:
        keep = sub >= shift
        a_sh = pltpu.roll(a3, shift, 1)
        b_sh = pltpu.roll(b3, shift, 1)
        b3 = jnp.where(keep, a3 * b_sh + b3, b3)
        a3 = jnp.where(keep, a3 * a_sh, a3)
        shift *= 2
    h_last = hprev_ref[...]
    groups = []
    for blk in range(nblk):
        hb = a3[blk] * h_last + b3[blk]
        groups.append(hb)
        h_last = jnp.broadcast_to(hb[SUBLANES - 1:SUBLANES], (SUBLANES, D_MODEL))
    hprev_ref[...] = h_last
    return jnp.concatenate(groups, axis=0)


def _lru_out(hr, gate, merge_gate, wo_ref):
    y = (jax.nn.gelu(gate) * hr).astype(BF16)
    yl = jnp.dot(y, wo_ref[...], preferred_element_type=F32)
    return (_sigmoid(merge_gate) * yl).astype(BF16)


def _attn_kernel(lam_ref, q_ref, qnext_ref, k_ref, vt_ref, sub_ref, o_ref, m_ref, l_ref, acc_ref, qz_ref, sa_ref,
                 sb_ref, mxa_ref, mxb_ref, *, tq, out_scale):
    i = pl.program_id(2)
    last = pl.num_programs(2) - 1
    heads = range(ATTN_HEADS_PER_STEP)
    m_ref[...] = jnp.full_like(m_ref, NEG_INF)
    l_ref[...] = jnp.zeros_like(l_ref)
    acc_ref[...] = jnp.zeros_like(acc_ref)
    cw = ATTN_COL_GROUP

    def load_queries(src_ref):
        row = lax.broadcasted_iota(I32, (V_DIM, tq), 0)
        for h in heads:
            qt = src_ref[:, h * V_DIM:(h + 1) * V_DIM].astype(F32).T
            zero = jnp.zeros_like(qt)
            qz_ref[h] = jnp.concatenate([jnp.where(row < HEAD_DIM, qt, zero), jnp.where(row >= HEAD_DIM, qt, zero)],
                                        axis=1).astype(BF16)

    def block_off(j):
        return pl.multiple_of(j * tq, tq)

    buf_a, buf_b = (sa_ref, mxa_ref), (sb_ref, mxb_ref)

    def scores(j, buf, hs=heads):
        s_ref, mx_ref = buf
        for h in hs:
            kb = k_ref[pl.ds(block_off(j), tq), h * V_DIM:(h + 1) * V_DIM]
            s = jnp.dot(kb, qz_ref[h], preferred_element_type=F32)
            s_ref[h] = s
            mx_ref[h] = jnp.max(s, axis=0, keepdims=True)

    def softmax_pv(j, buf, diagonal, hs=heads):
        s_ref, mx_ref = buf
        for h in hs:
            vtb = vt_ref[h * V_DIM:(h + 1) * V_DIM, pl.ds(block_off(j), tq)]
            for g in range(2 * tq // cw):
                cols = pl.ds(g * cw, cw)
                s = s_ref[h, :, cols]
                if diagonal:
                    key = lax.broadcasted_iota(I32, (tq, cw), 0)
                    qry = (lax.broadcasted_iota(I32, (tq, cw), 1) + g * cw) % tq
                    s = jnp.where((key // CHUNK) <= (qry // CHUNK), s, NEG_INF)
                    blk_max = jnp.max(s, axis=0, keepdims=True)
                else:
                    blk_max = mx_ref[h, :, cols]
                m_prev = m_ref[h, :, cols]
                m_new = jnp.maximum(m_prev, blk_max)
                alpha = jnp.exp2(m_prev - m_new)
                p = jnp.exp2(s - m_new)
                l_ref[h, :, cols] = alpha * l_ref[h, :, cols] + jnp.sum(p, axis=0, keepdims=True)
                acc_ref[h, :, cols] = alpha * acc_ref[h, :, cols] + jnp.dot(vtb, p.astype(BF16),
                                                                            preferred_element_type=F32)
                m_ref[h, :, cols] = m_new

    @pl.when(i == 0)
    def _():
        load_queries(q_ref)
        scores(0, buf_a)

    def pair(p, carry):
        j = 2 * p
        for h in heads:
            scores(j + 1, buf_b, (h,))
            softmax_pv(j, buf_a, False, (h,))
        for h in heads:
            scores(j + 2, buf_a, (h,))
            softmax_pv(j + 1, buf_b, False, (h,))
        return carry

    lax.fori_loop(0, i // 2, pair, 0)

    @pl.when(i % 2 == 1)
    def _():
        for h in heads:
            scores(i, buf_b, (h,))
            softmax_pv(i - 1, buf_a, False, (h,))
        softmax_pv(i, buf_b, True)

    @pl.when(i % 2 == 0)
    def _():
        softmax_pv(i, buf_a, True)

    def write_output():
        for h in heads:
            o12 = acc_ref[h] * (1.0 / l_ref[h])
            ot = o12[:, :tq] - lam_ref[0, 0] * o12[:, tq:]
            ot = ot * lax.rsqrt(jnp.mean(ot * ot, axis=0, keepdims=True) + EPS) * sub_ref[...] * out_scale
            o_ref[:, h * V_DIM:(h + 1) * V_DIM] = ot.T.astype(BF16)

    @pl.when(i < last)
    def _():
        load_queries(qnext_ref)
        scores(0, buf_a)
        write_output()

    @pl.when(i == last)
    def _():
        write_output()


def _diff_attn(lam, q, k, vt, sub, lambda_init, tq):
    bsz, s, _ = q.shape
    hps = ATTN_HEADS_PER_STEP
    width = hps * V_DIM
    nq = s // tq
    qspec = pl.BlockSpec((None, tq, width), lambda b, h, i: (b, i, h))
    qnext_spec = pl.BlockSpec((None, tq, width), lambda b, h, i: (b, jnp.minimum(i + 1, nq - 1), h))
    kspec = pl.BlockSpec((None, s, width), lambda b, h, i: (b, 0, h))
    vtspec = pl.BlockSpec((None, width, s), lambda b, h, i: (b, h, 0))
    est = 2 * 2 * s * width * 2 + 4 * tq * width * 2 + hps * (V_DIM * 2 * tq * 6 + 2 * tq * 2 * tq * 4) \
        + 3 * 2 * tq * tq * 4
    return pl.pallas_call(
        functools.partial(_attn_kernel, tq=tq, out_scale=1.0 - lambda_init),
        out_shape=jax.ShapeDtypeStruct((bsz, s, ATTN_HEADS * V_DIM), BF16),
        grid=(bsz, ATTN_HEADS // hps, nq),
        in_specs=[pl.BlockSpec(memory_space=pltpu.SMEM), qspec, qnext_spec, kspec, vtspec,
                  pl.BlockSpec((V_DIM, 1), lambda b, h, i: (0, 0))],
        out_specs=qspec,
        scratch_shapes=[pltpu.VMEM((hps, 1, 2 * tq), F32), pltpu.VMEM((hps, 1, 2 * tq), F32),
                        pltpu.VMEM((hps, V_DIM, 2 * tq), F32), pltpu.VMEM((hps, V_DIM, 2 * tq), BF16),
                        pltpu.VMEM((hps, tq, 2 * tq), F32), pltpu.VMEM((hps, tq, 2 * tq), F32),
                        pltpu.VMEM((hps, 1, 2 * tq), F32), pltpu.VMEM((hps, 1, 2 * tq), F32)],
        compiler_params=_params(("parallel", "parallel", "arbitrary"), est),
        name="diff_attn",
    )(lam, q, q, k, vt, sub)


def _mix_out(x, o, ga, ml, wao_ref, wout_ref):
    ya = jnp.dot(o, wao_ref[...], preferred_element_type=F32)
    mixed = ml.astype(F32) + _sigmoid(ga.astype(F32)) * ya
    return x + jnp.dot(mixed.astype(BF16), wout_ref[...], preferred_element_type=F32)


def _mem_kv_kernel(mem_ref, g_ref, w_ref, ckg_ref, k_ref, v_ref):
    h = _rms(mem_ref[...], g_ref[...]).astype(BF16)
    kv = jnp.dot(h, w_ref[...], preferred_element_type=F32)
    ks = [_rms(kv[:, hd * MEM_HEAD_DIM:(hd + 1) * MEM_HEAD_DIM], ckg_ref[...]) for hd in range(MEM_HEADS)]
    k_ref[...] = jnp.concatenate(ks, axis=1).astype(BF16)
    v_ref[...] = kv[:, MEM_WIDTH:].astype(BF16)


def _mem_kv(mem, g, w, ckg):
    bsz, m, _ = mem.shape
    fixed = lambda b: (0, 0)
    out = pl.BlockSpec((None, m, MEM_WIDTH), lambda b: (b, 0, 0))
    est = 2 * m * D_MODEL * 4 + w.size * 2 + 4 * m * MEM_WIDTH * 2 + 4 * m * D_MODEL * 4
    return pl.pallas_call(
        _mem_kv_kernel,
        out_shape=[jax.ShapeDtypeStruct((bsz, m, MEM_WIDTH), BF16)] * 2,
        grid=(bsz,),
        in_specs=[pl.BlockSpec((None, m, D_MODEL), lambda b: (b, 0, 0)), _resident((1, D_MODEL), fixed),
                  _resident(w.shape, fixed), _resident((1, MEM_HEAD_DIM), fixed)],
        out_specs=[out, out],
        compiler_params=_params(("parallel",), est),
        name="mem_kv",
    )(mem, g, w, ckg)


def _split_bf16(x):
    hi = x.astype(BF16)
    return hi, (x - hi.astype(F32)).astype(BF16)


def _cross_router_kernel(x_ref, o_ref, ga_ref, ml_ref, wao_ref, wout_ref, gcx_ref, wcq_ref, cqg_ref, kc_ref, vc_ref,
                         wco_ref, gffn_ref, wrh_ref, wrl_ref, br_ref, x2_ref, hn_ref, eid_ref, ew_ref, hist_ref):
    x1 = _mix_out(x_ref[...], o_ref[...], ga_ref[...], ml_ref[...], wao_ref, wout_ref)
    q = jnp.dot(_rms(x1, gcx_ref[...]).astype(BF16), wcq_ref[...], preferred_element_type=F32)
    outs = []
    for hd in range(MEM_HEADS):
        sl = slice(hd * MEM_HEAD_DIM, (hd + 1) * MEM_HEAD_DIM)
        qh = _rms(q[:, sl], cqg_ref[...]) * MEM_HEAD_DIM ** -0.5
        s = lax.dot_general(qh.astype(BF16), kc_ref[:, sl], (((1,), (1,)), ((), ())), preferred_element_type=F32)
        p = jnp.exp(s - jnp.max(s, axis=1, keepdims=True))
        o = jnp.dot(p.astype(BF16), vc_ref[:, sl], preferred_element_type=F32)
        outs.append(o / jnp.sum(p, axis=1, keepdims=True))
    x2 = x1 + jnp.dot(jnp.concatenate(outs, axis=1).astype(BF16), wco_ref[...], preferred_element_type=F32)
    x2_ref[...] = x2

    hn = _rms(x2, gffn_ref[...])
    hn_ref[...] = hn.astype(BF16)
    h_hi, h_lo = _split_bf16(hn)
    logits = (jnp.dot(h_hi, wrh_ref[...], preferred_element_type=F32)
              + jnp.dot(h_lo, wrh_ref[...], preferred_element_type=F32)
              + jnp.dot(h_hi, wrl_ref[...], preferred_element_type=F32)) + br_ref[...]
    lane = lax.broadcasted_iota(I32, logits.shape, 1)
    is_group = lane < N_GROUPS
    gl = jnp.where(is_group, logits, NEG_INF)
    gmax = jnp.max(gl, axis=1, keepdims=True)
    gval = 1.0 / jnp.sum(jnp.where(is_group, jnp.exp(gl - gmax), 0.0), axis=1, keepdims=True)
    gidx = jnp.min(jnp.where(gl == gmax, lane, LANES), axis=1, keepdims=True)
    lane_group = lax.shift_right_logical(lane + (EXPERTS_PER_GROUP - N_GROUPS), 3) - 1
    chosen = lane_group == gidx
    el = jnp.where(chosen, logits, NEG_INF)
    v1 = jnp.max(el, axis=1, keepdims=True)
    i1 = jnp.min(jnp.where(chosen & (el == v1), lane, LANES), axis=1, keepdims=True)
    rest = chosen & (lane != i1)
    el2 = jnp.where(rest, logits, NEG_INF)
    v2 = jnp.max(el2, axis=1, keepdims=True)
    i2 = jnp.min(jnp.where(rest & (el2 == v2), lane, LANES), axis=1, keepdims=True)
    t = jnp.exp(v2 - v1)
    w1 = gval / (1.0 + t)
    w2 = gval * t / (1.0 + t)
    e1, e2 = i1 - N_GROUPS, i2 - N_GROUPS
    eid_ref[...] = jnp.where(lane == 0, e1, jnp.where(lane == 1, e2, 0))
    ew_ref[...] = jnp.where(lane == 0, w1, jnp.where(lane == 1, w2, 0.0))
    count = jnp.sum((lane == e1).astype(I32) + (lane == e2).astype(I32), axis=0, keepdims=True)
    hist_ref[...] = jnp.broadcast_to(count, hist_ref.shape)


def _cross_router(x, o, ga, ml, wao, wout, gcx, wcq, cqg, kc, vc, wco, gffn, wrh, wrl, br, tm):
    bsz, s, _ = x.shape
    m = kc.shape[1]
    tok = pl.BlockSpec((None, tm, D_MODEL), lambda b, i: (b, i, 0))
    small = pl.BlockSpec((None, tm, LANES), lambda b, i: (b, i, 0))
    memb = pl.BlockSpec((None, m, MEM_WIDTH), lambda b, i: (b, 0, 0))
    fixed = lambda b, i: (0, 0)
    est = (3 * 2 * tm * D_MODEL * 4 + 4 * 2 * tm * D_MODEL * 2 + 2 * 2 * tm * LANES * 4 + 2 * 2 * m * MEM_WIDTH * 2
           + 2 * D_MODEL * D_MODEL * 2 + 2 * D_MODEL * MEM_WIDTH * 2 + 2 * D_MODEL * LANES * 2 + 8 * tm * D_MODEL * 4)
    return pl.pallas_call(
        _cross_router_kernel,
        out_shape=[jax.ShapeDtypeStruct((bsz, s, D_MODEL), F32), jax.ShapeDtypeStruct((bsz, s, D_MODEL), BF16),
                   jax.ShapeDtypeStruct((bsz, s, LANES), I32), jax.ShapeDtypeStruct((bsz, s, LANES), F32),
                   jax.ShapeDtypeStruct((bsz, s // tm, SUBLANES, LANES), I32)],
        grid=(bsz, s // tm),
        in_specs=[tok, tok, tok, tok, _resident(wao.shape, fixed), _resident(wout.shape, fixed),
                  _resident((1, D_MODEL), fixed), _resident(wcq.shape, fixed), _resident((1, MEM_HEAD_DIM), fixed),
                  memb, memb, _resident(wco.shape, fixed), _resident((1, D_MODEL), fixed),
                  _resident(wrh.shape, fixed), _resident(wrl.shape, fixed), _resident((1, LANES), fixed)],
        out_specs=[tok, tok, small, small,
                   pl.BlockSpec((None, None, SUBLANES, LANES), lambda b, i: (b, i, 0, 0))],
        compiler_params=_params(("parallel", "parallel"), est),
        name="mix_cross_router",
    )(x, o, ga, ml, wao, wout, gcx, wcq, cqg, kc, vc, wco, gffn, wrh, wrl, br)


MOE_CHUNK = SUBLANES
PACKED = D_MODEL // 2
U32 = jnp.uint32


def _pack_rows(x):
    hi = lax.bitcast_convert_type(x[:, :PACKED].astype(BF16).astype(F32), U32)
    lo = lax.bitcast_convert_type(x[:, PACKED:].astype(BF16).astype(F32), U32)
    return hi | lax.shift_right_logical(lo, jnp.full_like(lo, 16))


def _unpack_rows(p):
    a = lax.bitcast_convert_type(p & jnp.full_like(p, 0xFFFF0000), F32)
    b = lax.bitcast_convert_type(lax.shift_left(p, jnp.full_like(p, 16)), F32)
    return jnp.concatenate([a, b], axis=1).astype(BF16)


def _local_rows(tm):
    return TOP_K * tm + N_EXPERTS * MOE_CHUNK


def _segment_loop(t, lst_ref, gofs_ref, nch_ref, fn):
    quad = 4 * MOE_CHUNK

    def per_expert(e, carry):
        k = t * N_EXPERTS + e
        lst, gofs, nch = lst_ref[k], gofs_ref[k], nch_ref[k]
        nquad = lax.shift_right_logical(nch, 2)

        def per_quad(c, cc):
            fn(pl.multiple_of(lst + c * quad, MOE_CHUNK), pl.multiple_of(gofs + c * quad, MOE_CHUNK), quad)
            return cc
        lax.fori_loop(0, nquad, per_quad, 0)
        done = nquad * quad

        @pl.when((nch & 2) == 2)
        def _():
            fn(pl.multiple_of(lst + done, MOE_CHUNK), pl.multiple_of(gofs + done, MOE_CHUNK), 2 * MOE_CHUNK)

        @pl.when((nch & 1) == 1)
        def _():
            rest = done + (nch & 2) * MOE_CHUNK
            fn(pl.multiple_of(lst + rest, MOE_CHUNK), pl.multiple_of(gofs + rest, MOE_CHUNK), MOE_CHUNK)
        return carry
    lax.fori_loop(0, N_EXPERTS, per_expert, 0)


def _repeat(count, fn):
    def body(c, carry):
        fn()
        return carry
    lax.fori_loop(0, count, body, 0)


MOE_WAIT_GROUP = 8


def _wait_chunks(count, wait_rows):
    _repeat(lax.shift_right_logical(count, MOE_WAIT_GROUP.bit_length() - 1),
            lambda: wait_rows(MOE_WAIT_GROUP * MOE_CHUNK))
    _repeat(count & (MOE_WAIT_GROUP - 1), lambda: wait_rows(MOE_CHUNK))


def _dispatch_kernel(gofs_ref, lst_ref, nch_ref, ntot_ref, tail_ref, tailn_ref, tailtot_ref,
                     hn_ref, eid_ref, lrow_ref, xs_hbm, before_ref, xloc_ref, zero_ref, sem, *, tm):
    t = pl.program_id(0)
    last = pl.num_programs(0) - 1
    slot = t % 2
    na = TOP_K * tm
    loc = xloc_ref.shape[1]

    @pl.when(t == 0)
    def _():
        r = lax.broadcasted_iota(I32, (na, na), 0)
        c = lax.broadcasted_iota(I32, (na, na), 1)
        before_ref[...] = (r < c).astype(BF16)
        zero_ref[...] = jnp.zeros_like(zero_ref)

    et = eid_ref[...].astype(F32).T
    e_row = jnp.concatenate([et[0:1], et[1:2]], axis=1)
    hit = lax.broadcasted_iota(I32, (LANES, na), 0).astype(F32) == e_row
    hit_b = hit.astype(BF16)
    rank = jnp.dot(hit_b, before_ref[...], preferred_element_type=F32)
    start = jnp.dot(lrow_ref[...].astype(BF16), hit_b, preferred_element_type=F32)[0:1] * MOE_CHUNK
    pos = (start + jnp.sum(jnp.where(hit, rank, 0.0), axis=0, keepdims=True)).astype(I32)
    r = lax.broadcasted_iota(I32, (loc, tm), 0)
    sel = ((r == pos[:, :tm]) | (r == pos[:, tm:])).astype(BF16)
    xloc_ref[slot] = _pack_rows(jnp.dot(sel, hn_ref[...], preferred_element_type=F32))

    def copy_out(local_row, global_row, s, rows=MOE_CHUNK):
        return pltpu.make_async_copy(xloc_ref.at[s, pl.ds(local_row, rows), :],
                                     xs_hbm.at[pl.ds(global_row, rows), :], sem.at[s])

    _segment_loop(t, lst_ref, gofs_ref, nch_ref, lambda lr, gr, rows: copy_out(lr, gr, slot, rows).start())

    @pl.when(t > 0)
    def _():
        _wait_chunks(ntot_ref[jnp.maximum(t - 1, 0)], lambda rows: copy_out(0, 0, 1 - slot, rows).wait())

    @pl.when(t == last)
    def _():
        _wait_chunks(ntot_ref[t], lambda rows: copy_out(0, 0, slot, rows).wait())

        def zero_out(global_row):
            return pltpu.make_async_copy(zero_ref, xs_hbm.at[pl.ds(global_row, MOE_CHUNK), :], sem.at[0])

        def per_expert(e, carry):
            def per_chunk(c, cc):
                zero_out(pl.multiple_of(tail_ref[e] + c * MOE_CHUNK, MOE_CHUNK)).start()
                return cc
            lax.fori_loop(0, tailn_ref[e], per_chunk, 0)
            return carry
        lax.fori_loop(0, N_EXPERTS, per_expert, 0)
        _repeat(tailtot_ref[0], lambda: zero_out(0).wait())


def _dispatch(plan, hn2d, eid2d, lrow, p_rows, tm):
    n = hn2d.shape[0]
    na = TOP_K * tm
    loc = _local_rows(tm)
    pre = (plan["gofs"], plan["lst"], plan["nch"], plan["ntot"], plan["tail"], plan["tailn"], plan["tailtot"])
    est = 2 * tm * D_MODEL * 2 + na * na * 2 + 2 * loc * PACKED * 4 + loc * D_MODEL * 8 + loc * tm * 4 \
        + 4 * LANES * na * 4
    grid_spec = pltpu.PrefetchScalarGridSpec(
        num_scalar_prefetch=len(pre),
        grid=(n // tm,),
        in_specs=[pl.BlockSpec((tm, D_MODEL), lambda t, *_: (t, 0)), pl.BlockSpec((tm, LANES), lambda t, *_: (t, 0)),
                  pl.BlockSpec((None, SUBLANES, LANES), lambda t, *_: (t, 0, 0))],
        out_specs=pl.BlockSpec(memory_space=pl.ANY),
        scratch_shapes=[pltpu.VMEM((na, na), BF16), pltpu.VMEM((2, loc, PACKED), U32),
                        pltpu.VMEM((MOE_CHUNK, PACKED), U32), pltpu.SemaphoreType.DMA((2,))],
    )
    return pl.pallas_call(
        functools.partial(_dispatch_kernel, tm=tm),
        out_shape=jax.ShapeDtypeStruct((p_rows, PACKED), U32),
        grid_spec=grid_spec,
        compiler_params=_params(("arbitrary",), est),
        name="moe_dispatch",
    )(*pre, hn2d, eid2d, lrow)


def _expert_kernel(blk_e_ref, blk_src_ref, blk_n_ref, xs_ref, wgu_ref, wd_ref, ys_ref, wgu_bf, wd_bf):
    i = pl.program_id(0)

    @pl.when(blk_n_ref[i] > 0)
    def _():
        @pl.when((i == 0) | (blk_e_ref[i] != blk_e_ref[jnp.maximum(i - 1, 0)]))
        def _():
            wgu_bf[...] = wgu_ref[0].astype(BF16)
            wd_bf[...] = wd_ref[0].astype(BF16)

        gu = jnp.dot(_unpack_rows(xs_ref[...]), wgu_bf[...], preferred_element_type=F32)
        act = (jax.nn.silu(gu[:, :EXPERT_FF]) * gu[:, EXPERT_FF:]).astype(BF16)
        ys_ref[...] = _pack_rows(jnp.dot(act, wd_bf[...], preferred_element_type=F32))


def _experts(plan, xs, wgu, wd, eb):
    n_blk = xs.shape[0] // eb
    rows = pl.BlockSpec((eb, PACKED), lambda i, be, bs, bn: (bs[i], 0))
    w_elems = wgu.shape[1] * wgu.shape[2] + wd.shape[1] * wd.shape[2]
    est = 2 * 2 * eb * PACKED * 4 + 2 * w_elems * 4 + w_elems * 2 + 6 * eb * D_MODEL * 4
    grid_spec = pltpu.PrefetchScalarGridSpec(
        num_scalar_prefetch=3,
        grid=(n_blk,),
        in_specs=[rows, pl.BlockSpec((1,) + wgu.shape[1:], lambda i, be, bs, bn: (be[i], 0, 0)),
                  pl.BlockSpec((1,) + wd.shape[1:], lambda i, be, bs, bn: (be[i], 0, 0))],
        out_specs=rows,
        scratch_shapes=[pltpu.VMEM(wgu.shape[1:], BF16), pltpu.VMEM(wd.shape[1:], BF16)],
    )
    return pl.pallas_call(
        _expert_kernel,
        out_shape=jax.ShapeDtypeStruct(xs.shape, U32),
        grid_spec=grid_spec,
        compiler_params=_params(("arbitrary",), est),
        name="experts",
    )(plan["blk_e"], plan["blk_src"], plan["blk_n"], xs, wgu, wd)


def _combine_kernel(gofs_ref, lst_ref, nch_ref, ntot_ref, x2_ref, ew_ref, eid_ref, lrow_ref, ys_hbm, o_ref,
                    before_ref, yloc_ref, sem, *, tm):
    t = pl.program_id(0)
    last = pl.num_programs(0) - 1
    slot = t % 2
    na = TOP_K * tm
    loc = yloc_ref.shape[1]

    def copy_in(local_row, global_row, s, rows=MOE_CHUNK):
        return pltpu.make_async_copy(ys_hbm.at[pl.ds(global_row, rows), :],
                                     yloc_ref.at[s, pl.ds(local_row, rows), :], sem.at[s])

    def fetch(tile, s):
        _segment_loop(tile, lst_ref, gofs_ref, nch_ref, lambda lr, gr, rows: copy_in(lr, gr, s, rows).start())

    @pl.when(t == 0)
    def _():
        r = lax.broadcasted_iota(I32, (na, na), 0)
        c = lax.broadcasted_iota(I32, (na, na), 1)
        before_ref[...] = (c < r).astype(BF16)
        yloc_ref[...] = jnp.zeros_like(yloc_ref)
        fetch(0, 0)

    @pl.when(t < last)
    def _():
        fetch(t + 1, 1 - slot)

    _wait_chunks(ntot_ref[t], lambda rows: copy_in(0, 0, slot, rows).wait())

    eid = eid_ref[...]
    lane = lax.broadcasted_iota(I32, (tm, LANES), 1)
    hit = jnp.concatenate([lane == eid[:, 0:1], lane == eid[:, 1:2]], axis=0)
    rank = jnp.dot(before_ref[...], hit.astype(BF16), preferred_element_type=F32)
    start = lrow_ref[0:1, :] * MOE_CHUNK
    pos = jnp.sum(jnp.where(hit, rank + start, 0.0), axis=1, keepdims=True).astype(I32)
    col = lax.broadcasted_iota(I32, (tm, loc), 1)
    ew = ew_ref[...]
    sel = jnp.where(col == pos[:tm], ew[:, 0:1], 0.0) + jnp.where(col == pos[tm:], ew[:, 1:2], 0.0)
    o_ref[...] = x2_ref[...] + jnp.dot(sel.astype(BF16), _unpack_rows(yloc_ref[slot]), preferred_element_type=F32)


def _combine(plan, x2, ew, eid2d, lrow, ys, tm):
    n = x2.shape[0]
    na = TOP_K * tm
    loc = _local_rows(tm)
    pre = (plan["gofs"], plan["lst"], plan["nch"], plan["ntot"])
    tok = pl.BlockSpec((tm, D_MODEL), lambda t, *_: (t, 0))
    small = pl.BlockSpec((tm, LANES), lambda t, *_: (t, 0))
    est = 2 * 2 * tm * D_MODEL * 4 + na * na * 2 + 2 * loc * PACKED * 4 + loc * D_MODEL * 10 + 2 * tm * loc * 4 \
        + 3 * tm * D_MODEL * 4
    grid_spec = pltpu.PrefetchScalarGridSpec(
        num_scalar_prefetch=len(pre),
        grid=(n // tm,),
        in_specs=[tok, small, small, pl.BlockSpec((None, SUBLANES, LANES), lambda t, *_: (t, 0, 0)),
                  pl.BlockSpec(memory_space=pl.ANY)],
        out_specs=tok,
        scratch_shapes=[pltpu.VMEM((na, na), BF16), pltpu.VMEM((2, loc, PACKED), U32),
                        pltpu.SemaphoreType.DMA((2,))],
    )
    return pl.pallas_call(
        functools.partial(_combine_kernel, tm=tm),
        out_shape=jax.ShapeDtypeStruct((n, D_MODEL), F32),
        grid_spec=grid_spec,
        compiler_params=_params(("arbitrary",), est),
        name="moe_combine",
    )(*pre, x2, ew, eid2d, lrow, ys)


def _rope_table(positions):
    inv_freq = jnp.exp(-math.log(ROPE_THETA) * jnp.arange(ROT_HALF, dtype=F32) / ROT_HALF)
    ang = positions.astype(F32).reshape(-1, 1) * inv_freq
    seg = jnp.concatenate([jnp.cos(ang), jnp.sin(ang), jnp.zeros((ang.shape[0], HEAD_DIM - ROT_DIMS), F32)], axis=1)
    return jnp.tile(seg, (1, LANES // HEAD_DIM))


def _plan_rows(n_tok, tm, eb):
    worst = n_tok * TOP_K + (n_tok // tm) * N_EXPERTS * (MOE_CHUNK - 1)
    return (worst + eb - 1) // eb * eb + N_EXPERTS * eb


def _moe_plan(hist, p_rows, eb):
    n_tiles = hist.shape[0]
    seg = (hist + MOE_CHUNK - 1) // MOE_CHUNK * MOE_CHUNK
    tot = jnp.sum(seg, axis=0)
    region = (tot + eb - 1) // eb * eb
    pend = jnp.cumsum(region)
    pstart = pend - region
    gofs = pstart[None, :] + jnp.cumsum(seg, axis=0) - seg
    lst = jnp.cumsum(seg, axis=1) - seg
    nch = seg // MOE_CHUNK
    n_blk = p_rows // eb
    blk_first = jnp.arange(n_blk, dtype=I32) * eb
    blk_e = jnp.minimum(jnp.sum(blk_first[:, None] >= pend[None, :], axis=1), N_EXPERTS - 1).astype(I32)
    blk_n = jnp.clip(pstart[blk_e] + tot[blk_e] - blk_first, 0, eb)
    blk_src = jnp.minimum(jnp.arange(n_blk, dtype=I32), jnp.maximum(pend[-1] // eb - 1, 0))
    tailn = (region - tot) // MOE_CHUNK
    lrow = jnp.pad((lst // MOE_CHUNK).astype(F32), ((0, 0), (0, LANES - N_EXPERTS)))
    flat = lambda a: a.reshape(-1).astype(I32)
    plan = dict(gofs=flat(gofs), lst=flat(lst), nch=flat(nch), ntot=flat(jnp.sum(nch, axis=1)),
                tail=flat(pstart + tot), tailn=flat(tailn), tailtot=flat(jnp.sum(tailn)),
                blk_e=blk_e, blk_src=flat(blk_src), blk_n=flat(blk_n))
    return plan, jnp.broadcast_to(lrow[:, None, :], (n_tiles, SUBLANES, LANES))


def _tile(n, pref):
    t = min(n, pref)
    assert n % t == 0, (n, pref)
    return t


def kernel(x, mem, positions, norm_mix, w_in, conv_w, conv_b, lru_wa, lru_ba, lru_wx, lru_bx, lru_lambda, w_lru_o, q_norm, k_norm, lambda_q1, lambda_k1, lambda_q2, lambda_k2, subln, w_attn_o, w_out, norm_cx, norm_mem, w_cq, w_ckv, cq_norm, ck_norm, w_co, norm_ffn, w_group, b_group, w_router, b_router, w_gate_up, w_down):
    bsz, s, d = x.shape
    assert d == D_MODEL and w_in.shape[-1] == N_PROJ * D_MODEL
    n = bsz * s
    depth = w_in.shape[0]
    ts = _tile(s, TOKEN_TILE)
    tq = _tile(s, TOKEN_TILE)
    assert tq % CHUNK == 0 and ts % SUBLANES == 0
    eb = _tile(n, EXPERT_BLOCK)
    rope = _rope_table(positions)
    row = lambda v: v.reshape(1, -1).astype(F32)
    rep = LANES // HEAD_DIM

    for layer in range(depth):
        lambda_init = 0.8 - 0.6 * math.exp(-0.3 * layer)
        wax = (0.5 * jnp.concatenate([lru_wa[layer], lru_wx[layer]], axis=-1)).astype(BF16)
        bax = 0.5 * jnp.stack([lru_ba[layer], lru_bx[layer]]).astype(F32)
        lru_params = (conv_w[layer].astype(F32), row(conv_b[layer]), wax, bax, row(lru_lambda[layer]),
                      w_lru_o[layer].astype(BF16))
        ml, q, k, vt, ga = _in_proj(
            x.reshape(n, d), row(norm_mix[layer]), w_in[layer].astype(BF16),
            jnp.tile(row(q_norm[layer]), (1, rep)), jnp.tile(row(k_norm[layer]), (1, rep)), rope, lru_params, ts, s)
        seq = lambda a: a.reshape(bsz, s, d)

        lam = (jnp.exp(jnp.sum(lambda_q1[layer].astype(F32) * lambda_k1[layer].astype(F32)))
               - jnp.exp(jnp.sum(lambda_q2[layer].astype(F32) * lambda_k2[layer].astype(F32))) + lambda_init)
        o = _diff_attn(lam.reshape(1, 1), seq(q), seq(k), vt, subln[layer].astype(F32).reshape(V_DIM, 1),
                       lambda_init, tq)

        kc, vc = _mem_kv(mem, row(norm_mem[layer]), w_ckv[layer].astype(BF16), row(ck_norm[layer]))
        w_r = jnp.concatenate([w_group[layer], w_router[layer],
                               jnp.zeros((d, LANES - N_GROUPS - N_EXPERTS), F32)], axis=1).astype(F32)
        b_r = jnp.concatenate([b_group[layer], b_router[layer],
                               jnp.zeros((LANES - N_GROUPS - N_EXPERTS,), F32)]).reshape(1, LANES).astype(F32)
        wrh, wrl = _split_bf16(w_r)
        x2, hn, eid, ew, hist = _cross_router(
            x, o, seq(ga), seq(ml), w_attn_o[layer].astype(BF16), w_out[layer].astype(BF16),
            row(norm_cx[layer]), w_cq[layer].astype(BF16), row(cq_norm[layer]), kc, vc,
            w_co[layer].astype(BF16), row(norm_ffn[layer]), wrh, wrl, b_r, ts)

        p_rows = _plan_rows(n, ts, eb)
        plan, lrow = _moe_plan(hist[:, :, 0, :N_EXPERTS].reshape(n // ts, N_EXPERTS), p_rows, eb)
        eid2d = eid.reshape(n, LANES)
        xs = _dispatch(plan, hn.reshape(n, d), eid2d, lrow, p_rows, ts)
        ys = _experts(plan, xs, w_gate_up[layer], w_down[layer], eb)
        x = _combine(plan, x2.reshape(n, d), ew.reshape(n, LANES), eid2d, lrow, ys, ts).reshape(bsz, s, d)
    return x
```

```python
import functools
import math

import jax
import jax.numpy as jnp
from jax import lax
from jax.experimental import pallas as pl
from jax.experimental.pallas import tpu as pltpu

F32 = jnp.float32
BF16 = jnp.bfloat16
I32 = jnp.int32

D_MODEL = 1024
CHUNK = 64
LRU_BLOCKS = 8
LRU_BLOCK_WIDTH = D_MODEL // LRU_BLOCKS
CONV_WIDTH = 4
LRU_C = 8.0
ATTN_HEADS = 8
HEAD_DIM = 64
V_DIM = 2 * HEAD_DIM
ROPE_THETA = 500000.0
ROT_DIMS = HEAD_DIM // 4
ROT_HALF = ROT_DIMS // 2
MEM_HEADS = 4
MEM_HEAD_DIM = 128
MEM_WIDTH = MEM_HEADS * MEM_HEAD_DIM
N_GROUPS = 4
EXPERTS_PER_GROUP = 8
N_EXPERTS = N_GROUPS * EXPERTS_PER_GROUP
TOP_K = 2
EXPERT_FF = 512
N_PROJ = 7
EPS = 1e-6
NEG_INF = -1e30
LOG2_E = math.log2(math.e)
ATTN_COL_GROUP = 512
ATTN_HEADS_PER_STEP = 4
TOKEN_TILE = 512
EXPERT_BLOCK = 512

LANES = 128
SUBLANES = 8
V7X_VMEM_BYTES = 64 * 1024 * 1024
MIB = 1024 * 1024


def _vmem_limit(estimate_bytes):
    return int(min(max(estimate_bytes * 3 // 2, 16 * MIB), V7X_VMEM_BYTES - 8 * MIB))


def _params(semantics, vmem_estimate):
    return pltpu.CompilerParams(dimension_semantics=semantics, vmem_limit_bytes=_vmem_limit(vmem_estimate))


def _resident(shape, index_map):
    return pl.BlockSpec(shape, index_map, pipeline_mode=pl.Buffered(1))


def _rms(x, g):
    return x * lax.rsqrt(jnp.mean(x * x, axis=-1, keepdims=True) + EPS) * g


def _sigmoid(x):
    return 0.5 * jnp.tanh(0.5 * x) + 0.5


def _segment_ones():
    r = lax.broadcasted_iota(I32, (LANES, LANES), 0) // HEAD_DIM
    c = lax.broadcasted_iota(I32, (LANES, LANES), 1) // HEAD_DIM
    return (r == c).astype(BF16)


def _qk_post(p, gain, cos_t, sin_lo, sin_hi, seg, scale):
    cols = []
    for c in range(D_MODEL // LANES):
        pc = p[:, c * LANES:(c + 1) * LANES]
        ss = jnp.dot((pc * pc).astype(BF16), seg, preferred_element_type=F32)
        y = pc * lax.rsqrt(ss * (1.0 / HEAD_DIM) + EPS) * gain
        y = y * cos_t + pltpu.roll(y, LANES - ROT_HALF, 1) * sin_lo + pltpu.roll(y, ROT_HALF, 1) * sin_hi
        cols.append((y * scale).astype(BF16))
    return jnp.concatenate(cols, axis=1)


def _in_proj_kernel(x_ref, g_ref, w_ref, qg_ref, kg_ref, rope_ref, cw_ref, cb_ref, wax_ref, bax_ref, lam_ref, wo_ref,
                    ml_ref, q_ref, k_ref, vt_ref, ga_ref, xpad_ref, hprev_ref, *, per_seq):
    @pl.when(pl.program_id(0) % per_seq == 0)
    def _():
        xpad_ref[0:SUBLANES, :] = jnp.zeros((SUBLANES, D_MODEL), F32)
        hprev_ref[...] = jnp.zeros_like(hprev_ref)

    h = _rms(x_ref[...], g_ref[...]).astype(BF16)

    def proj(j):
        return jnp.dot(h, w_ref[:, j * D_MODEL:(j + 1) * D_MODEL], preferred_element_type=F32)

    xc, pre_a, pre_x = _lru_conv_gates(proj(0), cw_ref, cb_ref, wax_ref, xpad_ref)
    seg = _segment_ones()
    tab = rope_ref[...]
    seg_lane = lax.broadcasted_iota(I32, tab.shape, 1) % HEAD_DIM
    first, second = seg_lane < ROT_HALF, (seg_lane >= ROT_HALF) & (seg_lane < ROT_DIMS)
    cos_t = jnp.where(first, tab, jnp.where(second, pltpu.roll(tab, ROT_HALF, 1), 1.0))
    sin_lo = jnp.where(first, -pltpu.roll(tab, LANES - ROT_HALF, 1), 0.0)
    sin_hi = jnp.where(second, tab, 0.0)
    q_ref[...] = _qk_post(proj(2), qg_ref[...], cos_t, sin_lo, sin_hi, seg, HEAD_DIM ** -0.5 * LOG2_E)
    k_ref[...] = _qk_post(proj(3), kg_ref[...], cos_t, sin_lo, sin_hi, seg, 1.0)
    hr = _lru_scan(xc, pre_a, pre_x, bax_ref, lam_ref, hprev_ref)
    gate, merge_gate = proj(1), proj(5)
    v = proj(4)
    for hd in range(ATTN_HEADS):
        vt_ref[hd * V_DIM:(hd + 1) * V_DIM, :] = v[:, hd * V_DIM:(hd + 1) * V_DIM].T.astype(BF16)
    ml_ref[...] = _lru_out(hr, gate, merge_gate, wo_ref)
    ga_ref[...] = proj(6).astype(BF16)


def _in_proj(x2d, g, w_in, qg, kg, rope, lru_params, tm, seq_len):
    n = x2d.shape[0]
    per_seq = seq_len // tm
    row = lambda i: (i, 0)
    tok = pl.BlockSpec((tm, D_MODEL), row)
    tok_out = jax.ShapeDtypeStruct((n, D_MODEL), BF16)
    vt_out = jax.ShapeDtypeStruct((n // seq_len, D_MODEL, seq_len), BF16)
    vt_spec = pl.BlockSpec((None, D_MODEL, tm), lambda i: (i // per_seq, 0, i % per_seq))
    whole = lambda a: _resident(a.shape, lambda i: (0,) * a.ndim)
    consts = (g, w_in, qg, kg)
    est = (sum(a.size * a.dtype.itemsize for a in consts + tuple(lru_params)) + 2 * tm * D_MODEL * 4
           + 2 * tm * LANES * 4 + 5 * 2 * tm * D_MODEL * 2 + 16 * tm * D_MODEL * 4)
    return pl.pallas_call(
        functools.partial(_in_proj_kernel, per_seq=per_seq),
        out_shape=[tok_out] * 3 + [vt_out, tok_out],
        grid=(n // tm,),
        in_specs=[tok] + [whole(a) for a in consts] + [pl.BlockSpec((tm, LANES), row)]
        + [whole(a) for a in lru_params],
        out_specs=[tok] * 3 + [vt_spec, tok],
        scratch_shapes=[pltpu.VMEM((tm + SUBLANES, D_MODEL), F32), pltpu.VMEM((SUBLANES, D_MODEL), F32)],
        compiler_params=_params(("arbitrary",), est),
        name="in_proj_lru",
    )(x2d, *consts, rope, *lru_params)


def _lru_conv_gates(x, cw_ref, cb_ref, wax_ref, xpad_ref):
    tt = x.shape[0]
    xpad_ref[SUBLANES:SUBLANES + tt, :] = x
    cw = cw_ref[...]
    xc = cb_ref[...] + cw[3:4] * x
    for j in range(1, CONV_WIDTH):
        xc = xc + cw[CONV_WIDTH - 1 - j:CONV_WIDTH - j] * xpad_ref[SUBLANES - j:SUBLANES - j + tt, :]
    xpad_ref[0:SUBLANES, :] = x[tt - SUBLANES:tt]

    xcb = xc.astype(BF16)
    ra, ri = [], []
    for n in range(LRU_BLOCKS):
        g = jnp.dot(xcb[:, n * LRU_BLOCK_WIDTH:(n + 1) * LRU_BLOCK_WIDTH], wax_ref[n], preferred_element_type=F32)
        ra.append(g[:, :LRU_BLOCK_WIDTH])
        ri.append(g[:, LRU_BLOCK_WIDTH:])
    return xc, jnp.concatenate(ra, axis=1), jnp.concatenate(ri, axis=1)


def _lru_scan(xc, pre_a, pre_x, bax_ref, lam_ref, hprev_ref):
    nblk = xc.shape[0] // SUBLANES
    bax = bax_ref[...]
    t_r = jnp.tanh(pre_a + bax[0:1])
    i = 0.5 * jnp.tanh(pre_x + bax[1:2]) + 0.5
    u = (0.5 * LRU_C) * jax.nn.softplus(-lam_ref[...]) * (t_r + 1.0)
    a = jnp.exp(-u)
    gain2 = jnp.tanh(u) * (a * a + 1.0)
    b = jnp.where(gain2 > 0.0, gain2 * lax.rsqrt(gain2), 0.0) * i * xc

    a3 = a.reshape(nblk, SUBLANES, D_MODEL)
    b3 = b.reshape(nblk, SUBLANES, D_MODEL)
    sub = lax.broadcasted_iota(I32, (nblk, SUBLANES, D_MODEL), 1)
    shift = 1
    while shift < SUBLANES:
        keep = sub >= shift
        a_sh = pltpu.roll(a3, shift, 1)
        b_sh = pltpu.roll(b3, shift, 1)
        b3 = jnp.where(keep, a3 * b_sh + b3, b3)
        a3 = jnp.where(keep, a3 * a_sh, a3)
        shift *= 2
    h_last = hprev_ref[...]
    groups = []
    for blk in range(nblk):
        hb = a3[blk] * h_last + b3[blk]
        groups.append(hb)
        h_last = jnp.broadcast_to(hb[SUBLANES - 1:SUBLANES], (SUBLANES, D_MODEL))
    hprev_ref[...] = h_last
    return jnp.concatenate(groups, axis=0)


def _lru_out(hr, gate, merge_gate, wo_ref):
    y = (jax.nn.gelu(gate) * hr).astype(BF16)
    yl = jnp.dot(y, wo_ref[...], preferred_element_type=F32)
    return (_sigmoid(merge_gate) * yl).astype(BF16)


def _attn_kernel(lam_ref, q_ref, qnext_ref, k_ref, vt_ref, sub_ref, o_ref, m_ref, l_ref, acc_ref, qz_ref, sa_ref,
                 sb_ref, mxa_ref, mxb_ref, *, tq, out_scale):
    i = pl.program_id(2)
    last = pl.num_programs(2) - 1
    heads = range(ATTN_HEADS_PER_STEP)
    m_ref[...] = jnp.full_like(m_ref, NEG_INF)
    l_ref[...] = jnp.zeros_like(l_ref)
    acc_ref[...] = jnp.zeros_like(acc_ref)
    cw = ATTN_COL_GROUP

    def load_queries(src_ref):
        row = lax.broadcasted_iota(I32, (V_DIM, tq), 0)
        for h in heads:
            qt = src_ref[:, h * V_DIM:(h + 1) * V_DIM].astype(F32).T
            zero = jnp.zeros_like(qt)
            qz_ref[h] = jnp.concatenate([jnp.where(row < HEAD_DIM, qt, zero), jnp.where(row >= HEAD_DIM, qt, zero)],
                                        axis=1).astype(BF16)

    def block_off(j):
        return pl.multiple_of(j * tq, tq)

    buf_a, buf_b = (sa_ref, mxa_ref), (sb_ref, mxb_ref)

    def scores(j, buf, hs=heads):
        s_ref, mx_ref = buf
        for h in hs:
            kb = k_ref[pl.ds(block_off(j), tq), h * V_DIM:(h + 1) * V_DIM]
            s = jnp.dot(kb, qz_ref[h], preferred_element_type=F32)
            s_ref[h] = s
            mx_ref[h] = jnp.max(s, axis=0, keepdims=True)

    def softmax_pv(j, buf, diagonal, hs=heads):
        s_ref, mx_ref = buf
        for h in hs:
            vtb = vt_ref[h * V_DIM:(h + 1) * V_DIM, pl.ds(block_off(j), tq)]
            for g in range(2 * tq // cw):
                cols = pl.ds(g * cw, cw)
                s = s_ref[h, :, cols]
                if diagonal:
                    key = lax.broadcasted_iota(I32, (tq, cw), 0)
                    qry = (lax.broadcasted_iota(I32, (tq, cw), 1) + g * cw) % tq
                    s = jnp.where((key // CHUNK) <= (qry // CHUNK), s, NEG_INF)
                    blk_max = jnp.max(s, axis=0, keepdims=True)
                else:
                    blk_max = mx_ref[h, :, cols]
                m_prev = m_ref[h, :, cols]
                m_new = jnp.maximum(m_prev, blk_max)
                alpha = jnp.exp2(m_prev - m_new)
                p = jnp.exp2(s - m_new)
                l_ref[h, :, cols] = alpha * l_ref[h, :, cols] + jnp.sum(p, axis=0, keepdims=True)
                acc_ref[h, :, cols] = alpha * acc_ref[h, :, cols] + jnp.dot(vtb, p.astype(BF16),
                                                                            preferred_element_type=F32)
                m_ref[h, :, cols] = m_new

    @pl.when(i == 0)
    def _():
        load_queries(q_ref)
        scores(0, buf_a)

    def pair(p, carry):
        j = 2 * p
        for h in heads:
            scores(j + 1, buf_b, (h,))
            softmax_pv(j, buf_a, False, (h,))
        for h in heads:
            scores(j + 2, buf_a, (h,))
            softmax_pv(j + 1, buf_b, False, (h,))
        return carry

    lax.fori_loop(0, i // 2, pair, 0)

    @pl.when(i % 2 == 1)
    def _():
        for h in heads:
            scores(i, buf_b, (h,))
            softmax_pv(i - 1, buf_a, False, (h,))
        softmax_pv(i, buf_b, True)

    @pl.when(i % 2 == 0)
    def _():
        softmax_pv(i, buf_a, True)

    def write_output():
        for h in heads:
            o12 = acc_ref[h] * (1.0 / l_ref[h])
            ot = o12[:, :tq] - lam_ref[0, 0] * o12[:, tq:]
            ot = ot * lax.rsqrt(jnp.mean(ot * ot, axis=0, keepdims=True) + EPS) * sub_ref[...] * out_scale
            o_ref[:, h * V_DIM:(h + 1) * V_DIM] = ot.T.astype(BF16)

    @pl.when(i < last)
    def _():
        load_queries(qnext_ref)
        scores(0, buf_a)
        write_output()

    @pl.when(i == last)
    def _():
        write_output()


def _diff_attn(lam, q, k, vt, sub, lambda_init, tq):
    bsz, s, _ = q.shape
    hps = ATTN_HEADS_PER_STEP
    width = hps * V_DIM
    nq = s // tq
    qspec = pl.BlockSpec((None, tq, width), lambda b, h, i: (b, i, h))
    qnext_spec = pl.BlockSpec((None, tq, width), lambda b, h, i: (b, jnp.minimum(i + 1, nq - 1), h))
    kspec = pl.BlockSpec((None, s, width), lambda b, h, i: (b, 0, h))
    vtspec = pl.BlockSpec((None, width, s), lambda b, h, i: (b, h, 0))
    est = 2 * 2 * s * width * 2 + 4 * tq * width * 2 + hps * (V_DIM * 2 * tq * 6 + 2 * tq * 2 * tq * 4) \
        + 3 * 2 * tq * tq * 4
    return pl.pallas_call(
        functools.partial(_attn_kernel, tq=tq, out_scale=1.0 - lambda_init),
        out_shape=jax.ShapeDtypeStruct((bsz, s, ATTN_HEADS * V_DIM), BF16),
        grid=(bsz, ATTN_HEADS // hps, nq),
        in_specs=[pl.BlockSpec(memory_space=pltpu.SMEM), qspec, qnext_spec, kspec, vtspec,
                  pl.BlockSpec((V_DIM, 1), lambda b, h, i: (0, 0))],
        out_specs=qspec,
        scratch_shapes=[pltpu.VMEM((hps, 1, 2 * tq), F32), pltpu.VMEM((hps, 1, 2 * tq), F32),
                        pltpu.VMEM((hps, V_DIM, 2 * tq), F32), pltpu.VMEM((hps, V_DIM, 2 * tq), BF16),
                        pltpu.VMEM((hps, tq, 2 * tq), F32), pltpu.VMEM((hps, tq, 2 * tq), F32),
                        pltpu.VMEM((hps, 1, 2 * tq), F32), pltpu.VMEM((hps, 1, 2 * tq), F32)],
        compiler_params=_params(("parallel", "parallel", "arbitrary"), est),
        name="diff_attn",
    )(lam, q, q, k, vt, sub)


def _mix_out(x, o, ga, ml, wao_ref, wout_ref):
    ya = jnp.dot(o, wao_ref[...], preferred_element_type=F32)
    mixed = ml.astype(F32) + _sigmoid(ga.astype(F32)) * ya
    return x + jnp.dot(mixed.astype(BF16), wout_ref[...], preferred_element_type=F32)


def _mem_kv_kernel(mem_ref, g_ref, w_ref, ckg_ref, k_ref, v_ref):
    h = _rms(mem_ref[...], g_ref[...]).astype(BF16)
    kv = jnp.dot(h, w_ref[...], preferred_element_type=F32)
    ks = [_rms(kv[:, hd * MEM_HEAD_DIM:(hd + 1) * MEM_HEAD_DIM], ckg_ref[...]) for hd in range(MEM_HEADS)]
    k_ref[...] = jnp.concatenate(ks, axis=1).astype(BF16)
    v_ref[...] = kv[:, MEM_WIDTH:].astype(BF16)


def _mem_kv(mem, g, w, ckg):
    bsz, m, _ = mem.shape
    fixed = lambda b: (0, 0)
    out = pl.BlockSpec((None, m, MEM_WIDTH), lambda b: (b, 0, 0))
    est = 2 * m * D_MODEL * 4 + w.size * 2 + 4 * m * MEM_WIDTH * 2 + 4 * m * D_MODEL * 4
    return pl.pallas_call(
        _mem_kv_kernel,
        out_shape=[jax.ShapeDtypeStruct((bsz, m, MEM_WIDTH), BF16)] * 2,
        grid=(bsz,),
        in_specs=[pl.BlockSpec((None, m, D_MODEL), lambda b: (b, 0, 0)), _resident((1, D_MODEL), fixed),
                  _resident(w.shape, fixed), _resident((1, MEM_HEAD_DIM), fixed)],
        out_specs=[out, out],
        compiler_params=_params(("parallel",), est),
        name="mem_kv",
    )(mem, g, w, ckg)


def _split_bf16(x):
    hi = x.astype(BF16)
    return hi, (x - hi.astype(F32)).astype(BF16)


def _cross_router_kernel(x_ref, o_ref, ga_ref, ml_ref, wao_ref, wout_ref, gcx_ref, wcq_ref, cqg_ref, kc_ref, vc_ref,
                         wco_ref, gffn_ref, wrh_ref, wrl_ref, br_ref, x2_ref, hn_ref, eid_ref, ew_ref, hist_ref):
    x1 = _mix_out(x_ref[...], o_ref[...], ga_ref[...], ml_ref[...], wao_ref, wout_ref)
    q = jnp.dot(_rms(x1, gcx_ref[...]).astype(BF16), wcq_ref[...], preferred_element_type=F32)
    outs = []
    for hd in range(MEM_HEADS):
        sl = slice(hd * MEM_HEAD_DIM, (hd + 1) * MEM_HEAD_DIM)
        qh = _rms(q[:, sl], cqg_ref[...]) * MEM_HEAD_DIM ** -0.5
        s = lax.dot_general(qh.astype(BF16), kc_ref[:, sl], (((1,), (1,)), ((), ())), preferred_element_type=F32)
        p = jnp.exp(s - jnp.max(s, axis=1, keepdims=True))
        o = jnp.dot(p.astype(BF16), vc_ref[:, sl], preferred_element_type=F32)
        outs.append(o / jnp.sum(p, axis=1, keepdims=True))
    x2 = x1 + jnp.dot(jnp.concatenate(outs, axis=1).astype(BF16), wco_ref[...], preferred_element_type=F32)
    x2_ref[...] = x2

    hn = _rms(x2, gffn_ref[...])
    hn_ref[...] = hn.astype(BF16)
    h_hi, h_lo = _split_bf16(hn)
    logits = (jnp.dot(h_hi, wrh_ref[...], preferred_element_type=F32)
              + jnp.dot(h_lo, wrh_ref[...], preferred_element_type=F32)
              + jnp.dot(h_hi, wrl_ref[...], preferred_element_type=F32)) + br_ref[...]
    lane = lax.broadcasted_iota(I32, logits.shape, 1)
    is_group = lane < N_GROUPS
    gl = jnp.where(is_group, logits, NEG_INF)
    gmax = jnp.max(gl, axis=1, keepdims=True)
    gval = 1.0 / jnp.sum(jnp.where(is_group, jnp.exp(gl - gmax), 0.0), axis=1, keepdims=True)
    gidx = jnp.min(jnp.where(gl == gmax, lane, LANES), axis=1, keepdims=True)
    lane_group = lax.shift_right_logical(lane + (EXPERTS_PER_GROUP - N_GROUPS), 3) - 1
    chosen = lane_group == gidx
    el = jnp.where(chosen, logits, NEG_INF)
    v1 = jnp.max(el, axis=1, keepdims=True)
    i1 = jnp.min(jnp.where(chosen & (el == v1), lane, LANES), axis=1, keepdims=True)
    rest = chosen & (lane != i1)
    el2 = jnp.where(rest, logits, NEG_INF)
    v2 = jnp.max(el2, axis=1, keepdims=True)
    i2 = jnp.min(jnp.where(rest & (el2 == v2), lane, LANES), axis=1, keepdims=True)
    t = jnp.exp(v2 - v1)
    w1 = gval / (1.0 + t)
    w2 = gval * t / (1.0 + t)
    e1, e2 = i1 - N_GROUPS, i2 - N_GROUPS
    eid_ref[...] = jnp.where(lane == 0, e1, jnp.where(lane == 1, e2, 0))
    ew_ref[...] = jnp.where(lane == 0, w1, jnp.where(lane == 1, w2, 0.0))
    count = jnp.sum((lane == e1).astype(I32) + (lane == e2).astype(I32), axis=0, keepdims=True)
    hist_ref[...] = jnp.broadcast_to(count, hist_ref.shape)


def _cross_router(x, o, ga, ml, wao, wout, gcx, wcq, cqg, kc, vc, wco, gffn, wrh, wrl, br, tm):
    bsz, s, _ = x.shape
    m = kc.shape[1]
    tok = pl.BlockSpec((None, tm, D_MODEL), lambda b, i: (b, i, 0))
    small = pl.BlockSpec((None, tm, LANES), lambda b, i: (b, i, 0))
    memb = pl.BlockSpec((None, m, MEM_WIDTH), lambda b, i: (b, 0, 0))
    fixed = lambda b, i: (0, 0)
    est = (3 * 2 * tm * D_MODEL * 4 + 4 * 2 * tm * D_MODEL * 2 + 2 * 2 * tm * LANES * 4 + 2 * 2 * m * MEM_WIDTH * 2
           + 2 * D_MODEL * D_MODEL * 2 + 2 * D_MODEL * MEM_WIDTH * 2 + 2 * D_MODEL * LANES * 2 + 8 * tm * D_MODEL * 4)
    return pl.pallas_call(
        _cross_router_kernel,
        out_shape=[jax.ShapeDtypeStruct((bsz, s, D_MODEL), F32), jax.ShapeDtypeStruct((bsz, s, D_MODEL), BF16),
                   jax.ShapeDtypeStruct((bsz, s, LANES), I32), jax.ShapeDtypeStruct((bsz, s, LANES), F32),
                   jax.ShapeDtypeStruct((bsz, s // tm, SUBLANES, LANES), I32)],
        grid=(bsz, s // tm),
        in_specs=[tok, tok, tok, tok, _resident(wao.shape, fixed), _resident(wout.shape, fixed),
                  _resident((1, D_MODEL), fixed), _resident(wcq.shape, fixed), _resident((1, MEM_HEAD_DIM), fixed),
                  memb, memb, _resident(wco.shape, fixed), _resident((1, D_MODEL), fixed),
                  _resident(wrh.shape, fixed), _resident(wrl.shape, fixed), _resident((1, LANES), fixed)],
        out_specs=[tok, tok, small, small,
                   pl.BlockSpec((None, None, SUBLANES, LANES), lambda b, i: (b, i, 0, 0))],
        compiler_params=_params(("parallel", "parallel"), est),
        name="mix_cross_router",
    )(x, o, ga, ml, wao, wout, gcx, wcq, cqg, kc, vc, wco, gffn, wrh, wrl, br)


MOE_CHUNK = SUBLANES
PACKED = D_MODEL // 2
U32 = jnp.uint32


def _pack_rows(x):
    hi = lax.bitcast_convert_type(x[:, :PACKED].astype(BF16).astype(F32), U32)
    lo = lax.bitcast_convert_type(x[:, PACKED:].astype(BF16).astype(F32), U32)
    return hi | lax.shift_right_logical(lo, jnp.full_like(lo, 16))


def _unpack_rows(p):
    a = lax.bitcast_convert_type(p & jnp.full_like(p, 0xFFFF0000), F32)
    b = lax.bitcast_convert_type(lax.shift_left(p, jnp.full_like(p, 16)), F32)
    return jnp.concatenate([a, b], axis=1).astype(BF16)


def _local_rows(tm):
    return TOP_K * tm + N_EXPERTS * MOE_CHUNK


def _segment_loop(t, lst_ref, gofs_ref, nch_ref, fn):
    quad = 4 * MOE_CHUNK

    def per_expert(e, carry):
        k = t * N_EXPERTS + e
        lst, gofs, nch = lst_ref[k], gofs_ref[k], nch_ref[k]
        nquad = lax.shift_right_logical(nch, 2)

        def per_quad(c, cc):
            fn(pl.multiple_of(lst + c * quad, MOE_CHUNK), pl.multiple_of(gofs + c * quad, MOE_CHUNK), quad, 0)
            return cc
        lax.fori_loop(0, nquad, per_quad, 0)
        done = nquad * quad

        @pl.when((nch & 2) == 2)
        def _():
            fn(pl.multiple_of(lst + done, MOE_CHUNK), pl.multiple_of(gofs + done, MOE_CHUNK), 2 * MOE_CHUNK, 1)

        @pl.when((nch & 1) == 1)
        def _():
            rest = done + (nch & 2) * MOE_CHUNK
            fn(pl.multiple_of(lst + rest, MOE_CHUNK), pl.multiple_of(gofs + rest, MOE_CHUNK), MOE_CHUNK, 1)
        return carry
    lax.fori_loop(0, N_EXPERTS, per_expert, 0)


def _repeat(count, fn):
    def body(c, carry):
        fn()
        return carry
    lax.fori_loop(0, count, body, 0)


MOE_WAIT_GROUP = 8


def _wait_chunks(count, wait_rows):
    _repeat(lax.shift_right_logical(count, MOE_WAIT_GROUP.bit_length() - 1),
            lambda: wait_rows(MOE_WAIT_GROUP * MOE_CHUNK))
    _repeat(count & (MOE_WAIT_GROUP - 1), lambda: wait_rows(MOE_CHUNK))


def _dispatch_kernel(gofs_ref, lst_ref, nch_ref, ntot_ref, tail_ref, tailn_ref, tailtot_ref,
                     hn_ref, eid_ref, lrow_ref, xs_hbm, before_ref, xloc_ref, zero_ref, sem, *, tm):
    t = pl.program_id(0)
    last = pl.num_programs(0) - 1
    slot = t % 2
    na = TOP_K * tm
    loc = xloc_ref.shape[1]

    @pl.when(t == 0)
    def _():
        r = lax.broadcasted_iota(I32, (na, na), 0)
        c = lax.broadcasted_iota(I32, (na, na), 1)
        before_ref[...] = (r < c).astype(BF16)
        zero_ref[...] = jnp.zeros_like(zero_ref)

    et = eid_ref[...].astype(F32).T
    e_row = jnp.concatenate([et[0:1], et[1:2]], axis=1)
    hit = lax.broadcasted_iota(I32, (LANES, na), 0).astype(F32) == e_row
    hit_b = hit.astype(BF16)
    rank = jnp.dot(hit_b, before_ref[...], preferred_element_type=F32)
    start = jnp.dot(lrow_ref[...].astype(BF16), hit_b, preferred_element_type=F32)[0:1] * MOE_CHUNK
    pos = (start + jnp.sum(jnp.where(hit, rank, 0.0), axis=0, keepdims=True)).astype(I32)
    r = lax.broadcasted_iota(I32, (loc, tm), 0)
    sel = ((r == pos[:, :tm]) | (r == pos[:, tm:])).astype(BF16)
    xloc_ref[slot] = _pack_rows(jnp.dot(sel, hn_ref[...], preferred_element_type=F32))

    def copy_out(local_row, global_row, s, rows=MOE_CHUNK):
        return pltpu.make_async_copy(xloc_ref.at[s, pl.ds(local_row, rows), :],
                                     xs_hbm.at[pl.ds(global_row, rows), :], sem.at[s])

    _segment_loop(t, lst_ref, gofs_ref, nch_ref,
                  lambda lr, gr, rows, prio: copy_out(lr, gr, slot, rows).start(priority=prio))

    @pl.when(t > 0)
    def _():
        _wait_chunks(ntot_ref[jnp.maximum(t - 1, 0)], lambda rows: copy_out(0, 0, 1 - slot, rows).wait())

    @pl.when(t == last)
    def _():
        _wait_chunks(ntot_ref[t], lambda rows: copy_out(0, 0, slot, rows).wait())

        def zero_out(global_row):
            return pltpu.make_async_copy(zero_ref, xs_hbm.at[pl.ds(global_row, MOE_CHUNK), :], sem.at[0])

        def per_expert(e, carry):
            def per_chunk(c, cc):
                zero_out(pl.multiple_of(tail_ref[e] + c * MOE_CHUNK, MOE_CHUNK)).start()
                return cc
            lax.fori_loop(0, tailn_ref[e], per_chunk, 0)
            return carry
        lax.fori_loop(0, N_EXPERTS, per_expert, 0)
        _repeat(tailtot_ref[0], lambda: zero_out(0).wait())


def _dispatch(plan, hn2d, eid2d, lrow, p_rows, tm):
    n = hn2d.shape[0]
    na = TOP_K * tm
    loc = _local_rows(tm)
    pre = (plan["gofs"], plan["lst"], plan["nch"], plan["ntot"], plan["tail"], plan["tailn"], plan["tailtot"])
    est = 2 * tm * D_MODEL * 2 + na * na * 2 + 2 * loc * PACKED * 4 + loc * D_MODEL * 8 + loc * tm * 4 \
        + 4 * LANES * na * 4
    grid_spec = pltpu.PrefetchScalarGridSpec(
        num_scalar_prefetch=len(pre),
        grid=(n // tm,),
        in_specs=[pl.BlockSpec((tm, D_MODEL), lambda t, *_: (t, 0)), pl.BlockSpec((tm, LANES), lambda t, *_: (t, 0)),
                  pl.BlockSpec((None, SUBLANES, LANES), lambda t, *_: (t, 0, 0))],
        out_specs=pl.BlockSpec(memory_space=pl.ANY),
        scratch_shapes=[pltpu.VMEM((na, na), BF16), pltpu.VMEM((2, loc, PACKED), U32),
                        pltpu.VMEM((MOE_CHUNK, PACKED), U32), pltpu.SemaphoreType.DMA((2,))],
    )
    return pl.pallas_call(
        functools.partial(_dispatch_kernel, tm=tm),
        out_shape=jax.ShapeDtypeStruct((p_rows, PACKED), U32),
        grid_spec=grid_spec,
        compiler_params=_params(("arbitrary",), est),
        name="moe_dispatch",
    )(*pre, hn2d, eid2d, lrow)


def _expert_kernel(blk_e_ref, blk_src_ref, blk_n_ref, xs_ref, wgu_ref, wd_ref, ys_ref, wgu_bf, wd_bf):
    i = pl.program_id(0)

    @pl.when(blk_n_ref[i] > 0)
    def _():
        @pl.when((i == 0) | (blk_e_ref[i] != blk_e_ref[jnp.maximum(i - 1, 0)]))
        def _():
            wgu_bf[...] = wgu_ref[0].astype(BF16)
            wd_bf[...] = wd_ref[0].astype(BF16)

        gu = jnp.dot(_unpack_rows(xs_ref[...]), wgu_bf[...], preferred_element_type=F32)
        act = (jax.nn.silu(gu[:, :EXPERT_FF]) * gu[:, EXPERT_FF:]).astype(BF16)
        ys_ref[...] = _pack_rows(jnp.dot(act, wd_bf[...], preferred_element_type=F32))


def _experts(plan, xs, wgu, wd, eb):
    n_blk = xs.shape[0] // eb
    rows = pl.BlockSpec((eb, PACKED), lambda i, be, bs, bn: (bs[i], 0))
    w_elems = wgu.shape[1] * wgu.shape[2] + wd.shape[1] * wd.shape[2]
    est = 2 * 2 * eb * PACKED * 4 + 2 * w_elems * 4 + w_elems * 2 + 6 * eb * D_MODEL * 4
    grid_spec = pltpu.PrefetchScalarGridSpec(
        num_scalar_prefetch=3,
        grid=(n_blk,),
        in_specs=[rows, pl.BlockSpec((1,) + wgu.shape[1:], lambda i, be, bs, bn: (be[i], 0, 0)),
                  pl.BlockSpec((1,) + wd.shape[1:], lambda i, be, bs, bn: (be[i], 0, 0))],
        out_specs=rows,
        scratch_shapes=[pltpu.VMEM(wgu.shape[1:], BF16), pltpu.VMEM(wd.shape[1:], BF16)],
    )
    return pl.pallas_call(
        _expert_kernel,
        out_shape=jax.ShapeDtypeStruct(xs.shape, U32),
        grid_spec=grid_spec,
        compiler_params=_params(("arbitrary",), est),
        name="experts",
    )(plan["blk_e"], plan["blk_src"], plan["blk_n"], xs, wgu, wd)


def _combine_kernel(gofs_ref, lst_ref, nch_ref, ntot_ref, x2_ref, ew_ref, eid_ref, lrow_ref, ys_hbm, o_ref,
                    before_ref, yloc_ref, sem, *, tm):
    t = pl.program_id(0)
    last = pl.num_programs(0) - 1
    slot = t % 2
    na = TOP_K * tm
    loc = yloc_ref.shape[1]

    def copy_in(local_row, global_row, s, rows=MOE_CHUNK):
        return pltpu.make_async_copy(ys_hbm.at[pl.ds(global_row, rows), :],
                                     yloc_ref.at[s, pl.ds(local_row, rows), :], sem.at[s])

    def fetch(tile, s):
        _segment_loop(tile, lst_ref, gofs_ref, nch_ref,
                      lambda lr, gr, rows, prio: copy_in(lr, gr, s, rows).start(priority=prio))

    @pl.when(t == 0)
    def _():
        r = lax.broadcasted_iota(I32, (na, na), 0)
        c = lax.broadcasted_iota(I32, (na, na), 1)
        before_ref[...] = (c < r).astype(BF16)
        yloc_ref[...] = jnp.zeros_like(yloc_ref)
        fetch(0, 0)

    @pl.when(t < last)
    def _():
        fetch(t + 1, 1 - slot)

    _wait_chunks(ntot_ref[t], lambda rows: copy_in(0, 0, slot, rows).wait())

    eid = eid_ref[...]
    lane = lax.broadcasted_iota(I32, (tm, LANES), 1)
    hit = jnp.concatenate([lane == eid[:, 0:1], lane == eid[:, 1:2]], axis=0)
    rank = jnp.dot(before_ref[...], hit.astype(BF16), preferred_element_type=F32)
    start = lrow_ref[0:1, :] * MOE_CHUNK
    pos = jnp.sum(jnp.where(hit, rank + start, 0.0), axis=1, keepdims=True).astype(I32)
    col = lax.broadcasted_iota(I32, (tm, loc), 1)
    ew = ew_ref[...]
    sel = jnp.where(col == pos[:tm], ew[:, 0:1], 0.0) + jnp.where(col == pos[tm:], ew[:, 1:2], 0.0)
    o_ref[...] = x2_ref[...] + jnp.dot(sel.astype(BF16), _unpack_rows(yloc_ref[slot]), preferred_element_type=F32)


def _combine(plan, x2, ew, eid2d, lrow, ys, tm):
    n = x2.shape[0]
    na = TOP_K * tm
    loc = _local_rows(tm)
    pre = (plan["gofs"], plan["lst"], plan["nch"], plan["ntot"])
    tok = pl.BlockSpec((tm, D_MODEL), lambda t, *_: (t, 0))
    small = pl.BlockSpec((tm, LANES), lambda t, *_: (t, 0))
    est = 2 * 2 * tm * D_MODEL * 4 + na * na * 2 + 2 * loc * PACKED * 4 + loc * D_MODEL * 10 + 2 * tm * loc * 4 \
        + 3 * tm * D_MODEL * 4
    grid_spec = pltpu.PrefetchScalarGridSpec(
        num_scalar_prefetch=len(pre),
        grid=(n // tm,),
        in_specs=[tok, small, small, pl.BlockSpec((None, SUBLANES, LANES), lambda t, *_: (t, 0, 0)),
                  pl.BlockSpec(memory_space=pl.ANY)],
        out_specs=tok,
        scratch_shapes=[pltpu.VMEM((na, na), BF16), pltpu.VMEM((2, loc, PACKED), U32),
                        pltpu.SemaphoreType.DMA((2,))],
    )
    return pl.pallas_call(
        functools.partial(_combine_kernel, tm=tm),
        out_shape=jax.ShapeDtypeStruct((n, D_MODEL), F32),
        grid_spec=grid_spec,
        compiler_params=_params(("arbitrary",), est),
        name="moe_combine",
    )(*pre, x2, ew, eid2d, lrow, ys)


def _rope_table(positions):
    inv_freq = jnp.exp(-math.log(ROPE_THETA) * jnp.arange(ROT_HALF, dtype=F32) / ROT_HALF)
    ang = positions.astype(F32).reshape(-1, 1) * inv_freq
    seg = jnp.concatenate([jnp.cos(ang), jnp.sin(ang), jnp.zeros((ang.shape[0], HEAD_DIM - ROT_DIMS), F32)], axis=1)
    return jnp.tile(seg, (1, LANES // HEAD_DIM))


def _plan_rows(n_tok, tm, eb):
    worst = n_tok * TOP_K + (n_tok // tm) * N_EXPERTS * (MOE_CHUNK - 1)
    return (worst + eb - 1) // eb * eb + N_EXPERTS * eb


def _moe_plan(hist, p_rows, eb):
    n_tiles = hist.shape[0]
    seg = (hist + MOE_CHUNK - 1) // MOE_CHUNK * MOE_CHUNK
    tot = jnp.sum(seg, axis=0)
    region = (tot + eb - 1) // eb * eb
    pend = jnp.cumsum(region)
    pstart = pend - region
    gofs = pstart[None, :] + jnp.cumsum(seg, axis=0) - seg
    lst = jnp.cumsum(seg, axis=1) - seg
    nch = seg // MOE_CHUNK
    n_blk = p_rows // eb
    blk_first = jnp.arange(n_blk, dtype=I32) * eb
    blk_e = jnp.minimum(jnp.sum(blk_first[:, None] >= pend[None, :], axis=1), N_EXPERTS - 1).astype(I32)
    blk_n = jnp.clip(pstart[blk_e] + tot[blk_e] - blk_first, 0, eb)
    blk_src = jnp.minimum(jnp.arange(n_blk, dtype=I32), jnp.maximum(pend[-1] // eb - 1, 0))
    tailn = (region - tot) // MOE_CHUNK
    lrow = jnp.pad((lst // MOE_CHUNK).astype(F32), ((0, 0), (0, LANES - N_EXPERTS)))
    flat = lambda a: a.reshape(-1).astype(I32)
    plan = dict(gofs=flat(gofs), lst=flat(lst), nch=flat(nch), ntot=flat(jnp.sum(nch, axis=1)),
                tail=flat(pstart + tot), tailn=flat(tailn), tailtot=flat(jnp.sum(tailn)),
                blk_e=blk_e, blk_src=flat(blk_src), blk_n=flat(blk_n))
    return plan, jnp.broadcast_to(lrow[:, None, :], (n_tiles, SUBLANES, LANES))


def _tile(n, pref):
    t = min(n, pref)
    assert n % t == 0, (n, pref)
    return t


def kernel(x, mem, positions, norm_mix, w_in, conv_w, conv_b, lru_wa, lru_ba, lru_wx, lru_bx, lru_lambda, w_lru_o, q_norm, k_norm, lambda_q1, lambda_k1, lambda_q2, lambda_k2, subln, w_attn_o, w_out, norm_cx, norm_mem, w_cq, w_ckv, cq_norm, ck_norm, w_co, norm_ffn, w_group, b_group, w_router, b_router, w_gate_up, w_down):
    bsz, s, d = x.shape
    assert d == D_MODEL and w_in.shape[-1] == N_PROJ * D_MODEL
    n = bsz * s
    depth = w_in.shape[0]
    ts = _tile(s, TOKEN_TILE)
    tq = _tile(s, TOKEN_TILE)
    assert tq % CHUNK == 0 and ts % SUBLANES == 0
    eb = _tile(n, EXPERT_BLOCK)
    rope = _rope_table(positions)
    row = lambda v: v.reshape(1, -1).astype(F32)
    rep = LANES // HEAD_DIM

    for layer in range(depth):
        lambda_init = 0.8 - 0.6 * math.exp(-0.3 * layer)
        wax = (0.5 * jnp.concatenate([lru_wa[layer], lru_wx[layer]], axis=-1)).astype(BF16)
        bax = 0.5 * jnp.stack([lru_ba[layer], lru_bx[layer]]).astype(F32)
        lru_params = (conv_w[layer].astype(F32), row(conv_b[layer]), wax, bax, row(lru_lambda[layer]),
                      w_lru_o[layer].astype(BF16))
        ml, q, k, vt, ga = _in_proj(
            x.reshape(n, d), row(norm_mix[layer]), w_in[layer].astype(BF16),
            jnp.tile(row(q_norm[layer]), (1, rep)), jnp.tile(row(k_norm[layer]), (1, rep)), rope, lru_params, ts, s)
        seq = lambda a: a.reshape(bsz, s, d)

        lam = (jnp.exp(jnp.sum(lambda_q1[layer].astype(F32) * lambda_k1[layer].astype(F32)))
               - jnp.exp(jnp.sum(lambda_q2[layer].astype(F32) * lambda_k2[layer].astype(F32))) + lambda_init)
        o = _diff_attn(lam.reshape(1, 1), seq(q), seq(k), vt, subln[layer].astype(F32).reshape(V_DIM, 1),
                       lambda_init, tq)

        kc, vc = _mem_kv(mem, row(norm_mem[layer]), w_ckv[layer].astype(BF16), row(ck_norm[layer]))
        w_r = jnp.concatenate([w_group[layer], w_router[layer],
                               jnp.zeros((d, LANES - N_GROUPS - N_EXPERTS), F32)], axis=1).astype(F32)
        b_r = jnp.concatenate([b_group[layer], b_router[layer],
                               jnp.zeros((LANES - N_GROUPS - N_EXPERTS,), F32)]).reshape(1, LANES).astype(F32)
        wrh, wrl = _split_bf16(w_r)
        x2, hn, eid, ew, hist = _cross_router(
            x, o, seq(ga), seq(ml), w_attn_o[layer].astype(BF16), w_out[layer].astype(BF16),
            row(norm_cx[layer]), w_cq[layer].astype(BF16), row(cq_norm[layer]), kc, vc,
            w_co[layer].astype(BF16), row(norm_ffn[layer]), wrh, wrl, b_r, ts)

        p_rows = _plan_rows(n, ts, eb)
        plan, lrow = _moe_plan(hist[:, :, 0, :N_EXPERTS].reshape(n // ts, N_EXPERTS), p_rows, eb)
        eid2d = eid.reshape(n, LANES)
        xs = _dispatch(plan, hn.reshape(n, d), eid2d, lrow, p_rows, ts)
        ys = _experts(plan, xs, w_gate_up[layer], w_down[layer], eb)
        x = _combine(plan, x2.reshape(n, d), ew.reshape(n, LANES), eid2d, lrow, ys, ts).reshape(bsz, s, d)
    return x
```
